```python
import math, functools
import jax, jax.numpy as jnp
from jax import lax
import numpy as np

D_MODEL = 1024
BATCH = 2
SEQ = 8192
DEPTH = 2
DEC_BATCH = 128
DEC_SEQ = 4
PAST_LEN = 8192
PAGE_SIZE = 128

MLA_HEADS = 8
MLA_NOPE = 64
MLA_ROPE = 32
MLA_V = 64
MLA_Q_LORA = 384
MLA_KV_LORA = 256
MLA_SCALE = (MLA_NOPE + MLA_ROPE) ** -0.5
ROPE_BASE = 10000.0
Q_BLOCK = 128
HG_HEADS = 4
HG_DK = 128
HG_DV = 128
GLA_HEADS = 4
GLA_DK = 64
GLA_DV = 128
GLA_GATE_RANK = 16
GLA_GATE_NORMALIZER = 16.0
CHUNK = 64
N_EXPERTS = 32
TOP_K = 4
D_EXPERT = 1024
SWIGLU_LIMIT = 7.0
SWIGLU_ALPHA = 1.702
MOE_BLOCK = 128
EPS = 1e-6
NEG_INF = -1e30
TINY = 1e-30

MLA_WIDTH = MLA_HEADS * MLA_V
HG_K_WIDTH = HG_HEADS * HG_DK
HG_V_WIDTH = HG_HEADS * HG_DV
GLA_K_WIDTH = GLA_HEADS * GLA_DK
GLA_V_WIDTH = GLA_HEADS * GLA_DV
N_BRANCH = 3
IN_SPLITS = (MLA_Q_LORA, MLA_KV_LORA, MLA_ROPE,
             HG_K_WIDTH, HG_K_WIDTH, HG_V_WIDTH, HG_V_WIDTH,
             GLA_K_WIDTH, GLA_K_WIDTH, GLA_V_WIDTH, GLA_V_WIDTH, GLA_GATE_RANK,
             N_BRANCH * D_MODEL)
IN_WIDTH = sum(IN_SPLITS)

kernel_name = 'hybrid_mla_hgrn2_gla_moe_adaln_step'


def rmsnorm(x, w):
    xf = x.astype(jnp.float32)
    y = xf * lax.rsqrt(jnp.mean(xf * xf, axis=-1, keepdims=True) + EPS)
    return (y * w.astype(jnp.float32)).astype(x.dtype)


def rope(x, pos):
    half = MLA_ROPE // 2
    inv = ROPE_BASE ** (-jnp.arange(half, dtype=jnp.float32) / half)
    ang = pos.astype(jnp.float32)[:, None] * inv[None, :]
    ang = ang.reshape(ang.shape[:1] + (1,) * (x.ndim - 3) + (half,))
    cos, sin = jnp.cos(ang), jnp.sin(ang)
    xf = x.astype(jnp.float32)
    x1, x2 = xf[..., :half], xf[..., half:]
    return jnp.concatenate([x1 * cos - x2 * sin, x2 * cos + x1 * sin], axis=-1).astype(x.dtype)


def gated_linear_scan(q, k, v, log_a, s0):
    B, T, H, K = q.shape
    V = v.shape[-1]
    C = CHUNK if T % CHUNK == 0 else T
    n = T // C
    f32 = jnp.float32

    def chunks(a):
        return a.astype(f32).reshape(B, n, C, H, a.shape[-1]).transpose(1, 0, 3, 2, 4)

    causal = jnp.tril(jnp.ones((C, C), dtype=bool))[:, :, None]

    def step(S, inp):
        qc, kc, vc, gc = inp
        b = jnp.cumsum(gc, axis=2)
        o = jnp.einsum('bhtk,bhkv->bhtv', qc * jnp.exp(b), S)
        diff = b[:, :, :, None, :] - b[:, :, None, :, :]
        rel = jnp.where(causal, jnp.exp(jnp.where(causal, diff, 0.0)), 0.0)
        att = jnp.einsum('bhtk,bhsk,bhtsk->bhts', qc, kc, rel)
        o = o + jnp.einsum('bhts,bhsv->bhtv', att, vc)
        b_end = b[:, :, -1:, :]
        S = jnp.exp(b_end[:, :, 0, :, None]) * S + jnp.einsum('bhsk,bhsv->bhkv', kc * jnp.exp(b_end - b), vc)
        return S, o

    s_T, o = lax.scan(step, s0.astype(f32), (chunks(q), chunks(k), chunks(v), chunks(log_a)))
    o = o.transpose(1, 0, 3, 2, 4).reshape(B, T, H, V)
    return o.astype(v.dtype), s_T.astype(s0.dtype)


def hgrn2(q, f_logit, i, g, lb, norm_w, s0):
    B, T, _ = q.shape
    log_f = jnp.logaddexp(jnp.log(jnp.maximum(lb, TINY)),
                          jnp.log1p(-lb) + jax.nn.log_sigmoid(f_logit.astype(jnp.float32)))
    k = -jnp.expm1(log_f)
    heads = lambda a, d: a.reshape(B, T, HG_HEADS, d)
    o, s = gated_linear_scan(heads(jax.nn.silu(q) * HG_DK ** -0.5, HG_DK), heads(k, HG_DK),
                             heads(i, HG_DV), heads(log_f, HG_DK), s0)
    o = rmsnorm(o.reshape(B, T, HG_V_WIDTH), norm_w) * jax.nn.silu(g)
    return o, s


def gla(q, k, v, g, g_low, w_gate, b_gate, norm_w, s0):
    B, T, _ = q.shape
    log_a = jax.nn.log_sigmoid((g_low @ w_gate + b_gate).astype(jnp.float32)) / GLA_GATE_NORMALIZER
    heads = lambda a, d: a.reshape(B, T, GLA_HEADS, d)
    o, s = gated_linear_scan(heads(q * GLA_DK ** -0.5, GLA_DK), heads(k, GLA_DK),
                             heads(v, GLA_DV), heads(log_a, GLA_DK), s0)
    o = rmsnorm(o, norm_w) * jax.nn.silu(heads(g, GLA_DV))
    return o.reshape(B, T, GLA_V_WIDTH), s


def mla_attend_prompt(q_nope, q_pe, latent, k_pe, w_uk, w_uv):
    B, S, H, _ = q_nope.shape
    k_nope = (latent @ w_uk).reshape(B, S, H, MLA_NOPE)
    v = (latent @ w_uv).reshape(B, S, H, MLA_V)
    qb = min(Q_BLOCK, S)
    nb = S // qb
    blocks = lambda a: a.reshape((B, nb, qb) + a.shape[2:]).swapaxes(0, 1)
    kpos = jnp.arange(S)

    def one_block(args):
        qn, qp, start = args
        s = jnp.einsum('bqhd,bkhd->bhqk', qn, k_nope) + jnp.einsum('bqhr,bkr->bhqk', qp, k_pe)
        qpos = start + jnp.arange(qb)
        s = jnp.where(kpos[None, :] <= qpos[:, None], s.astype(jnp.float32) * MLA_SCALE, NEG_INF)
        p = jax.nn.softmax(s, axis=-1).astype(v.dtype)
        return jnp.einsum('bhqk,bkhd->bqhd', p, v)

    o = lax.map(one_block, (blocks(q_nope), blocks(q_pe), jnp.arange(nb) * qb))
    return o.swapaxes(0, 1).reshape(B, S, H * MLA_V)


def mla_attend_sample(cache_latent, cache_k_rope, page_table, l, q_nope, q_pe, latent, k_pe, w_uk, w_uv):
    Bd, T, H, _ = q_nope.shape
    lat_past = cache_latent[l, page_table].reshape(Bd, -1, MLA_KV_LORA)
    kpe_past = cache_k_rope[l, page_table].reshape(Bd, -1, MLA_ROPE)
    P = lat_past.shape[1]
    q_lat = jnp.einsum('bthd,chd->bthc', q_nope, w_uk.reshape(MLA_KV_LORA, H, MLA_NOPE))
    s_past = jnp.einsum('bthc,bsc->bhts', q_lat, lat_past) + jnp.einsum('bthr,bsr->bhts', q_pe, kpe_past)
    s_new = jnp.einsum('bthc,buc->bhtu', q_lat, latent) + jnp.einsum('bthr,bur->bhtu', q_pe, k_pe)
    s_new = jnp.where(jnp.tril(jnp.ones((T, T), dtype=bool)), s_new.astype(jnp.float32) * MLA_SCALE, NEG_INF)
    s = jnp.concatenate([s_past.astype(jnp.float32) * MLA_SCALE, s_new], axis=-1)
    p = jax.nn.softmax(s, axis=-1).astype(latent.dtype)
    o_lat = (jnp.einsum('bhts,bsc->bthc', p[..., :P], lat_past)
             + jnp.einsum('bhtu,buc->bthc', p[..., P:], latent))
    o = jnp.einsum('bthc,chd->bthd', o_lat, w_uv.reshape(MLA_KV_LORA, H, MLA_V))
    return o.reshape(Bd, T, H * MLA_V)


def token_mixer(h, pos, attend, lb, hg_s0, gla_s0, w_in, q_norm_w, w_uq, kv_norm_w, w_uk, w_uv,
                hg_norm_w, gla_w_gate, gla_b_gate, gla_norm_w, w_br_mla, w_br_hg, w_br_gla, w_out):
    B, T, _ = h.shape
    idx = np.cumsum(IN_SPLITS)[:-1].tolist()
    (cq, ckv, kr, hq, hf, hi, hgate, gq, gk, gv, ggate, glow, br) = jnp.split(h @ w_in, idx, axis=-1)
    qh = (rmsnorm(cq, q_norm_w) @ w_uq).reshape(B, T, MLA_HEADS, MLA_NOPE + MLA_ROPE)
    q_nope = qh[..., :MLA_NOPE]
    q_pe = rope(qh[..., MLA_NOPE:], pos)
    latent = rmsnorm(ckv, kv_norm_w)
    k_pe = rope(kr, pos)
    y_mla = attend(q_nope, q_pe, latent, k_pe, w_uk, w_uv)
    y_hg, s_hg = hgrn2(hq, hf, hi, hgate, lb, hg_norm_w, hg_s0)
    y_gla, s_gla = gla(gq, gk, gv, ggate, glow, gla_w_gate, gla_b_gate, gla_norm_w, gla_s0)
    g_mla, g_hg, g_gla = jnp.split(jax.nn.sigmoid(br), N_BRANCH, axis=-1)
    merged = g_mla * (y_mla @ w_br_mla) + g_hg * (y_hg @ w_br_hg) + g_gla * (y_gla @ w_br_gla)
    return merged @ w_out, latent, k_pe, s_hg, s_gla


def moe(h, router_w, router_b, w_gate_up, b_gate_up, w_down, b_down):
    B, T, D = h.shape
    xt = h.reshape(-1, D)
    N = xt.shape[0]
    NK = N * TOP_K
    logits = (xt @ router_w + router_b).astype(jnp.float32)
    top_v, top_e = lax.top_k(logits, TOP_K)
    wts = jax.nn.softmax(top_v, axis=-1)
    flat_e = top_e.reshape(-1).astype(jnp.int32)
    flat_tok = jnp.arange(NK, dtype=jnp.int32) // TOP_K
    order = jnp.argsort(flat_e)
    se, stok, sw = flat_e[order], flat_tok[order], wts.reshape(-1)[order]
    counts = jnp.zeros((N_EXPERTS,), jnp.int32).at[flat_e].add(1)
    padded = (counts + MOE_BLOCK - 1) // MOE_BLOCK * MOE_BLOCK
    pad_end = jnp.cumsum(padded)
    pad_start = pad_end - padded
    start = jnp.cumsum(counts) - counts
    dest = pad_start[se] + jnp.arange(NK, dtype=jnp.int32) - start[se]
    n_blocks = (NK + N_EXPERTS * (MOE_BLOCK - 1) + MOE_BLOCK - 1) // MOE_BLOCK
    slot_tok = jnp.zeros((n_blocks * MOE_BLOCK,), jnp.int32).at[dest].set(stok)
    block_e = jnp.minimum(jnp.searchsorted(pad_end, jnp.arange(n_blocks, dtype=jnp.int32) * MOE_BLOCK,
                                           side='right'), N_EXPERTS - 1).astype(jnp.int32)
    xb = xt[slot_tok].reshape(n_blocks, MOE_BLOCK, D)

    def expert_block(args):
        xi, e = args
        gu = xi @ w_gate_up[e] + b_gate_up[e]
        gate = jnp.minimum(gu[:, :D_EXPERT], SWIGLU_LIMIT)
        up = jnp.clip(gu[:, D_EXPERT:], -SWIGLU_LIMIT, SWIGLU_LIMIT)
        act = (up + 1) * gate * jax.nn.sigmoid(SWIGLU_ALPHA * gate)
        return act @ w_down[e] + b_down[e]

    yb = lax.map(expert_block, (xb, block_e)).reshape(-1, D)
    y = jnp.zeros_like(xt).at[stok].add(yb[dest] * sw[:, None].astype(xt.dtype))
    return y.reshape(B, T, D)


def setup_inputs(seed: int = 0) -> dict:
    key = jax.random.key(seed)
    ks = iter(jax.random.split(key, 48))
    nrm = lambda shape, scale: jax.random.normal(next(ks), shape, jnp.float32) * scale
    gain = lambda shape: 1.0 + nrm(shape, 0.02)
    n_pages = PAST_LEN // PAGE_SIZE
    n_pool = (DEC_BATCH * n_pages * 5) // 4
    page_table = jax.random.permutation(next(ks), n_pool)[:DEC_BATCH * n_pages]
    page_table = page_table.reshape(DEC_BATCH, n_pages).astype(jnp.int32)
    return {
        'x_prompt': nrm((BATCH, SEQ, D_MODEL), 1.0),
        'x_sample': nrm((DEC_BATCH, DEC_SEQ, D_MODEL), 1.0),
        'cache_latent': nrm((DEPTH, n_pool, PAGE_SIZE, MLA_KV_LORA), 1.0),
        'cache_k_rope': nrm((DEPTH, n_pool, PAGE_SIZE, MLA_ROPE), 1.0),
        'state_hgrn': nrm((DEPTH, DEC_BATCH, HG_HEADS, HG_DK, HG_DV), 0.5),
        'state_gla': nrm((DEPTH, DEC_BATCH, GLA_HEADS, GLA_DK, GLA_DV), 1.0),
        'page_table': page_table,
        'c_prompt': nrm((BATCH, D_MODEL), 1.0),
        'c_sample': nrm((DEC_BATCH, D_MODEL), 1.0),
        'norm_mix_w': gain((DEPTH, D_MODEL)),
        'norm_ffn_w': gain((DEPTH, D_MODEL)),
        'final_norm_w': gain((D_MODEL,)),
        'w_ada': nrm((DEPTH, D_MODEL, 6 * D_MODEL), 0.5 * D_MODEL ** -0.5),
        'b_ada': nrm((DEPTH, 6 * D_MODEL), 0.02),
        'w_in': nrm((DEPTH, D_MODEL, IN_WIDTH), D_MODEL ** -0.5),
        'mla_q_norm_w': gain((DEPTH, MLA_Q_LORA)),
        'mla_w_uq': nrm((DEPTH, MLA_Q_LORA, MLA_HEADS * (MLA_NOPE + MLA_ROPE)), MLA_Q_LORA ** -0.5),
        'mla_kv_norm_w': gain((DEPTH, MLA_KV_LORA)),
        'mla_w_uk': nrm((DEPTH, MLA_KV_LORA, MLA_HEADS * MLA_NOPE), MLA_KV_LORA ** -0.5),
        'mla_w_uv': nrm((DEPTH, MLA_KV_LORA, MLA_HEADS * MLA_V), MLA_KV_LORA ** -0.5),
        'hgrn_lower_bounds': 1.0 + nrm((DEPTH, HG_K_WIDTH), 0.1),
        'hgrn_norm_w': gain((DEPTH, HG_V_WIDTH)),
        'gla_w_gate': nrm((DEPTH, GLA_GATE_RANK, GLA_K_WIDTH), GLA_GATE_RANK ** -0.5),
        'gla_b_gate': nrm((DEPTH, GLA_K_WIDTH), 0.1),
        'gla_norm_w': gain((DEPTH, GLA_DV)),
        'w_branch_mla': nrm((DEPTH, MLA_WIDTH, D_MODEL), MLA_WIDTH ** -0.5),
        'w_branch_hgrn': nrm((DEPTH, HG_V_WIDTH, D_MODEL), HG_V_WIDTH ** -0.5),
        'w_branch_gla': nrm((DEPTH, GLA_V_WIDTH, D_MODEL), GLA_V_WIDTH ** -0.5),
        'w_out': nrm((DEPTH, D_MODEL, D_MODEL), D_MODEL ** -0.5),
        'router_w': nrm((DEPTH, D_MODEL, N_EXPERTS), D_MODEL ** -0.5),
        'router_b': nrm((DEPTH, N_EXPERTS), 0.01),
        'w_gate_up': nrm((DEPTH, N_EXPERTS, D_MODEL, 2 * D_EXPERT), D_MODEL ** -0.5),
        'b_gate_up': nrm((DEPTH, N_EXPERTS, 2 * D_EXPERT), 0.01),
        'w_down': nrm((DEPTH, N_EXPERTS, D_EXPERT, D_MODEL), D_EXPERT ** -0.5),
        'b_down': nrm((DEPTH, N_EXPERTS, D_MODEL), 0.01),
    }


def reference(x_prompt, x_sample, cache_latent, cache_k_rope, state_hgrn, state_gla, page_table,
              c_prompt, c_sample, norm_mix_w, norm_ffn_w, final_norm_w, w_ada, b_ada, w_in,
              mla_q_norm_w, mla_w_uq, mla_kv_norm_w, mla_w_uk, mla_w_uv, hgrn_lower_bounds,
              hgrn_norm_w, gla_w_gate, gla_b_gate, gla_norm_w, w_branch_mla, w_branch_hgrn,
              w_branch_gla, w_out, router_w, router_b, w_gate_up, b_gate_up, w_down, b_down):
    lbs = jax.nn.softmax(hgrn_lower_bounds.astype(jnp.float32), axis=0)
    lbs = jnp.cumsum(lbs, axis=0) - lbs[0]

    def run(x, c, pos, attend, hg0, gla0):
        lat_rows, kpe_rows, hg_fin, gla_fin = [], [], [], []
        for l in range(DEPTH):
            mod = jax.nn.silu(c) @ w_ada[l] + b_ada[l]
            sh1, sc1, g1, sh2, sc2, g2 = jnp.split(mod[:, None, :], 6, axis=-1)
            h = rmsnorm(x, norm_mix_w[l]) * (1 + sc1) + sh1
            y, lat, kpe, s_hg, s_gla = token_mixer(
                h, pos, functools.partial(attend, l), lbs[l], hg0[l], gla0[l],
                w_in[l], mla_q_norm_w[l], mla_w_uq[l], mla_kv_norm_w[l], mla_w_uk[l], mla_w_uv[l],
                hgrn_norm_w[l], gla_w_gate[l], gla_b_gate[l], gla_norm_w[l],
                w_branch_mla[l], w_branch_hgrn[l], w_branch_gla[l], w_out[l])
            x = x + g1 * y
            h = rmsnorm(x, norm_ffn_w[l]) * (1 + sc2) + sh2
            x = x + g2 * moe(h, router_w[l], router_b[l], w_gate_up[l], b_gate_up[l], w_down[l], b_down[l])
            lat_rows.append(lat)
            kpe_rows.append(kpe)
            hg_fin.append(s_hg)
            gla_fin.append(s_gla)
        return (rmsnorm(x, final_norm_w), jnp.stack(lat_rows), jnp.stack(kpe_rows),
                jnp.stack(hg_fin), jnp.stack(gla_fin))

    def attend_prompt(l, q_nope, q_pe, latent, k_pe, w_uk, w_uv):
        return mla_attend_prompt(q_nope, q_pe, latent, k_pe, w_uk, w_uv)

    def attend_sample(l, q_nope, q_pe, latent, k_pe, w_uk, w_uv):
        return mla_attend_sample(cache_latent, cache_k_rope, page_table, l,
                                 q_nope, q_pe, latent, k_pe, w_uk, w_uv)

    Bp, S = x_prompt.shape[:2]
    pos_p = jnp.arange(S, dtype=jnp.int32)
    hg0_p = jnp.zeros((DEPTH, Bp, HG_HEADS, HG_DK, HG_DV), x_prompt.dtype)
    gla0_p = jnp.zeros((DEPTH, Bp, GLA_HEADS, GLA_DK, GLA_DV), x_prompt.dtype)
    y_prompt, lat_p, kpe_p, hg_p, gla_p = run(x_prompt, c_prompt, pos_p, attend_prompt, hg0_p, gla0_p)

    past_len = page_table.shape[1] * cache_latent.shape[2]
    pos_s = past_len + jnp.arange(x_sample.shape[1], dtype=jnp.int32)
    y_sample, lat_s, kpe_s, hg_s, gla_s = run(x_sample, c_sample, pos_s, attend_sample, state_hgrn, state_gla)

    return (y_prompt, y_sample, lat_p, kpe_p, hg_p, gla_p, lat_s, kpe_s, hg_s, gla_s)
```

```python
import functools

import numpy as np
import jax
import jax.numpy as jnp
from jax import lax
from jax.experimental import pallas as pl
from jax.experimental.pallas import tpu as pltpu

F32 = jnp.float32
BF16 = jnp.bfloat16

D_MODEL = 1024
DEPTH = 2
PAGE_SIZE = 128
MLA_HEADS = 8
MLA_NOPE = 64
MLA_ROPE = 32
MLA_V = 64
MLA_Q_LORA = 384
MLA_KV_LORA = 256
MLA_SCALE = (MLA_NOPE + MLA_ROPE) ** -0.5
ROPE_BASE = 10000.0
HG_HEADS = 4
HG_DK = 128
HG_DV = 128
GLA_HEADS = 4
GLA_DK = 64
GLA_DV = 128
GLA_GATE_RANK = 16
GLA_GATE_NORMALIZER = 16.0
N_EXPERTS = 32
TOP_K = 4
D_EXPERT = 1024
SWIGLU_LIMIT = 7.0
SWIGLU_ALPHA = 1.702
EPS = 1e-6
NEG_INF = -1e30
TINY = 1e-30

IN_SPLITS = (MLA_Q_LORA, MLA_KV_LORA, MLA_ROPE, 512, 512, 512, 512, 256, 256, 512, 512, GLA_GATE_RANK,
             3 * D_MODEL)

LANE = 128
HEAD_SLAB = 128
VMEM_LIMIT = 56 * 1024 * 1024

Z_WIDTH = 8192
Z_CQ, Z_KR, Z_CKV, Z_KRR, Z_GLOW = 0, 384, 512, 768, 896
Z_HQ, Z_HF, Z_HI, Z_HGATE = 1024, 1536, 2048, 2560
Z_GQ, Z_GK, Z_GV, Z_GGATE, Z_BR = 3072, 3584, 4096, 4608, 5120

SCAN_CHUNK = 64
SCAN_SUB = 16
MOE_TM = 256


def _cparams(sem, vmem=VMEM_LIMIT):
    return pltpu.CompilerParams(dimension_semantics=sem, vmem_limit_bytes=vmem)


def _dot(a, b):
    return jnp.dot(a, b, preferred_element_type=F32)


def _dot_nt(a, b):
    return lax.dot_general(a, b, (((1,), (1,)), ((), ())), preferred_element_type=F32)


def _dot_tn(a, b):
    return lax.dot_general(a, b, (((0,), (0,)), ((), ())), preferred_element_type=F32)


def _rms(x):
    return x * lax.rsqrt(jnp.mean(x * x, axis=-1, keepdims=True) + EPS)


def _log_sigmoid(x):
    return jnp.minimum(x, 0.0) - jnp.log1p(jnp.exp(-jnp.abs(x)))


def _adaln_body(c_ref, w_ref, b_ref, o_ref):
    c = c_ref[...]
    a = (c * jax.nn.sigmoid(c)).astype(BF16)
    o_ref[0] = _dot(a, w_ref[0].astype(BF16)) + b_ref[0]


def _adaln(c_all, w_ada, b_ada):
    rows = c_all.shape[0]
    tn = 1536
    n_out = w_ada.shape[-1]
    return pl.pallas_call(
        _adaln_body,
        grid=(DEPTH, n_out // tn),
        in_specs=[pl.BlockSpec((rows, D_MODEL), lambda l, j: (0, 0)),
                  pl.BlockSpec((1, D_MODEL, tn), lambda l, j: (l, 0, j)),
                  pl.BlockSpec((1, 1, tn), lambda l, j: (l, 0, j))],
        out_specs=pl.BlockSpec((1, rows, tn), lambda l, j: (l, 0, j)),
        out_shape=jax.ShapeDtypeStruct((DEPTH, rows, n_out), F32),
        compiler_params=_cparams(("parallel", "parallel")),
        name="adaln",
    )(c_all, w_ada, b_ada.reshape(DEPTH, 1, n_out))


class _Group:
    def __init__(self, batch, time, per_token):
        self.batch, self.time, self.per_token = batch, time, per_token
        self.n_tok = batch * time

    def tile(self, tm):
        tm = min(tm, self.n_tok)
        assert (self.n_tok if self.per_token else self.time) % tm == 0
        return tm

    def mod(self, m, tm):
        if self.per_token:
            arr = jnp.repeat(m, self.time, axis=0)[None]
            return arr, pl.BlockSpec((1, tm, m.shape[-1]), lambda *g: (0, g[0], 0))
        arr = m[:, None, :]
        per = self.time // tm
        return arr, pl.BlockSpec((1, 1, m.shape[-1]), lambda *g: (g[0] // per, 0, 0))


def _in_proj_body(x_ref, nw_ref, sc_ref, sh_ref, w_ref, z_ref, h_scr):
    @pl.when(pl.program_id(1) == 0)
    def _():
        h = _rms(x_ref[...]) * nw_ref[...]
        h_scr[...] = (h * (1.0 + sc_ref[0]) + sh_ref[0]).astype(BF16)

    z_ref[...] = _dot(h_scr[...], w_ref[...])


def _in_proj(grp, x, norm_w, sc, sh, w_pad):
    tm, tn = grp.tile(1024), 1024
    sc_a, sc_s = grp.mod(sc, tm)
    sh_a, sh_s = grp.mod(sh, tm)
    return pl.pallas_call(
        _in_proj_body,
        grid=(grp.n_tok // tm, Z_WIDTH // tn),
        in_specs=[pl.BlockSpec((tm, D_MODEL), lambda i, j: (i, 0)),
                  pl.BlockSpec((1, D_MODEL), lambda i, j: (0, 0)),
                  sc_s, sh_s,
                  pl.BlockSpec((D_MODEL, tn), lambda i, j: (0, j))],
        out_specs=pl.BlockSpec((tm, tn), lambda i, j: (i, j)),
        out_shape=jax.ShapeDtypeStruct((grp.n_tok, Z_WIDTH), F32),
        scratch_shapes=[pltpu.VMEM((tm, D_MODEL), BF16)],
        compiler_params=_cparams(("parallel", "arbitrary")),
        name="in_proj",
    )(x, norm_w.reshape(1, D_MODEL), sc_a, sh_a, w_pad)


def _pad_w_in(w):
    idx = np.cumsum(IN_SPLITS)[:-1].tolist()
    cq, ckv, kr, hq, hf, hi, hgate, gq, gk, gv, ggate, glow, br = jnp.split(w, idx, axis=1)
    zeros = lambda n: jnp.zeros((w.shape[0], n), w.dtype)
    half = MLA_ROPE // 2
    kr_rot = jnp.concatenate([-kr[:, half:], kr[:, :half]], axis=1)
    slab = lambda a: jnp.concatenate([zeros(MLA_NOPE), a, zeros(HEAD_SLAB - MLA_NOPE - MLA_ROPE)], axis=1)
    pad_heads = lambda a: jnp.pad(a.reshape(-1, GLA_HEADS, GLA_DK),
                                  ((0, 0), (0, 0), (0, HEAD_SLAB - GLA_DK))).reshape(-1, GLA_HEADS * HEAD_SLAB)
    glow_slab = jnp.concatenate([glow, zeros(LANE - GLA_GATE_RANK)], axis=1)
    out = jnp.concatenate([cq, slab(kr), ckv, slab(kr_rot), glow_slab, hq, hf, hi, hgate,
                           pad_heads(gq), pad_heads(gk), gv, ggate, br], axis=1)
    assert out.shape[1] == Z_WIDTH
    return out.astype(BF16)


def _mla_weights(w_uq, w_uk, w_uv):
    hd = MLA_NOPE + MLA_ROPE
    half = MLA_ROPE // 2
    q = w_uq.reshape(MLA_Q_LORA, MLA_HEADS, hd)
    nope, pe = q[..., :MLA_NOPE], q[..., MLA_NOPE:]
    pe_rot = jnp.concatenate([-pe[..., half:], pe[..., :half]], axis=-1)
    z = lambda n: jnp.zeros((MLA_Q_LORA, MLA_HEADS, n), w_uq.dtype)
    wa = jnp.concatenate([nope, pe, z(HEAD_SLAB - hd)], axis=-1).reshape(MLA_Q_LORA, -1)
    wb = jnp.concatenate([z(MLA_NOPE), pe_rot, z(HEAD_SLAB - hd)], axis=-1).reshape(MLA_Q_LORA, -1)
    k = w_uk.reshape(MLA_KV_LORA, MLA_HEADS, MLA_NOPE)
    wka = jnp.pad(k, ((0, 0), (0, 0), (0, HEAD_SLAB - MLA_NOPE))).reshape(MLA_KV_LORA, -1)
    wukt = jnp.pad(k.transpose(1, 2, 0), ((0, 0), (0, HEAD_SLAB - MLA_NOPE), (0, 0)))
    return wa.astype(BF16), wb.astype(BF16), wka.astype(BF16), w_uv.astype(BF16), wukt.astype(BF16)


def _rope_tables(pos):
    half = MLA_ROPE // 2
    inv = ROPE_BASE ** (-jnp.arange(half, dtype=F32) / half)
    ang = pos.astype(F32)[:, None] * inv[None, :]
    cos, sin = jnp.cos(ang), jnp.sin(ang)
    n = pos.shape[0]
    tail = jnp.zeros((n, HEAD_SLAB - MLA_NOPE - MLA_ROPE), F32)
    ct = jnp.concatenate([jnp.ones((n, MLA_NOPE), F32), cos, cos, tail], axis=1)
    st = jnp.concatenate([jnp.zeros((n, MLA_NOPE), F32), sin, sin, tail], axis=1)
    return ct, st


def _mla_common(z_ref, qnw_ref, kvnw_ref, wa_ref, wb_ref, ct_ref, st_ref):
    z = z_ref[...]
    qn = (_rms(z[:, Z_CQ:Z_CQ + MLA_Q_LORA]) * qnw_ref[...]).astype(BF16)
    lat = _rms(z[:, Z_CKV:Z_CKV + MLA_KV_LORA]) * kvnw_ref[...]
    ct, st = ct_ref[...], st_ref[...]
    ct8 = jnp.concatenate([ct] * MLA_HEADS, axis=1)
    st8 = jnp.concatenate([st] * MLA_HEADS, axis=1)
    q_cat = (_dot(qn, wa_ref[...]) * ct8 + _dot(qn, wb_ref[...]) * st8) * MLA_SCALE
    kpe = z[:, Z_KR:Z_KR + HEAD_SLAB] * ct + z[:, Z_KRR:Z_KRR + HEAD_SLAB] * st
    return q_cat, lat, kpe


def _mla_prep_prompt_body(z_ref, qnw_ref, kvnw_ref, wa_ref, wb_ref, ct_ref, st_ref, wka_ref, wv_ref,
                          q_ref, k_ref, v_ref, lat_ref, kpe_ref):
    q_cat, lat, kpe = _mla_common(z_ref, qnw_ref, kvnw_ref, wa_ref, wb_ref, ct_ref, st_ref)
    q_ref[...] = q_cat.astype(BF16)
    lat_ref[...] = lat
    kpe_ref[...] = kpe
    lb = lat.astype(BF16)
    k_ref[...] = (_dot(lb, wka_ref[...]) + jnp.concatenate([kpe] * MLA_HEADS, axis=1)).astype(BF16)
    v_ref[...] = _dot(lb, wv_ref[...]).astype(BF16)


def _mla_prep_sample_body(z_ref, qnw_ref, kvnw_ref, wa_ref, wb_ref, ct_ref, st_ref, wukt_ref,
                          q_ref, qlat_ref, lat_ref, kpe_ref):
    q_cat, lat, kpe = _mla_common(z_ref, qnw_ref, kvnw_ref, wa_ref, wb_ref, ct_ref, st_ref)
    qb = q_cat.astype(BF16)
    q_ref[...] = qb
    lat_ref[...] = lat
    kpe_ref[...] = kpe
    for h in range(MLA_HEADS):
        qlat_ref[:, h * MLA_KV_LORA:(h + 1) * MLA_KV_LORA] = _dot(
            qb[:, h * HEAD_SLAB:(h + 1) * HEAD_SLAB], wukt_ref[h]).astype(BF16)


def _mla_prep(grp, z, q_norm_w, kv_norm_w, mw, ct, st, sample):
    wa, wb, wka, wv, wukt = mw
    tm = grp.tile(512)
    n_tiles = grp.n_tok // tm
    hw = MLA_HEADS * HEAD_SLAB
    full = lambda a: pl.BlockSpec(a.shape, lambda i: (0,) * a.ndim)
    row = lambda w: pl.BlockSpec((tm, w), lambda i: (i, 0))
    if grp.per_token:
        tab = pl.BlockSpec((tm, HEAD_SLAB), lambda i: (i, 0))
    else:
        per = grp.time // tm
        tab = pl.BlockSpec((tm, HEAD_SLAB), lambda i: (i % per, 0))
    qnw = q_norm_w.reshape(1, -1)
    kvnw = kv_norm_w.reshape(1, -1)
    common_in = [pl.BlockSpec((tm, 1024), lambda i: (i, 0)), full(qnw), full(kvnw), full(wa), full(wb), tab, tab]
    n = grp.n_tok
    if sample:
        return pl.pallas_call(
            _mla_prep_sample_body, grid=(n_tiles,),
            in_specs=common_in + [full(wukt)],
            out_specs=[row(hw), row(MLA_HEADS * MLA_KV_LORA), row(MLA_KV_LORA), row(HEAD_SLAB)],
            out_shape=[jax.ShapeDtypeStruct((n, hw), BF16),
                       jax.ShapeDtypeStruct((n, MLA_HEADS * MLA_KV_LORA), BF16),
                       jax.ShapeDtypeStruct((n, MLA_KV_LORA), F32),
                       jax.ShapeDtypeStruct((n, HEAD_SLAB), F32)],
            compiler_params=_cparams(("parallel",)), name="mla_prep_sample",
        )(z, qnw, kvnw, wa, wb, ct, st, wukt)
    return pl.pallas_call(
        _mla_prep_prompt_body, grid=(n_tiles,),
        in_specs=common_in + [full(wka), full(wv)],
        out_specs=[row(hw), row(hw), row(MLA_HEADS * MLA_V), row(MLA_KV_LORA), row(HEAD_SLAB)],
        out_shape=[jax.ShapeDtypeStruct((n, hw), BF16),
                   jax.ShapeDtypeStruct((n, hw), BF16),
                   jax.ShapeDtypeStruct((n, MLA_HEADS * MLA_V), BF16),
                   jax.ShapeDtypeStruct((n, MLA_KV_LORA), F32),
                   jax.ShapeDtypeStruct((n, HEAD_SLAB), F32)],
        compiler_params=_cparams(("parallel",)), name="mla_prep_prompt",
    )(z, qnw, kvnw, wa, wb, ct, st, wka, wv)


def _flash_body(q_ref, k_ref, v_ref, o_ref, m_scr, l_scr, acc_scr, *, tq):
    qi, ki = pl.program_id(1), pl.program_id(2)

    @pl.when(ki == 0)
    def _():
        m_scr[...] = jnp.full(m_scr.shape, NEG_INF, F32)
        l_scr[...] = jnp.zeros(l_scr.shape, F32)
        acc_scr[...] = jnp.zeros(acc_scr.shape, F32)

    low = lax.broadcasted_iota(jnp.int32, (tq, LANE), 1) < MLA_V

    def step(masked):
        if masked:
            keep = (lax.broadcasted_iota(jnp.int32, (tq, tq), 1)
                    <= lax.broadcasted_iota(jnp.int32, (tq, tq), 0))
        for hp in range(MLA_HEADS // 2):
            pv, al = [], []
            for e in range(2):
                h = 2 * hp + e
                s = _dot_nt(q_ref[0, :, h * HEAD_SLAB:(h + 1) * HEAD_SLAB],
                            k_ref[0, :, h * HEAD_SLAB:(h + 1) * HEAD_SLAB])
                if masked:
                    s = jnp.where(keep, s, NEG_INF)
                m_prev = m_scr[h]
                m_new = jnp.maximum(m_prev, jnp.max(s, axis=-1, keepdims=True))
                alpha = jnp.exp(m_prev - m_new)
                p = jnp.exp(s - m_new)
                l_scr[h] = alpha * l_scr[h] + jnp.sum(p, axis=-1, keepdims=True)
                m_scr[h] = m_new
                pv.append(_dot(p.astype(BF16), v_ref[0, :, hp * LANE:(hp + 1) * LANE]))
                al.append(alpha)
            sl = slice(hp * LANE, (hp + 1) * LANE)
            acc_scr[:, sl] = jnp.where(low, al[0], al[1]) * acc_scr[:, sl] + jnp.where(low, pv[0], pv[1])

    @pl.when(ki < qi)
    def _():
        step(False)

    @pl.when(ki == qi)
    def _():
        step(True)

    @pl.when(ki == pl.num_programs(2) - 1)
    def _():
        for hp in range(MLA_HEADS // 2):
            sl = slice(hp * LANE, (hp + 1) * LANE)
            o_ref[0, :, sl] = acc_scr[:, sl] / jnp.where(low, l_scr[2 * hp], l_scr[2 * hp + 1])


def _flash(q, k, v, batch, seq, tq):
    hw = MLA_HEADS * HEAD_SLAB
    vw = MLA_HEADS * MLA_V
    nq = seq // tq
    q3, k3, v3 = q.reshape(batch, seq, hw), k.reshape(batch, seq, hw), v.reshape(batch, seq, vw)
    out = pl.pallas_call(
        functools.partial(_flash_body, tq=tq),
        grid=(batch, nq, nq),
        in_specs=[pl.BlockSpec((1, tq, hw), lambda b, i, j: (b, i, 0)),
                  pl.BlockSpec((1, tq, hw), lambda b, i, j: (b, jnp.minimum(i, j), 0)),
                  pl.BlockSpec((1, tq, vw), lambda b, i, j: (b, jnp.minimum(i, j), 0))],
        out_specs=pl.BlockSpec((1, tq, vw), lambda b, i, j: (b, i, 0)),
        out_shape=jax.ShapeDtypeStruct((batch, seq, vw), F32),
        scratch_shapes=[pltpu.VMEM((MLA_HEADS, tq, 1), F32), pltpu.VMEM((MLA_HEADS, tq, 1), F32),
                        pltpu.VMEM((tq, vw), F32)],
        compiler_params=_cparams(("parallel", "parallel", "arbitrary")),
        name="flash",
    )(q3, k3, v3)
    return out.reshape(batch * seq, vw)


def _paged_body(pt_ref, q_ref, qlat_ref, nlat_ref, nkpe_ref, lat_hbm, kpe_hbm, o_ref,
                lat_buf, kpe_buf, sem, *, layer, n_pages, n_new):
    b = pl.program_id(0)
    nb = pl.num_programs(0)
    rows = q_ref.shape[1]

    def copies(bb, slot):
        out = []
        for j in range(n_pages):
            pg = pt_ref[bb, j]
            dst = pl.ds(j * PAGE_SIZE, PAGE_SIZE)
            out.append(pltpu.make_async_copy(lat_hbm.at[layer, pg], lat_buf.at[slot, dst], sem.at[0, slot]))
            out.append(pltpu.make_async_copy(kpe_hbm.at[layer, pg], kpe_buf.at[slot, dst], sem.at[1, slot]))
        return out

    @pl.when(b == 0)
    def _():
        for c in copies(0, 0):
            c.start()

    slot = b % 2

    @pl.when(b + 1 < nb)
    def _():
        for c in copies(b + 1, 1 - slot):
            c.start()

    for c in copies(b, slot):
        c.wait()

    qlat = qlat_ref[0]
    qpe = q_ref[0][:, MLA_NOPE:MLA_NOPE + MLA_ROPE]
    lat = lat_buf[slot].astype(BF16)
    kpe = kpe_buf[slot].astype(BF16)
    s_past = _dot_nt(qlat, lat) + _dot_nt(qpe, kpe)
    nlat = nlat_ref[0].astype(BF16)
    nkpe = nkpe_ref[0][:, MLA_NOPE:MLA_NOPE + MLA_ROPE].astype(BF16)
    s_new = _dot_nt(qlat, nlat) + _dot_nt(qpe, nkpe)
    t_of_row = lax.broadcasted_iota(jnp.int32, (rows, n_new), 0) // MLA_HEADS
    s_new = jnp.where(lax.broadcasted_iota(jnp.int32, (rows, n_new), 1) <= t_of_row, s_new, NEG_INF)
    m = jnp.maximum(jnp.max(s_past, axis=-1, keepdims=True), jnp.max(s_new, axis=-1, keepdims=True))
    p_past = jnp.exp(s_past - m)
    p_new = jnp.exp(s_new - m)
    denom = jnp.sum(p_past, axis=-1, keepdims=True) + jnp.sum(p_new, axis=-1, keepdims=True)
    o = _dot(p_past.astype(BF16), lat) + _dot(p_new.astype(BF16), nlat)
    o_ref[0] = o / denom


def _paged(page_table, q_cat, q_lat, lat_new, kpe_new, cache_latent, cache_k_rope, layer, batch, n_new):
    rows = n_new * MLA_HEADS
    n_pages = page_table.shape[1]
    past = n_pages * PAGE_SIZE
    q3 = q_cat.reshape(batch, rows, HEAD_SLAB)
    ql3 = q_lat.reshape(batch, rows, MLA_KV_LORA)
    nl3 = lat_new.reshape(batch, n_new, MLA_KV_LORA)
    nk3 = kpe_new.reshape(batch, n_new, HEAD_SLAB)
    blk = lambda r, w: pl.BlockSpec((1, r, w), lambda b, pt: (b, 0, 0))
    out = pl.pallas_call(
        functools.partial(_paged_body, layer=layer, n_pages=n_pages, n_new=n_new),
        grid_spec=pltpu.PrefetchScalarGridSpec(
            num_scalar_prefetch=1, grid=(batch,),
            in_specs=[blk(rows, HEAD_SLAB), blk(rows, MLA_KV_LORA), blk(n_new, MLA_KV_LORA), blk(n_new, HEAD_SLAB),
                      pl.BlockSpec(memory_space=pl.ANY), pl.BlockSpec(memory_space=pl.ANY)],
            out_specs=blk(rows, MLA_KV_LORA),
            scratch_shapes=[pltpu.VMEM((2, past, MLA_KV_LORA), F32), pltpu.VMEM((2, past, MLA_ROPE), F32),
                            pltpu.SemaphoreType.DMA((2, 2))]),
        out_shape=jax.ShapeDtypeStruct((batch, rows, MLA_KV_LORA), F32),
        compiler_params=_cparams(("arbitrary",)),
        name="paged",
    )(page_table, q3, ql3, nl3, nk3, cache_latent, cache_k_rope)
    return out.reshape(batch * n_new, MLA_HEADS * MLA_KV_LORA)


def _head_proj_body(o_ref, w_ref, y_ref):
    for h in range(MLA_HEADS):
        y_ref[:, h * MLA_V:(h + 1) * MLA_V] = _dot(
            o_ref[:, h * MLA_KV_LORA:(h + 1) * MLA_KV_LORA].astype(BF16), w_ref[h])


def _head_proj(o_lat, w_uv):
    n = o_lat.shape[0]
    w = w_uv.reshape(MLA_KV_LORA, MLA_HEADS, MLA_V).transpose(1, 0, 2).astype(BF16)
    return pl.pallas_call(
        _head_proj_body, grid=(1,),
        in_specs=[pl.BlockSpec(o_lat.shape, lambda i: (0, 0)), pl.BlockSpec(w.shape, lambda i: (0, 0, 0))],
        out_specs=pl.BlockSpec((n, MLA_HEADS * MLA_V), lambda i: (0, 0)),
        out_shape=jax.ShapeDtypeStruct((n, MLA_HEADS * MLA_V), F32),
        compiler_params=_cparams(("arbitrary",)), name="head_proj",
    )(o_lat, w)


N_REC_HEADS = 4
REC_WIDTH = N_REC_HEADS * HEAD_SLAB


def _split3(x):
    a = x.astype(BF16)
    r = x - a.astype(F32)
    b = r.astype(BF16)
    c = (r - b.astype(F32)).astype(BF16)
    return a, b, c


def _scan_body(*refs, gla, chunk, sub, valid, has_s0, k_dim):
    if gla:
        q_ref, k_ref, v_ref, glow_ref, wg_ref, bg_ref, tri_ref = refs[:7]
        rest = refs[7:]
    else:
        q_ref, k_ref, v_ref, la_ref, l1_ref, tri_ref = refs[:6]
        rest = refs[6:]
    if has_s0:
        s0_ref, o_ref, sfin_ref, st_scr = rest
    else:
        o_ref, sfin_ref, st_scr = rest
    ci = pl.program_id(1)

    @pl.when(ci == 0)
    def _():
        if has_s0:
            for h in range(N_REC_HEADS):
                s0 = s0_ref[0, h]
                if k_dim < HEAD_SLAB:
                    s0 = jnp.concatenate([s0, jnp.zeros((HEAD_SLAB - k_dim, s0.shape[1]), F32)], axis=0)
                st_scr[h] = s0.T
        else:
            st_scr[...] = jnp.zeros(st_scr.shape, F32)

    v = v_ref[...]
    if gla:
        q = q_ref[...] * (GLA_DK ** -0.5)
        k = k_ref[...]
        g = _log_sigmoid(_dot(glow_ref[...].astype(BF16), wg_ref[...]) + bg_ref[...]) * (1.0 / GLA_GATE_NORMALIZER)
    else:
        xq = q_ref[...]
        q = xq * jax.nn.sigmoid(xq) * (HG_DK ** -0.5)
        a = la_ref[...]
        bb = l1_ref[...] + _log_sigmoid(k_ref[...])
        g = jnp.maximum(a, bb) + jnp.log1p(jnp.exp(-jnp.abs(a - bb)))
        k = 1.0 - jnp.exp(g)
    if valid < chunk:
        live = lax.broadcasted_iota(jnp.int32, (chunk, 1), 0) < valid
        g = jnp.where(live, g, 0.0)
        k = jnp.where(live, k, 0.0)

    tri = tri_ref[...]
    g1, g2, g3 = _split3(g)
    b = _dot(tri, g1) + _dot(tri, g2) + _dot(tri, g3)
    b_end = b[chunk - 1:chunk]
    qe = (q * jnp.exp(b)).astype(BF16)
    kd = (k * jnp.exp(b_end - b)).astype(BF16)
    vb = v.astype(BF16)
    hs = lambda h: slice(h * HEAD_SLAB, (h + 1) * HEAD_SLAB)

    o_inter = jnp.concatenate(
        [_dot_nt(qe[:, hs(h)], st_scr[h].astype(BF16)) for h in range(N_REC_HEADS)], axis=1)

    row_in_sub = lax.broadcasted_iota(jnp.int32, (sub, 1), 0)
    blocks = []
    for i in range(chunk // sub):
        lo = i * sub
        if lo >= valid:
            blocks.append(o_inter[lo:lo + sub])
            continue
        bi, qi_, ki_, vi = b[lo:lo + sub], q[lo:lo + sub], k[lo:lo + sub], v[lo:lo + sub]
        blk = o_inter[lo:lo + sub]
        if i > 0:
            r = b[lo - 1:lo]
            qt = (qi_ * jnp.exp(bi - r)).astype(BF16)
            kt = (k[:lo] * jnp.exp(r - b[:lo])).astype(BF16)
            off = []
            for h in range(N_REC_HEADS):
                att = _dot_nt(qt[:, hs(h)], kt[:, hs(h)])
                off.append(_dot(att.astype(BF16), vb[:lo, hs(h)]))
            blk = blk + jnp.concatenate(off, axis=1)
        for s in range(min(sub, valid - lo)):
            e = jnp.exp(jnp.minimum(bi - bi[s:s + 1], 0.0))
            w = qi_ * (ki_[s:s + 1] * e)
            parts = []
            for h in range(N_REC_HEADS):
                a_ts = jnp.sum(w[:, hs(h)], axis=-1, keepdims=True)
                parts.append(jnp.where(row_in_sub >= s, a_ts, 0.0) * vi[s:s + 1, hs(h)])
            blk = blk + jnp.concatenate(parts, axis=1)
        blocks.append(blk)
    o_ref[...] = jnp.concatenate(blocks, axis=0)

    decay = jnp.exp(b_end)
    for h in range(N_REC_HEADS):
        st_scr[h] = st_scr[h] * decay[:, hs(h)] + _dot_tn(vb[:, hs(h)], kd[:, hs(h)])

    @pl.when(ci == pl.num_programs(1) - 1)
    def _():
        for h in range(N_REC_HEADS):
            sfin_ref[0, h] = st_scr[h].T[:k_dim]


def _scan(z, batch, time, chunk, sub, valid, gla, extra, s0):
    k_dim = GLA_DK if gla else HG_DK
    nck = time // chunk
    cb = lambda col: pl.BlockSpec((chunk, REC_WIDTH), lambda b, c: (b * nck + c, col // REC_WIDTH))
    full = lambda a: pl.BlockSpec(a.shape, lambda b, c: (0,) * a.ndim)
    tri = jnp.tril(jnp.ones((chunk, chunk), F32)).astype(BF16)
    if gla:
        wg, bg = extra
        ins = [z, z, z, z, wg, bg, tri]
        specs = [cb(Z_GQ), cb(Z_GK), cb(Z_GV),
                 pl.BlockSpec((chunk, LANE), lambda b, c: (b * nck + c, Z_GLOW // LANE)), full(wg), full(bg), full(tri)]
    else:
        la, l1 = extra
        ins = [z, z, z, la, l1, tri]
        specs = [cb(Z_HQ), cb(Z_HF), cb(Z_HI), full(la), full(l1), full(tri)]
    if s0 is not None:
        ins.append(s0)
        specs.append(pl.BlockSpec((1, N_REC_HEADS, k_dim, HEAD_SLAB), lambda b, c: (b, 0, 0, 0)))
    return pl.pallas_call(
        functools.partial(_scan_body, gla=gla, chunk=chunk, sub=sub, valid=valid, has_s0=s0 is not None,
                          k_dim=k_dim),
        grid=(batch, nck),
        in_specs=specs,
        out_specs=[pl.BlockSpec((chunk, REC_WIDTH), lambda b, c: (b * nck + c, 0)),
                   pl.BlockSpec((1, N_REC_HEADS, k_dim, HEAD_SLAB), lambda b, c: (b, 0, 0, 0))],
        out_shape=[jax.ShapeDtypeStruct((batch * time, REC_WIDTH), F32),
                   jax.ShapeDtypeStruct((batch, N_REC_HEADS, k_dim, HEAD_SLAB), F32)],
        scratch_shapes=[pltpu.VMEM((N_REC_HEADS, HEAD_SLAB, HEAD_SLAB), F32)],
        compiler_params=_cparams(("parallel", "arbitrary")),
        name="scan_gla" if gla else "scan_hgrn",
    )(*ins)


def _merge_body(x_ref, ym_ref, oh_ref, og_ref, hgate_ref, ggate_ref, br0_ref, br1_ref, br2_ref,
                g1_ref, sc2_ref, sh2_ref, hgw_ref, glw_ref, nfw_ref, wbm_ref, wbh_ref, wbg_ref, wout_ref,
                rwh_ref, rwl_ref, rb_ref, x1_ref, h2_ref, lg_ref):
    silu = lambda t: t * jax.nn.sigmoid(t)
    yh = _rms(oh_ref[...]) * hgw_ref[...] * silu(hgate_ref[...])
    og = og_ref[...]
    glw = glw_ref[...]
    yg = jnp.concatenate([_rms(og[:, h * GLA_DV:(h + 1) * GLA_DV]) * glw for h in range(GLA_HEADS)], axis=1)
    yg = yg * silu(ggate_ref[...])
    m = (jax.nn.sigmoid(br0_ref[...]) * _dot(ym_ref[...].astype(BF16), wbm_ref[...])
         + jax.nn.sigmoid(br1_ref[...]) * _dot(yh.astype(BF16), wbh_ref[...])
         + jax.nn.sigmoid(br2_ref[...]) * _dot(yg.astype(BF16), wbg_ref[...]))
    x1 = x_ref[...] + g1_ref[0] * _dot(m.astype(BF16), wout_ref[...])
    x1_ref[...] = x1
    h2 = _rms(x1) * nfw_ref[...] * (1.0 + sc2_ref[0]) + sh2_ref[0]
    hh = h2.astype(BF16)
    h2_ref[...] = hh
    hl = (h2 - hh.astype(F32)).astype(BF16)
    rwh = rwh_ref[...]
    lg_ref[...] = _dot(hh, rwh) + _dot(hh, rwl_ref[...]) + _dot(hl, rwh) + rb_ref[...]


def _merge(grp, x, y_mla, o_hg, o_gla, z, g1, sc2, sh2, lw):
    tm = grp.tile(256)
    row = lambda w: pl.BlockSpec((tm, w), lambda i: (i, 0))
    zc = lambda col, w: pl.BlockSpec((tm, w), lambda i: (i, col // w))
    full = lambda a: pl.BlockSpec(a.shape, lambda i: (0,) * a.ndim)
    g1_a, g1_s = grp.mod(g1, tm)
    sc_a, sc_s = grp.mod(sc2, tm)
    sh_a, sh_s = grp.mod(sh2, tm)
    ws = [lw["hg_norm_w"], lw["gla_norm_w"], lw["norm_ffn_w"], lw["w_br_mla"], lw["w_br_hg"], lw["w_br_gla"],
          lw["w_out"], lw["rw_hi"], lw["rw_lo"], lw["rb"]]
    n = grp.n_tok
    return pl.pallas_call(
        _merge_body, grid=(grp.n_tok // tm,),
        in_specs=[row(D_MODEL), row(512), row(512), row(512), zc(Z_HGATE, 512), zc(Z_GGATE, 512),
                  zc(Z_BR, 1024), zc(Z_BR + 1024, 1024), zc(Z_BR + 2048, 1024),
                  g1_s, sc_s, sh_s] + [full(w) for w in ws],
        out_specs=[row(D_MODEL), row(D_MODEL), row(LANE)],
        out_shape=[jax.ShapeDtypeStruct((n, D_MODEL), F32), jax.ShapeDtypeStruct((n, D_MODEL), BF16),
                   jax.ShapeDtypeStruct((n, LANE), F32)],
        compiler_params=_cparams(("parallel",)), name="merge",
    )(x, y_mla, o_hg, o_gla, z, z, z, z, z, g1_a, sc_a, sh_a, *ws)


def _experts_body(be_ref, nu_ref, x_ref, sw_ref, wgu_ref, bgu_ref, wd_ref, bd_ref, y_ref, wgu_scr, wd_scr):
    i = pl.program_id(0)
    e = be_ref[i]
    prev = be_ref[jnp.maximum(i - 1, 0)]

    @pl.when(jnp.logical_or(i == 0, e != prev))
    def _():
        wgu_scr[...] = wgu_ref[0].astype(BF16)
        wd_scr[...] = wd_ref[0].astype(BF16)

    @pl.when(i < nu_ref[0])
    def _():
        gu = _dot(x_ref[...], wgu_scr[...]) + bgu_ref[0]
        gate = jnp.minimum(gu[:, :D_EXPERT], SWIGLU_LIMIT)
        up = jnp.clip(gu[:, D_EXPERT:], -SWIGLU_LIMIT, SWIGLU_LIMIT)
        act = (up + 1.0) * gate * jax.nn.sigmoid(SWIGLU_ALPHA * gate)
        y_ref[...] = (_dot(act.astype(BF16), wd_scr[...]) + bd_ref[0]) * sw_ref[...]

    @pl.when(i >= nu_ref[0])
    def _():
        y_ref[...] = jnp.zeros(y_ref.shape, F32)


def _experts(block_e, n_used, xb, slot_w, w_gate_up, b_gate_up, w_down, b_down, layer):
    n_slots = xb.shape[0]
    tm = MOE_TM
    ne = N_EXPERTS
    return pl.pallas_call(
        _experts_body,
        grid_spec=pltpu.PrefetchScalarGridSpec(
            num_scalar_prefetch=2, grid=(n_slots // tm,),
            in_specs=[pl.BlockSpec((tm, D_MODEL), lambda i, be, nu: (i, 0)),
                      pl.BlockSpec((tm, 1), lambda i, be, nu: (i, 0)),
                      pl.BlockSpec((1, D_MODEL, 2 * D_EXPERT), lambda i, be, nu: (layer * ne + be[i], 0, 0)),
                      pl.BlockSpec((1, 1, 2 * D_EXPERT), lambda i, be, nu: (layer * ne + be[i], 0, 0)),
                      pl.BlockSpec((1, D_EXPERT, D_MODEL), lambda i, be, nu: (layer * ne + be[i], 0, 0)),
                      pl.BlockSpec((1, 1, D_MODEL), lambda i, be, nu: (layer * ne + be[i], 0, 0))],
            out_specs=pl.BlockSpec((tm, D_MODEL), lambda i, be, nu: (i, 0)),
            scratch_shapes=[pltpu.VMEM((D_MODEL, 2 * D_EXPERT), BF16), pltpu.VMEM((D_EXPERT, D_MODEL), BF16)]),
        out_shape=jax.ShapeDtypeStruct((n_slots, D_MODEL), F32),
        compiler_params=_cparams(("arbitrary",)),
        name="experts",
    )(block_e, n_used, xb, slot_w,
      w_gate_up.reshape(DEPTH * ne, D_MODEL, 2 * D_EXPERT), b_gate_up.reshape(DEPTH * ne, 1, 2 * D_EXPERT),
      w_down.reshape(DEPTH * ne, D_EXPERT, D_MODEL), b_down.reshape(DEPTH * ne, 1, D_MODEL))


def _route(logits):
    n = logits.shape[0]
    nk = n * TOP_K
    tm = MOE_TM
    top_v, top_e = lax.top_k(logits, TOP_K)
    wts = jax.nn.softmax(top_v, axis=-1)
    flat_e = top_e.reshape(-1).astype(jnp.int32)
    flat_tok = jnp.arange(nk, dtype=jnp.int32) // TOP_K
    order = jnp.argsort(flat_e)
    se, stok, sw = flat_e[order], flat_tok[order], wts.reshape(-1)[order]
    counts = jnp.zeros((N_EXPERTS,), jnp.int32).at[flat_e].add(1)
    padded = (counts + tm - 1) // tm * tm
    pad_end = jnp.cumsum(padded)
    pad_start = pad_end - padded
    start = jnp.cumsum(counts) - counts
    dest = pad_start[se] + jnp.arange(nk, dtype=jnp.int32) - start[se]
    n_blocks = (nk + N_EXPERTS * (tm - 1) + tm - 1) // tm
    slot_tok = jnp.zeros((n_blocks * tm,), jnp.int32).at[dest].set(stok)
    slot_w = jnp.zeros((n_blocks * tm,), F32).at[dest].set(sw)
    block_e = jnp.minimum(jnp.searchsorted(pad_end, jnp.arange(n_blocks, dtype=jnp.int32) * tm, side='right'),
                          N_EXPERTS - 1).astype(jnp.int32)
    n_used = (pad_end[-1] // tm).astype(jnp.int32).reshape(1)
    dest_tk = jnp.zeros((nk,), jnp.int32).at[order].set(dest).reshape(n, TOP_K)
    return slot_tok, slot_w.reshape(-1, 1), block_e, n_used, dest_tk


def _final_norm_body(x_ref, w_ref, o_ref):
    o_ref[...] = _rms(x_ref[...]) * w_ref[...]


def _final_norm(x, w, tm):
    n = x.shape[0]
    return pl.pallas_call(
        _final_norm_body, grid=(n // tm,),
        in_specs=[pl.BlockSpec((tm, D_MODEL), lambda i: (i, 0)), pl.BlockSpec((1, D_MODEL), lambda i: (0, 0))],
        out_specs=pl.BlockSpec((tm, D_MODEL), lambda i: (i, 0)),
        out_shape=jax.ShapeDtypeStruct((n, D_MODEL), F32),
        compiler_params=_cparams(("parallel",)), name="final_norm",
    )(x, w.reshape(1, D_MODEL))


def kernel(x_prompt, x_sample, cache_latent, cache_k_rope, state_hgrn, state_gla, page_table, c_prompt, c_sample, norm_mix_w, norm_ffn_w, final_norm_w, w_ada, b_ada, w_in, mla_q_norm_w, mla_w_uq, mla_kv_norm_w, mla_w_uk, mla_w_uv, hgrn_lower_bounds, hgrn_norm_w, gla_w_gate, gla_b_gate, gla_norm_w, w_branch_mla, w_branch_hgrn, w_branch_gla, w_out, router_w, router_b, w_gate_up, b_gate_up, w_down, b_down):
    bp, seq, _ = x_prompt.shape
    bs, tnew, _ = x_sample.shape
    n_p, n_s = bp * seq, bs * tnew
    past_len = page_table.shape[1] * cache_latent.shape[2]
    gp = _Group(bp, seq, per_token=False)
    gs = _Group(bs, tnew, per_token=True)
    s_pad = SCAN_SUB

    rows = bp + bs
    rows_pad = -(-rows // 8) * 8
    c_all = jnp.pad(jnp.concatenate([c_prompt, c_sample], axis=0), ((0, rows_pad - rows), (0, 0)))
    mod = _adaln(c_all, w_ada, b_ada)

    lbs = jax.nn.softmax(hgrn_lower_bounds.astype(F32), axis=0)
    lbs = jnp.cumsum(lbs, axis=0) - lbs[0]
    log_lb = jnp.log(jnp.maximum(lbs, TINY))
    log_1m_lb = jnp.log1p(-lbs)

    ct_p, st_p = _rope_tables(jnp.arange(seq, dtype=jnp.int32))
    pos_s = past_len + jnp.arange(tnew, dtype=jnp.int32)
    ct_s, st_s = _rope_tables(jnp.tile(pos_s, bs))

    xp = x_prompt.reshape(n_p, D_MODEL)
    xs = x_sample.reshape(n_s, D_MODEL)
    lat_p, kpe_p, hg_p, gla_p, lat_s, kpe_s, hg_s, gla_s = [], [], [], [], [], [], [], []

    for l in range(DEPTH):
        w_pad = _pad_w_in(w_in[l])
        mw = _mla_weights(mla_w_uq[l], mla_w_uk[l], mla_w_uv[l])
        rw = jnp.pad(router_w[l], ((0, 0), (0, LANE - N_EXPERTS)))
        rw_hi = rw.astype(BF16)
        lw = dict(hg_norm_w=hgrn_norm_w[l].reshape(1, -1), gla_norm_w=gla_norm_w[l].reshape(1, -1),
                  norm_ffn_w=norm_ffn_w[l].reshape(1, -1), w_br_mla=w_branch_mla[l].astype(BF16),
                  w_br_hg=w_branch_hgrn[l].astype(BF16), w_br_gla=w_branch_gla[l].astype(BF16),
                  w_out=w_out[l].astype(BF16), rw_hi=rw_hi, rw_lo=(rw - rw_hi.astype(F32)).astype(BF16),
                  rb=jnp.pad(router_b[l], (0, LANE - N_EXPERTS)).reshape(1, LANE))
        wg = jnp.pad(gla_w_gate[l].reshape(GLA_GATE_RANK, GLA_HEADS, GLA_DK),
                     ((0, LANE - GLA_GATE_RANK), (0, 0), (0, HEAD_SLAB - GLA_DK))).reshape(LANE, REC_WIDTH).astype(BF16)
        bg = jnp.pad(gla_b_gate[l].reshape(GLA_HEADS, GLA_DK), ((0, 0), (0, HEAD_SLAB - GLA_DK))).reshape(1, REC_WIDTH)
        hg_extra = (log_lb[l].reshape(1, -1), log_1m_lb[l].reshape(1, -1))
        split6 = lambda m: jnp.split(m, 6, axis=-1)
        sh1p, sc1p, g1p, sh2p, sc2p, g2p = split6(mod[l, :bp])
        sh1s, sc1s, g1s, sh2s, sc2s, g2s = split6(mod[l, bp:rows])

        zp = _in_proj(gp, xp, norm_mix_w[l], sc1p, sh1p, w_pad)
        q, k, v, latp, kpep = _mla_prep(gp, zp, mla_q_norm_w[l], mla_kv_norm_w[l], mw, ct_p, st_p, sample=False)
        y_mla_p = _flash(q, k, v, bp, seq, tq=512)
        o_hg_p, s_hg_p = _scan(zp, bp, seq, SCAN_CHUNK, SCAN_SUB, SCAN_CHUNK, False, hg_extra, None)
        o_gl_p, s_gl_p = _scan(zp, bp, seq, SCAN_CHUNK, SCAN_SUB, SCAN_CHUNK, True, (wg, bg), None)
        x1p, h2p, lgp = _merge(gp, xp, y_mla_p, o_hg_p, o_gl_p, zp, g1p, sc2p, sh2p, lw)

        zs = _in_proj(gs, xs, norm_mix_w[l], sc1s, sh1s, w_pad)
        qs, qlat, lats, kpes = _mla_prep(gs, zs, mla_q_norm_w[l], mla_kv_norm_w[l], mw, ct_s, st_s, sample=True)
        o_lat = _paged(page_table, qs, qlat, lats, kpes, cache_latent, cache_k_rope, l, bs, tnew)
        y_mla_s = _head_proj(o_lat, mla_w_uv[l])
        zs_pad = jnp.pad(zs.reshape(bs, tnew, Z_WIDTH), ((0, 0), (0, s_pad - tnew), (0, 0))).reshape(bs * s_pad, Z_WIDTH)
        unpad = lambda o: o.reshape(bs, s_pad, -1)[:, :tnew].reshape(n_s, -1)
        o_hg_s, s_hg_s = _scan(zs_pad, bs, s_pad, s_pad, SCAN_SUB, tnew, False, hg_extra, state_hgrn[l])
        o_gl_s, s_gl_s = _scan(zs_pad, bs, s_pad, s_pad, SCAN_SUB, tnew, True, (wg, bg), state_gla[l])
        x1s, h2s, lgs = _merge(gs, xs, y_mla_s, unpad(o_hg_s), unpad(o_gl_s), zs, g1s, sc2s, sh2s, lw)

        h2 = jnp.concatenate([h2p, h2s], axis=0)
        logits = jnp.concatenate([lgp, lgs], axis=0)[:, :N_EXPERTS]
        slot_tok, slot_w, block_e, n_used, dest_tk = _route(logits)
        yb = _experts(block_e, n_used, h2[slot_tok], slot_w, w_gate_up, b_gate_up, w_down, b_down, l)
        y = jnp.sum(yb[dest_tk], axis=1)
        xp = x1p + jnp.repeat(g2p, seq, axis=0) * y[:n_p]
        xs = x1s + jnp.repeat(g2s, tnew, axis=0) * y[n_p:]

        sl = slice(MLA_NOPE, MLA_NOPE + MLA_ROPE)
        lat_p.append(latp.reshape(bp, seq, -1)); kpe_p.append(kpep[:, sl].reshape(bp, seq, -1))
        hg_p.append(s_hg_p); gla_p.append(s_gl_p)
        lat_s.append(lats.reshape(bs, tnew, -1)); kpe_s.append(kpes[:, sl].reshape(bs, tnew, -1))
        hg_s.append(s_hg_s); gla_s.append(s_gl_s)

    y_prompt = _final_norm(xp, final_norm_w, 1024).reshape(bp, seq, D_MODEL)
    y_sample = _final_norm(xs, final_norm_w, n_s).reshape(bs, tnew, D_MODEL)
    st = jnp.stack
    return (y_prompt, y_sample, st(lat_p), st(kpe_p), st(hg_p), st(gla_p),
            st(lat_s), st(kpe_s), st(hg_s), st(gla_s))
```

```python
import functools

import numpy as np
import jax
import jax.numpy as jnp
from jax import lax
from jax.experimental import pallas as pl
from jax.experimental.pallas import tpu as pltpu

F32 = jnp.float32
BF16 = jnp.bfloat16

D_MODEL = 1024
DEPTH = 2
PAGE_SIZE = 128
MLA_HEADS = 8
MLA_NOPE = 64
MLA_ROPE = 32
MLA_V = 64
MLA_Q_LORA = 384
MLA_KV_LORA = 256
MLA_SCALE = (MLA_NOPE + MLA_ROPE) ** -0.5
ROPE_BASE = 10000.0
HG_HEADS = 4
HG_DK = 128
HG_DV = 128
GLA_HEADS = 4
GLA_DK = 64
GLA_DV = 128
GLA_GATE_RANK = 16
GLA_GATE_NORMALIZER = 16.0
N_EXPERTS = 32
TOP_K = 4
D_EXPERT = 1024
SWIGLU_LIMIT = 7.0
SWIGLU_ALPHA = 1.702
EPS = 1e-6
NEG_INF = -1e30
TINY = 1e-30

IN_SPLITS = (MLA_Q_LORA, MLA_KV_LORA, MLA_ROPE, 512, 512, 512, 512, 256, 256, 512, 512, GLA_GATE_RANK,
             3 * D_MODEL)

LANE = 128
HEAD_SLAB = 128
VMEM_LIMIT = 56 * 1024 * 1024

Z_WIDTH = 8192
Z_CQ, Z_KR, Z_CKV, Z_KRR, Z_GLOW = 0, 384, 512, 768, 896
Z_HQ, Z_HF, Z_HI, Z_HGATE = 1024, 1536, 2048, 2560
Z_GQ, Z_GK, Z_GV, Z_GGATE, Z_BR = 3072, 3584, 4096, 4608, 5120

SCAN_CHUNK = 64
SCAN_SUB = 16
MOE_TM = 256


def _cparams(sem, vmem=VMEM_LIMIT):
    return pltpu.CompilerParams(dimension_semantics=sem, vmem_limit_bytes=vmem)


def _dot(a, b):
    return jnp.dot(a, b, preferred_element_type=F32)


def _dot_nt(a, b):
    return lax.dot_general(a, b, (((1,), (1,)), ((), ())), preferred_element_type=F32)


def _dot_tn(a, b):
    return lax.dot_general(a, b, (((0,), (0,)), ((), ())), preferred_element_type=F32)


def _rms(x):
    return x * lax.rsqrt(jnp.mean(x * x, axis=-1, keepdims=True) + EPS)


def _log_sigmoid(x):
    return jnp.minimum(x, 0.0) - jnp.log1p(jnp.exp(-jnp.abs(x)))


def _adaln_body(c_ref, w_ref, b_ref, o_ref):
    c = c_ref[...]
    a = (c * jax.nn.sigmoid(c)).astype(BF16)
    o_ref[0] = _dot(a, w_ref[0].astype(BF16)) + b_ref[0]


def _adaln(c_all, w_ada, b_ada):
    rows = c_all.shape[0]
    tn = 1536
    n_out = w_ada.shape[-1]
    return pl.pallas_call(
        _adaln_body,
        grid=(DEPTH, n_out // tn),
        in_specs=[pl.BlockSpec((rows, D_MODEL), lambda l, j: (0, 0)),
                  pl.BlockSpec((1, D_MODEL, tn), lambda l, j: (l, 0, j)),
                  pl.BlockSpec((1, 1, tn), lambda l, j: (l, 0, j))],
        out_specs=pl.BlockSpec((1, rows, tn), lambda l, j: (l, 0, j)),
        out_shape=jax.ShapeDtypeStruct((DEPTH, rows, n_out), F32),
        compiler_params=_cparams(("parallel", "parallel")),
        name="adaln",
    )(c_all, w_ada, b_ada.reshape(DEPTH, 1, n_out))


class _Group:
    def __init__(self, batch, time, per_token):
        self.batch, self.time, self.per_token = batch, time, per_token
        self.n_tok = batch * time

    def tile(self, tm):
        tm = min(tm, self.n_tok)
        assert (self.n_tok if self.per_token else self.time) % tm == 0
        return tm

    def mod(self, m, tm):
        if self.per_token:
            arr = jnp.repeat(m, self.time, axis=0)[None]
            return arr, pl.BlockSpec((1, tm, m.shape[-1]), lambda *g: (0, g[0], 0))
        arr = m[:, None, :]
        per = self.time // tm
        return arr, pl.BlockSpec((1, 1, m.shape[-1]), lambda *g: (g[0] // per, 0, 0))


def _in_proj_body(x_ref, nw_ref, sc_ref, sh_ref, w_ref, z_ref, h_scr):
    @pl.when(pl.program_id(1) == 0)
    def _():
        h = _rms(x_ref[...]) * nw_ref[...]
        h_scr[...] = (h * (1.0 + sc_ref[0]) + sh_ref[0]).astype(BF16)

    z_ref[...] = _dot(h_scr[...], w_ref[...])


def _in_proj(grp, x, norm_w, sc, sh, w_pad):
    tm, tn = grp.tile(1024), 1024
    sc_a, sc_s = grp.mod(sc, tm)
    sh_a, sh_s = grp.mod(sh, tm)
    return pl.pallas_call(
        _in_proj_body,
        grid=(grp.n_tok // tm, Z_WIDTH // tn),
        in_specs=[pl.BlockSpec((tm, D_MODEL), lambda i, j: (i, 0)),
                  pl.BlockSpec((1, D_MODEL), lambda i, j: (0, 0)),
                  sc_s, sh_s,
                  pl.BlockSpec((D_MODEL, tn), lambda i, j: (0, j))],
        out_specs=pl.BlockSpec((tm, tn), lambda i, j: (i, j)),
        out_shape=jax.ShapeDtypeStruct((grp.n_tok, Z_WIDTH), F32),
        scratch_shapes=[pltpu.VMEM((tm, D_MODEL), BF16)],
        compiler_params=_cparams(("parallel", "arbitrary")),
        name="in_proj",
    )(x, norm_w.reshape(1, D_MODEL), sc_a, sh_a, w_pad)


def _pad_w_in(w):
    idx = np.cumsum(IN_SPLITS)[:-1].tolist()
    cq, ckv, kr, hq, hf, hi, hgate, gq, gk, gv, ggate, glow, br = jnp.split(w, idx, axis=1)
    zeros = lambda n: jnp.zeros((w.shape[0], n), w.dtype)
    half = MLA_ROPE // 2
    kr_rot = jnp.concatenate([-kr[:, half:], kr[:, :half]], axis=1)
    slab = lambda a: jnp.concatenate([zeros(MLA_NOPE), a, zeros(HEAD_SLAB - MLA_NOPE - MLA_ROPE)], axis=1)
    pad_heads = lambda a: jnp.pad(a.reshape(-1, GLA_HEADS, GLA_DK),
                                  ((0, 0), (0, 0), (0, HEAD_SLAB - GLA_DK))).reshape(-1, GLA_HEADS * HEAD_SLAB)
    glow_slab = jnp.concatenate([glow, zeros(LANE - GLA_GATE_RANK)], axis=1)
    out = jnp.concatenate([cq, slab(kr), ckv, slab(kr_rot), glow_slab, hq, hf, hi, hgate,
                           pad_heads(gq), pad_heads(gk), gv, ggate, br], axis=1)
    assert out.shape[1] == Z_WIDTH
    return out.astype(BF16)


def _mla_weights(w_uq, w_uk, w_uv):
    hd = MLA_NOPE + MLA_ROPE
    half = MLA_ROPE // 2
    q = w_uq.reshape(MLA_Q_LORA, MLA_HEADS, hd)
    nope, pe = q[..., :MLA_NOPE], q[..., MLA_NOPE:]
    pe_rot = jnp.concatenate([-pe[..., half:], pe[..., :half]], axis=-1)
    z = lambda n: jnp.zeros((MLA_Q_LORA, MLA_HEADS, n), w_uq.dtype)
    wa = jnp.concatenate([nope, pe, z(HEAD_SLAB - hd)], axis=-1).reshape(MLA_Q_LORA, -1)
    wb = jnp.concatenate([z(MLA_NOPE), pe_rot, z(HEAD_SLAB - hd)], axis=-1).reshape(MLA_Q_LORA, -1)
    k = w_uk.reshape(MLA_KV_LORA, MLA_HEADS, MLA_NOPE)
    wka = jnp.pad(k, ((0, 0), (0, 0), (0, HEAD_SLAB - MLA_NOPE))).reshape(MLA_KV_LORA, -1)
    wukt = jnp.pad(k.transpose(1, 2, 0), ((0, 0), (0, HEAD_SLAB - MLA_NOPE), (0, 0)))
    return wa.astype(BF16), wb.astype(BF16), wka.astype(BF16), w_uv.astype(BF16), wukt.astype(BF16)


def _rope_tables(pos):
    half = MLA_ROPE // 2
    inv = ROPE_BASE ** (-jnp.arange(half, dtype=F32) / half)
    ang = pos.astype(F32)[:, None] * inv[None, :]
    cos, sin = jnp.cos(ang), jnp.sin(ang)
    n = pos.shape[0]
    tail = jnp.zeros((n, HEAD_SLAB - MLA_NOPE - MLA_ROPE), F32)
    ct = jnp.concatenate([jnp.ones((n, MLA_NOPE), F32), cos, cos, tail], axis=1)
    st = jnp.concatenate([jnp.zeros((n, MLA_NOPE), F32), sin, sin, tail], axis=1)
    return ct, st


def _mla_common(z_ref, qnw_ref, kvnw_ref, wa_ref, wb_ref, ct_ref, st_ref):
    z = z_ref[...]
    qn = (_rms(z[:, Z_CQ:Z_CQ + MLA_Q_LORA]) * qnw_ref[...]).astype(BF16)
    lat = _rms(z[:, Z_CKV:Z_CKV + MLA_KV_LORA]) * kvnw_ref[...]
    ct, st = ct_ref[...], st_ref[...]
    ct8 = jnp.concatenate([ct] * MLA_HEADS, axis=1)
    st8 = jnp.concatenate([st] * MLA_HEADS, axis=1)
    q_cat = (_dot(qn, wa_ref[...]) * ct8 + _dot(qn, wb_ref[...]) * st8) * MLA_SCALE
    kpe = z[:, Z_KR:Z_KR + HEAD_SLAB] * ct + z[:, Z_KRR:Z_KRR + HEAD_SLAB] * st
    return q_cat, lat, kpe


def _mla_prep_prompt_body(z_ref, qnw_ref, kvnw_ref, wa_ref, wb_ref, ct_ref, st_ref, wka_ref, wv_ref,
                          q_ref, k_ref, v_ref, lat_ref, kpe_ref):
    q_cat, lat, kpe = _mla_common(z_ref, qnw_ref, kvnw_ref, wa_ref, wb_ref, ct_ref, st_ref)
    q_ref[...] = q_cat.astype(BF16)
    lat_ref[...] = lat
    kpe_ref[...] = kpe
    lb = lat.astype(BF16)
    k_ref[...] = (_dot(lb, wka_ref[...]) + jnp.concatenate([kpe] * MLA_HEADS, axis=1)).astype(BF16)
    v_ref[...] = _dot(lb, wv_ref[...]).astype(BF16)


def _mla_prep_sample_body(z_ref, qnw_ref, kvnw_ref, wa_ref, wb_ref, ct_ref, st_ref, wukt_ref,
                          q_ref, qlat_ref, lat_ref, kpe_ref):
    q_cat, lat, kpe = _mla_common(z_ref, qnw_ref, kvnw_ref, wa_ref, wb_ref, ct_ref, st_ref)
    qb = q_cat.astype(BF16)
    q_ref[...] = qb
    lat_ref[...] = lat
    kpe_ref[...] = kpe
    for h in range(MLA_HEADS):
        qlat_ref[:, h * MLA_KV_LORA:(h + 1) * MLA_KV_LORA] = _dot(
            qb[:, h * HEAD_SLAB:(h + 1) * HEAD_SLAB], wukt_ref[h]).astype(BF16)


def _mla_prep(grp, z, q_norm_w, kv_norm_w, mw, ct, st, sample):
    wa, wb, wka, wv, wukt = mw
    tm = grp.tile(512)
    n_tiles = grp.n_tok // tm
    hw = MLA_HEADS * HEAD_SLAB
    full = lambda a: pl.BlockSpec(a.shape, lambda i: (0,) * a.ndim)
    row = lambda w: pl.BlockSpec((tm, w), lambda i: (i, 0))
    if grp.per_token:
        tab = pl.BlockSpec((tm, HEAD_SLAB), lambda i: (i, 0))
    else:
        per = grp.time // tm
        tab = pl.BlockSpec((tm, HEAD_SLAB), lambda i: (i % per, 0))
    qnw = q_norm_w.reshape(1, -1)
    kvnw = kv_norm_w.reshape(1, -1)
    common_in = [pl.BlockSpec((tm, 1024), lambda i: (i, 0)), full(qnw), full(kvnw), full(wa), full(wb), tab, tab]
    n = grp.n_tok
    if sample:
        return pl.pallas_call(
            _mla_prep_sample_body, grid=(n_tiles,),
            in_specs=common_in + [full(wukt)],
            out_specs=[row(hw), row(MLA_HEADS * MLA_KV_LORA), row(MLA_KV_LORA), row(HEAD_SLAB)],
            out_shape=[jax.ShapeDtypeStruct((n, hw), BF16),
                       jax.ShapeDtypeStruct((n, MLA_HEADS * MLA_KV_LORA), BF16),
                       jax.ShapeDtypeStruct((n, MLA_KV_LORA), F32),
                       jax.ShapeDtypeStruct((n, HEAD_SLAB), F32)],
            compiler_params=_cparams(("parallel",)), name="mla_prep_sample",
        )(z, qnw, kvnw, wa, wb, ct, st, wukt)
    return pl.pallas_call(
        _mla_prep_prompt_body, grid=(n_tiles,),
        in_specs=common_in + [full(wka), full(wv)],
        out_specs=[row(hw), row(hw), row(MLA_HEADS * MLA_V), row(MLA_KV_LORA), row(HEAD_SLAB)],
        out_shape=[jax.ShapeDtypeStruct((n, hw), BF16),
                   jax.ShapeDtypeStruct((n, hw), BF16),
                   jax.ShapeDtypeStruct((n, MLA_HEADS * MLA_V), BF16),
                   jax.ShapeDtypeStruct((n, MLA_KV_LORA), F32),
                   jax.ShapeDtypeStruct((n, HEAD_SLAB), F32)],
        compiler_params=_cparams(("parallel",)), name="mla_prep_prompt",
    )(z, qnw, kvnw, wa, wb, ct, st, wka, wv)


def _flash_body(q_ref, k_ref, v_ref, o_ref, m_scr, l_scr, acc_scr, *, tq):
    qi, ki = pl.program_id(1), pl.program_id(2)

    @pl.when(ki == 0)
    def _():
        m_scr[...] = jnp.full(m_scr.shape, NEG_INF, F32)
        l_scr[...] = jnp.zeros(l_scr.shape, F32)
        acc_scr[...] = jnp.zeros(acc_scr.shape, F32)

    low = lax.broadcasted_iota(jnp.int32, (tq, LANE), 1) < MLA_V

    def step(masked):
        if masked:
            keep = (lax.broadcasted_iota(jnp.int32, (tq, tq), 1)
                    <= lax.broadcasted_iota(jnp.int32, (tq, tq), 0))
        for hp in range(MLA_HEADS // 2):
            pv, al = [], []
            for e in range(2):
                h = 2 * hp + e
                s = _dot_nt(q_ref[0, :, h * HEAD_SLAB:(h + 1) * HEAD_SLAB],
                            k_ref[0, :, h * HEAD_SLAB:(h + 1) * HEAD_SLAB])
                if masked:
                    s = jnp.where(keep, s, NEG_INF)
                m_prev = m_scr[h]
                m_new = jnp.maximum(m_prev, jnp.max(s, axis=-1, keepdims=True))
                alpha = jnp.exp(m_prev - m_new)
                p = jnp.exp(s - jnp.concatenate([m_new] * (tq // LANE), axis=1))
                l_scr[h] = alpha * l_scr[h] + jnp.sum(p, axis=-1, keepdims=True)
                m_scr[h] = m_new
                pv.append(_dot(p.astype(BF16), v_ref[0, :, hp * LANE:(hp + 1) * LANE]))
                al.append(alpha)
            sl = slice(hp * LANE, (hp + 1) * LANE)
            acc_scr[:, sl] = jnp.where(low, al[0], al[1]) * acc_scr[:, sl] + jnp.where(low, pv[0], pv[1])

    @pl.when(ki < qi)
    def _():
        step(False)

    @pl.when(ki == qi)
    def _():
        step(True)

    @pl.when(ki == pl.num_programs(2) - 1)
    def _():
        for hp in range(MLA_HEADS // 2):
            sl = slice(hp * LANE, (hp + 1) * LANE)
            o_ref[0, :, sl] = acc_scr[:, sl] / jnp.where(low, l_scr[2 * hp], l_scr[2 * hp + 1])


def _flash(q, k, v, batch, seq, tq):
    hw = MLA_HEADS * HEAD_SLAB
    vw = MLA_HEADS * MLA_V
    nq = seq // tq
    q3, k3, v3 = q.reshape(batch, seq, hw), k.reshape(batch, seq, hw), v.reshape(batch, seq, vw)
    out = pl.pallas_call(
        functools.partial(_flash_body, tq=tq),
        grid=(batch, nq, nq),
        in_specs=[pl.BlockSpec((1, tq, hw), lambda b, i, j: (b, i, 0)),
                  pl.BlockSpec((1, tq, hw), lambda b, i, j: (b, jnp.minimum(i, j), 0)),
                  pl.BlockSpec((1, tq, vw), lambda b, i, j: (b, jnp.minimum(i, j), 0))],
        out_specs=pl.BlockSpec((1, tq, vw), lambda b, i, j: (b, i, 0)),
        out_shape=jax.ShapeDtypeStruct((batch, seq, vw), F32),
        scratch_shapes=[pltpu.VMEM((MLA_HEADS, tq, LANE), F32), pltpu.VMEM((MLA_HEADS, tq, LANE), F32),
                        pltpu.VMEM((tq, vw), F32)],
        compiler_params=_cparams(("parallel", "parallel", "arbitrary")),
        name="flash",
    )(q3, k3, v3)
    return out.reshape(batch * seq, vw)


def _paged_body(pt_ref, q_ref, qlat_ref, nlat_ref, nkpe_ref, lat_hbm, kpe_hbm, o_ref,
                lat_buf, kpe_buf, sem, *, layer, n_pages, n_new):
    b = pl.program_id(0)
    nb = pl.num_programs(0)
    rows = q_ref.shape[1]

    def copies(bb, slot):
        out = []
        for j in range(n_pages):
            pg = pt_ref[bb, j]
            dst = pl.ds(j * PAGE_SIZE, PAGE_SIZE)
            out.append(pltpu.make_async_copy(lat_hbm.at[layer, pg], lat_buf.at[slot, dst], sem.at[0, slot]))
            out.append(pltpu.make_async_copy(kpe_hbm.at[layer, pg], kpe_buf.at[slot, :, dst], sem.at[1, slot]))
        return out

    @pl.when(b == 0)
    def _():
        for c in copies(0, 0):
            c.start()

    slot = b % 2

    @pl.when(b + 1 < nb)
    def _():
        for c in copies(b + 1, 1 - slot):
            c.start()

    for c in copies(b, slot):
        c.wait()

    qlat = qlat_ref[0]
    qpe = q_ref[0][:, MLA_NOPE:MLA_NOPE + MLA_ROPE]
    lat = lat_buf[slot].astype(BF16)
    kpe_t = kpe_buf[slot].astype(BF16)
    s_past = _dot_nt(qlat, lat) + _dot(qpe, kpe_t)
    nlat = nlat_ref[0].astype(BF16)
    nkpe = nkpe_ref[0][:, MLA_NOPE:MLA_NOPE + MLA_ROPE].astype(BF16)
    s_new = _dot_nt(qlat, nlat) + _dot_nt(qpe, nkpe)
    t_of_row = lax.broadcasted_iota(jnp.int32, (rows, n_new), 0) // MLA_HEADS
    s_new = jnp.where(lax.broadcasted_iota(jnp.int32, (rows, n_new), 1) <= t_of_row, s_new, NEG_INF)
    m = jnp.maximum(jnp.max(s_past, axis=-1, keepdims=True), jnp.max(s_new, axis=-1, keepdims=True))
    p_past = jnp.exp(s_past - m)
    p_new = jnp.exp(s_new - m)
    denom = jnp.sum(p_past, axis=-1, keepdims=True) + jnp.sum(p_new, axis=-1, keepdims=True)
    o = _dot(p_past.astype(BF16), lat) + _dot(p_new.astype(BF16), nlat)
    o_ref[0] = o / denom


def _paged(page_table, q_cat, q_lat, lat_new, kpe_new, cache_latent, cache_k_rope, layer, batch, n_new):
    rows = n_new * MLA_HEADS
    n_pages = page_table.shape[1]
    past = n_pages * PAGE_SIZE
    q3 = q_cat.reshape(batch, rows, HEAD_SLAB)
    ql3 = q_lat.reshape(batch, rows, MLA_KV_LORA)
    nl3 = lat_new.reshape(batch, n_new, MLA_KV_LORA)
    nk3 = kpe_new.reshape(batch, n_new, HEAD_SLAB)
    blk = lambda r, w: pl.BlockSpec((1, r, w), lambda b, pt: (b, 0, 0))
    out = pl.pallas_call(
        functools.partial(_paged_body, layer=layer, n_pages=n_pages, n_new=n_new),
        grid_spec=pltpu.PrefetchScalarGridSpec(
            num_scalar_prefetch=1, grid=(batch,),
            in_specs=[blk(rows, HEAD_SLAB), blk(rows, MLA_KV_LORA), blk(n_new, MLA_KV_LORA), blk(n_new, HEAD_SLAB),
                      pl.BlockSpec(memory_space=pl.ANY), pl.BlockSpec(memory_space=pl.ANY)],
            out_specs=blk(rows, MLA_KV_LORA),
            scratch_shapes=[pltpu.VMEM((2, past, MLA_KV_LORA), F32), pltpu.VMEM((2, MLA_ROPE, past), F32),
                            pltpu.SemaphoreType.DMA((2, 2))]),
        out_shape=jax.ShapeDtypeStruct((batch, rows, MLA_KV_LORA), F32),
        compiler_params=_cparams(("arbitrary",)),
        name="paged",
    )(page_table, q3, ql3, nl3, nk3, cache_latent, jnp.swapaxes(cache_k_rope, 2, 3))
    return out.reshape(batch * n_new, MLA_HEADS * MLA_KV_LORA)


def _head_proj_body(o_ref, w_ref, y_ref):
    for h in range(MLA_HEADS):
        y_ref[:, h * MLA_V:(h + 1) * MLA_V] = _dot(
            o_ref[:, h * MLA_KV_LORA:(h + 1) * MLA_KV_LORA].astype(BF16), w_ref[h])


def _head_proj(o_lat, w_uv):
    n = o_lat.shape[0]
    w = w_uv.reshape(MLA_KV_LORA, MLA_HEADS, MLA_V).transpose(1, 0, 2).astype(BF16)
    return pl.pallas_call(
        _head_proj_body, grid=(1,),
        in_specs=[pl.BlockSpec(o_lat.shape, lambda i: (0, 0)), pl.BlockSpec(w.shape, lambda i: (0, 0, 0))],
        out_specs=pl.BlockSpec((n, MLA_HEADS * MLA_V), lambda i: (0, 0)),
        out_shape=jax.ShapeDtypeStruct((n, MLA_HEADS * MLA_V), F32),
        compiler_params=_cparams(("arbitrary",)), name="head_proj",
    )(o_lat, w)


N_REC_HEADS = 4
REC_WIDTH = N_REC_HEADS * HEAD_SLAB


def _split3(x):
    a = x.astype(BF16)
    r = x - a.astype(F32)
    b = r.astype(BF16)
    c = (r - b.astype(F32)).astype(BF16)
    return a, b, c


def _scan_body(*refs, gla, chunk, sub, valid, has_s0, k_dim):
    if gla:
        q_ref, k_ref, v_ref, glow_ref, wg_ref, bg_ref, tri_ref = refs[:7]
        rest = refs[7:]
    else:
        q_ref, k_ref, v_ref, la_ref, l1_ref, tri_ref = refs[:6]
        rest = refs[6:]
    if has_s0:
        s0_ref, o_ref, sfin_ref, st_scr = rest
    else:
        o_ref, sfin_ref, st_scr = rest
    ci = pl.program_id(1)

    @pl.when(ci == 0)
    def _():
        if has_s0:
            for h in range(N_REC_HEADS):
                s0 = s0_ref[0, h]
                if k_dim < HEAD_SLAB:
                    s0 = jnp.concatenate([s0, jnp.zeros((HEAD_SLAB - k_dim, s0.shape[1]), F32)], axis=0)
                st_scr[h] = s0.T
        else:
            st_scr[...] = jnp.zeros(st_scr.shape, F32)

    v = v_ref[...]
    if gla:
        q = q_ref[...] * (GLA_DK ** -0.5)
        k = k_ref[...]
        g = _log_sigmoid(_dot(glow_ref[...].astype(BF16), wg_ref[...]) + bg_ref[...]) * (1.0 / GLA_GATE_NORMALIZER)
    else:
        xq = q_ref[...]
        q = xq * jax.nn.sigmoid(xq) * (HG_DK ** -0.5)
        a = la_ref[...]
        bb = l1_ref[...] + _log_sigmoid(k_ref[...])
        g = jnp.maximum(a, bb) + jnp.log1p(jnp.exp(-jnp.abs(a - bb)))
        k = 1.0 - jnp.exp(g)
    if valid < chunk:
        live = lax.broadcasted_iota(jnp.int32, (chunk, 1), 0) < valid
        g = jnp.where(live, g, 0.0)
        k = jnp.where(live, k, 0.0)

    tri = tri_ref[...]
    g1, g2, g3 = _split3(g)
    b = _dot(tri, g1) + _dot(tri, g2) + _dot(tri, g3)
    b_end = b[chunk - 1:chunk]
    qe = (q * jnp.exp(b)).astype(BF16)
    kd = (k * jnp.exp(b_end - b)).astype(BF16)
    vb = v.astype(BF16)
    hs = lambda h: slice(h * HEAD_SLAB, (h + 1) * HEAD_SLAB)

    o_inter = jnp.concatenate(
        [_dot_nt(qe[:, hs(h)], st_scr[h].astype(BF16)) for h in range(N_REC_HEADS)], axis=1)

    row_in_sub = lax.broadcasted_iota(jnp.int32, (sub, 1), 0)
    blocks = []
    for i in range(chunk // sub):
        lo = i * sub
        if lo >= valid:
            blocks.append(o_inter[lo:lo + sub])
            continue
        bi, qi_, ki_, vi = b[lo:lo + sub], q[lo:lo + sub], k[lo:lo + sub], v[lo:lo + sub]
        blk = o_inter[lo:lo + sub]
        if i > 0:
            r = b[lo - 1:lo]
            qt = (qi_ * jnp.exp(bi - r)).astype(BF16)
            kt = (k[:lo] * jnp.exp(r - b[:lo])).astype(BF16)
            off = []
            for h in range(N_REC_HEADS):
                att = _dot_nt(qt[:, hs(h)], kt[:, hs(h)])
                off.append(_dot(att.astype(BF16), vb[:lo, hs(h)]))
            blk = blk + jnp.concatenate(off, axis=1)
        for s in range(min(sub, valid - lo)):
            e = jnp.exp(jnp.minimum(bi - bi[s:s + 1], 0.0))
            w = qi_ * (ki_[s:s + 1] * e)
            parts = []
            for h in range(N_REC_HEADS):
                a_ts = jnp.sum(w[:, hs(h)], axis=-1, keepdims=True)
                parts.append(jnp.where(row_in_sub >= s, a_ts, 0.0) * vi[s:s + 1, hs(h)])
            blk = blk + jnp.concatenate(parts, axis=1)
        blocks.append(blk)
    o_ref[...] = jnp.concatenate(blocks, axis=0)

    decay = jnp.exp(b_end)
    for h in range(N_REC_HEADS):
        st_scr[h] = st_scr[h] * decay[:, hs(h)] + _dot_tn(vb[:, hs(h)], kd[:, hs(h)])

    @pl.when(ci == pl.num_programs(1) - 1)
    def _():
        for h in range(N_REC_HEADS):
            sfin_ref[0, h] = st_scr[h].T[:k_dim]


def _scan(z, batch, time, chunk, sub, valid, gla, extra, s0):
    k_dim = GLA_DK if gla else HG_DK
    nck = time // chunk
    cb = lambda col: pl.BlockSpec((chunk, REC_WIDTH), lambda b, c: (b * nck + c, col // REC_WIDTH))
    full = lambda a: pl.BlockSpec(a.shape, lambda b, c: (0,) * a.ndim)
    tri = jnp.tril(jnp.ones((chunk, chunk), F32)).astype(BF16)
    if gla:
        wg, bg = extra
        ins = [z, z, z, z, wg, bg, tri]
        specs = [cb(Z_GQ), cb(Z_GK), cb(Z_GV),
                 pl.BlockSpec((chunk, LANE), lambda b, c: (b * nck + c, Z_GLOW // LANE)), full(wg), full(bg), full(tri)]
    else:
        la, l1 = extra
        ins = [z, z, z, la, l1, tri]
        specs = [cb(Z_HQ), cb(Z_HF), cb(Z_HI), full(la), full(l1), full(tri)]
    if s0 is not None:
        ins.append(s0)
        specs.append(pl.BlockSpec((1, N_REC_HEADS, k_dim, HEAD_SLAB), lambda b, c: (b, 0, 0, 0)))
    return pl.pallas_call(
        functools.partial(_scan_body, gla=gla, chunk=chunk, sub=sub, valid=valid, has_s0=s0 is not None,
                          k_dim=k_dim),
        grid=(batch, nck),
        in_specs=specs,
        out_specs=[pl.BlockSpec((chunk, REC_WIDTH), lambda b, c: (b * nck + c, 0)),
                   pl.BlockSpec((1, N_REC_HEADS, k_dim, HEAD_SLAB), lambda b, c: (b, 0, 0, 0))],
        out_shape=[jax.ShapeDtypeStruct((batch * time, REC_WIDTH), F32),
                   jax.ShapeDtypeStruct((batch, N_REC_HEADS, k_dim, HEAD_SLAB), F32)],
        scratch_shapes=[pltpu.VMEM((N_REC_HEADS, HEAD_SLAB, HEAD_SLAB), F32)],
        compiler_params=_cparams(("parallel", "arbitrary")),
        name="scan_gla" if gla else "scan_hgrn",
    )(*ins)


def _merge_body(x_ref, ym_ref, oh_ref, og_ref, hgate_ref, ggate_ref, br0_ref, br1_ref, br2_ref,
                g1_ref, sc2_ref, sh2_ref, hgw_ref, glw_ref, nfw_ref, wbm_ref, wbh_ref, wbg_ref, wout_ref,
                rwh_ref, rwl_ref, rb_ref, x1_ref, h2_ref, lg_ref):
    silu = lambda t: t * jax.nn.sigmoid(t)
    yh = _rms(oh_ref[...]) * hgw_ref[...] * silu(hgate_ref[...])
    og = og_ref[...]
    glw = glw_ref[...]
    yg = jnp.concatenate([_rms(og[:, h * GLA_DV:(h + 1) * GLA_DV]) * glw for h in range(GLA_HEADS)], axis=1)
    yg = yg * silu(ggate_ref[...])
    m = (jax.nn.sigmoid(br0_ref[...]) * _dot(ym_ref[...].astype(BF16), wbm_ref[...])
         + jax.nn.sigmoid(br1_ref[...]) * _dot(yh.astype(BF16), wbh_ref[...])
         + jax.nn.sigmoid(br2_ref[...]) * _dot(yg.astype(BF16), wbg_ref[...]))
    x1 = x_ref[...] + g1_ref[0] * _dot(m.astype(BF16), wout_ref[...])
    x1_ref[...] = x1
    h2 = _rms(x1) * nfw_ref[...] * (1.0 + sc2_ref[0]) + sh2_ref[0]
    h2_ref[...] = h2
    hh = h2.astype(BF16)
    hl = (h2 - hh.astype(F32)).astype(BF16)
    rwh = rwh_ref[...]
    lg_ref[...] = _dot(hh, rwh) + _dot(hh, rwl_ref[...]) + _dot(hl, rwh) + rb_ref[...]


def _merge(grp, x, y_mla, o_hg, o_gla, z, g1, sc2, sh2, lw):
    tm = grp.tile(256)
    row = lambda w: pl.BlockSpec((tm, w), lambda i: (i, 0))
    zc = lambda col, w: pl.BlockSpec((tm, w), lambda i: (i, col // w))
    full = lambda a: pl.BlockSpec(a.shape, lambda i: (0,) * a.ndim)
    g1_a, g1_s = grp.mod(g1, tm)
    sc_a, sc_s = grp.mod(sc2, tm)
    sh_a, sh_s = grp.mod(sh2, tm)
    ws = [lw["hg_norm_w"], lw["gla_norm_w"], lw["norm_ffn_w"], lw["w_br_mla"], lw["w_br_hg"], lw["w_br_gla"],
          lw["w_out"], lw["rw_hi"], lw["rw_lo"], lw["rb"]]
    n = grp.n_tok
    return pl.pallas_call(
        _merge_body, grid=(grp.n_tok // tm,),
        in_specs=[row(D_MODEL), row(512), row(512), row(512), zc(Z_HGATE, 512), zc(Z_GGATE, 512),
                  zc(Z_BR, 1024), zc(Z_BR + 1024, 1024), zc(Z_BR + 2048, 1024),
                  g1_s, sc_s, sh_s] + [full(w) for w in ws],
        out_specs=[row(D_MODEL), row(D_MODEL), row(LANE)],
        out_shape=[jax.ShapeDtypeStruct((n, D_MODEL), F32), jax.ShapeDtypeStruct((n, D_MODEL), F32),
                   jax.ShapeDtypeStruct((n, LANE), F32)],
        compiler_params=_cparams(("parallel",)), name="merge",
    )(x, y_mla, o_hg, o_gla, z, z, z, z, z, g1_a, sc_a, sh_a, *ws)


def _experts_body(be_ref, nu_ref, x_ref, sw_ref, wgu_ref, bgu_ref, wd_ref, bd_ref, y_ref, wgu_scr, wd_scr):
    i = pl.program_id(0)
    e = be_ref[i]
    prev = be_ref[jnp.maximum(i - 1, 0)]

    @pl.when(jnp.logical_or(i == 0, e != prev))
    def _():
        wgu_scr[...] = wgu_ref[0].astype(BF16)
        wd_scr[...] = wd_ref[0].astype(BF16)

    @pl.when(i < nu_ref[0])
    def _():
        gu = _dot(x_ref[...].astype(BF16), wgu_scr[...]) + bgu_ref[0]
        gate = jnp.minimum(gu[:, :D_EXPERT], SWIGLU_LIMIT)
        up = jnp.clip(gu[:, D_EXPERT:], -SWIGLU_LIMIT, SWIGLU_LIMIT)
        act = (up + 1.0) * gate * jax.nn.sigmoid(SWIGLU_ALPHA * gate)
        y_ref[...] = (_dot(act.astype(BF16), wd_scr[...]) + bd_ref[0]) * sw_ref[...]

    @pl.when(i >= nu_ref[0])
    def _():
        y_ref[...] = jnp.zeros(y_ref.shape, F32)


def _experts(block_e, n_used, xb, slot_w, w_gate_up, b_gate_up, w_down, b_down, layer):
    n_slots = xb.shape[0]
    tm = MOE_TM
    ne = N_EXPERTS
    return pl.pallas_call(
        _experts_body,
        grid_spec=pltpu.PrefetchScalarGridSpec(
            num_scalar_prefetch=2, grid=(n_slots // tm,),
            in_specs=[pl.BlockSpec((tm, D_MODEL), lambda i, be, nu: (i, 0)),
                      pl.BlockSpec((tm, 1), lambda i, be, nu: (i, 0)),
                      pl.BlockSpec((1, D_MODEL, 2 * D_EXPERT), lambda i, be, nu: (layer * ne + be[i], 0, 0)),
                      pl.BlockSpec((1, 1, 2 * D_EXPERT), lambda i, be, nu: (layer * ne + be[i], 0, 0)),
                      pl.BlockSpec((1, D_EXPERT, D_MODEL), lambda i, be, nu: (layer * ne + be[i], 0, 0)),
                      pl.BlockSpec((1, 1, D_MODEL), lambda i, be, nu: (layer * ne + be[i], 0, 0))],
            out_specs=pl.BlockSpec((tm, D_MODEL), lambda i, be, nu: (i, 0)),
            scratch_shapes=[pltpu.VMEM((D_MODEL, 2 * D_EXPERT), BF16), pltpu.VMEM((D_EXPERT, D_MODEL), BF16)]),
        out_shape=jax.ShapeDtypeStruct((n_slots, D_MODEL), F32),
        compiler_params=_cparams(("arbitrary",)),
        name="experts",
    )(block_e, n_used, xb, slot_w,
      w_gate_up.reshape(DEPTH * ne, D_MODEL, 2 * D_EXPERT), b_gate_up.reshape(DEPTH * ne, 1, 2 * D_EXPERT),
      w_down.reshape(DEPTH * ne, D_EXPERT, D_MODEL), b_down.reshape(DEPTH * ne, 1, D_MODEL))


def _route(logits):
    n = logits.shape[0]
    nk = n * TOP_K
    tm = MOE_TM
    top_v, top_e = lax.top_k(logits, TOP_K)
    wts = jax.nn.softmax(top_v, axis=-1).reshape(-1)
    flat_e = top_e.reshape(-1).astype(jnp.int32)
    flat_tok = jnp.arange(nk, dtype=jnp.int32) // TOP_K
    onehot = (flat_e[None, :] == jnp.arange(N_EXPERTS, dtype=jnp.int32)[:, None]).astype(jnp.int32)
    counts = jnp.sum(onehot, axis=1)
    rank = jnp.sum(jnp.cumsum(onehot, axis=1) * onehot, axis=0) - 1
    padded = (counts + tm - 1) // tm * tm
    pad_end = jnp.cumsum(padded)
    pad_start = pad_end - padded
    start = jnp.cumsum(counts) - counts
    dest = jnp.sum(onehot * pad_start[:, None], axis=0) + rank
    _, stok, sw = lax.sort((dest, flat_tok, wts), num_keys=1)
    n_blocks = (nk + N_EXPERTS * (tm - 1) + tm - 1) // tm
    block_e = jnp.minimum(jnp.searchsorted(pad_end, jnp.arange(n_blocks, dtype=jnp.int32) * tm, side='right'),
                          N_EXPERTS - 1).astype(jnp.int32)
    slot = jnp.arange(n_blocks * tm, dtype=jnp.int32)
    slot_e = jnp.repeat(block_e, tm)
    off = slot - pad_start[slot_e]
    live = off < counts[slot_e]
    src = jnp.clip(start[slot_e] + off, 0, nk - 1)
    slot_tok = jnp.where(live, stok[src], 0)
    slot_w = jnp.where(live, sw[src], 0.0)
    n_used = (pad_end[-1] // tm).astype(jnp.int32).reshape(1)
    return slot_tok, slot_w.reshape(-1, 1), block_e, n_used, dest.reshape(n, TOP_K)


def _final_norm_body(x_ref, w_ref, o_ref):
    o_ref[...] = _rms(x_ref[...]) * w_ref[...]


def _final_norm(x, w, tm):
    n = x.shape[0]
    return pl.pallas_call(
        _final_norm_body, grid=(n // tm,),
        in_specs=[pl.BlockSpec((tm, D_MODEL), lambda i: (i, 0)), pl.BlockSpec((1, D_MODEL), lambda i: (0, 0))],
        out_specs=pl.BlockSpec((tm, D_MODEL), lambda i: (i, 0)),
        out_shape=jax.ShapeDtypeStruct((n, D_MODEL), F32),
        compiler_params=_cparams(("parallel",)), name="final_norm",
    )(x, w.reshape(1, D_MODEL))


def kernel(x_prompt, x_sample, cache_latent, cache_k_rope, state_hgrn, state_gla, page_table, c_prompt, c_sample, norm_mix_w, norm_ffn_w, final_norm_w, w_ada, b_ada, w_in, mla_q_norm_w, mla_w_uq, mla_kv_norm_w, mla_w_uk, mla_w_uv, hgrn_lower_bounds, hgrn_norm_w, gla_w_gate, gla_b_gate, gla_norm_w, w_branch_mla, w_branch_hgrn, w_branch_gla, w_out, router_w, router_b, w_gate_up, b_gate_up, w_down, b_down):
    bp, seq, _ = x_prompt.shape
    bs, tnew, _ = x_sample.shape
    n_p, n_s = bp * seq, bs * tnew
    past_len = page_table.shape[1] * cache_latent.shape[2]
    gp = _Group(bp, seq, per_token=False)
    gs = _Group(bs, tnew, per_token=True)
    s_pad = SCAN_SUB

    rows = bp + bs
    rows_pad = -(-rows // 8) * 8
    c_all = jnp.pad(jnp.concatenate([c_prompt, c_sample], axis=0), ((0, rows_pad - rows), (0, 0)))
    mod = _adaln(c_all, w_ada, b_ada)

    lbs = jax.nn.softmax(hgrn_lower_bounds.astype(F32), axis=0)
    lbs = jnp.cumsum(lbs, axis=0) - lbs[0]
    log_lb = jnp.log(jnp.maximum(lbs, TINY))
    log_1m_lb = jnp.log1p(-lbs)

    ct_p, st_p = _rope_tables(jnp.arange(seq, dtype=jnp.int32))
    pos_s = past_len + jnp.arange(tnew, dtype=jnp.int32)
    ct_s, st_s = _rope_tables(jnp.tile(pos_s, bs))

    xp = x_prompt.reshape(n_p, D_MODEL)
    xs = x_sample.reshape(n_s, D_MODEL)
    lat_p, kpe_p, hg_p, gla_p, lat_s, kpe_s, hg_s, gla_s = [], [], [], [], [], [], [], []

    for l in range(DEPTH):
        w_pad = _pad_w_in(w_in[l])
        mw = _mla_weights(mla_w_uq[l], mla_w_uk[l], mla_w_uv[l])
        rw = jnp.pad(router_w[l], ((0, 0), (0, LANE - N_EXPERTS)))
        rw_hi = rw.astype(BF16)
        lw = dict(hg_norm_w=hgrn_norm_w[l].reshape(1, -1), gla_norm_w=gla_norm_w[l].reshape(1, -1),
                  norm_ffn_w=norm_ffn_w[l].reshape(1, -1), w_br_mla=w_branch_mla[l].astype(BF16),
                  w_br_hg=w_branch_hgrn[l].astype(BF16), w_br_gla=w_branch_gla[l].astype(BF16),
                  w_out=w_out[l].astype(BF16), rw_hi=rw_hi, rw_lo=(rw - rw_hi.astype(F32)).astype(BF16),
                  rb=jnp.pad(router_b[l], (0, LANE - N_EXPERTS)).reshape(1, LANE))
        wg = jnp.pad(gla_w_gate[l].reshape(GLA_GATE_RANK, GLA_HEADS, GLA_DK),
                     ((0, LANE - GLA_GATE_RANK), (0, 0), (0, HEAD_SLAB - GLA_DK))).reshape(LANE, REC_WIDTH).astype(BF16)
        bg = jnp.pad(gla_b_gate[l].reshape(GLA_HEADS, GLA_DK), ((0, 0), (0, HEAD_SLAB - GLA_DK))).reshape(1, REC_WIDTH)
        hg_extra = (log_lb[l].reshape(1, -1), log_1m_lb[l].reshape(1, -1))
        split6 = lambda m: jnp.split(m, 6, axis=-1)
        sh1p, sc1p, g1p, sh2p, sc2p, g2p = split6(mod[l, :bp])
        sh1s, sc1s, g1s, sh2s, sc2s, g2s = split6(mod[l, bp:rows])

        zp = _in_proj(gp, xp, norm_mix_w[l], sc1p, sh1p, w_pad)
        q, k, v, latp, kpep = _mla_prep(gp, zp, mla_q_norm_w[l], mla_kv_norm_w[l], mw, ct_p, st_p, sample=False)
        y_mla_p = _flash(q, k, v, bp, seq, tq=512)
        o_hg_p, s_hg_p = _scan(zp, bp, seq, SCAN_CHUNK, SCAN_SUB, SCAN_CHUNK, False, hg_extra, None)
        o_gl_p, s_gl_p = _scan(zp, bp, seq, SCAN_CHUNK, SCAN_SUB, SCAN_CHUNK, True, (wg, bg), None)
        x1p, h2p, lgp = _merge(gp, xp, y_mla_p, o_hg_p, o_gl_p, zp, g1p, sc2p, sh2p, lw)

        zs = _in_proj(gs, xs, norm_mix_w[l], sc1s, sh1s, w_pad)
        qs, qlat, lats, kpes = _mla_prep(gs, zs, mla_q_norm_w[l], mla_kv_norm_w[l], mw, ct_s, st_s, sample=True)
        o_lat = _paged(page_table, qs, qlat, lats, kpes, cache_latent, cache_k_rope, l, bs, tnew)
        y_mla_s = _head_proj(o_lat, mla_w_uv[l])
        zs_pad = jnp.pad(zs.reshape(bs, tnew, Z_WIDTH), ((0, 0), (0, s_pad - tnew), (0, 0))).reshape(bs * s_pad, Z_WIDTH)
        unpad = lambda o: o.reshape(bs, s_pad, -1)[:, :tnew].reshape(n_s, -1)
        o_hg_s, s_hg_s = _scan(zs_pad, bs, s_pad, s_pad, SCAN_SUB, tnew, False, hg_extra, state_hgrn[l])
        o_gl_s, s_gl_s = _scan(zs_pad, bs, s_pad, s_pad, SCAN_SUB, tnew, True, (wg, bg), state_gla[l])
        x1s, h2s, lgs = _merge(gs, xs, y_mla_s, unpad(o_hg_s), unpad(o_gl_s), zs, g1s, sc2s, sh2s, lw)

        h2 = jnp.concatenate([h2p, h2s], axis=0)
        logits = jnp.concatenate([lgp, lgs], axis=0)[:, :N_EXPERTS]
        slot_tok, slot_w, block_e, n_used, dest_tk = _route(logits)
        yb = _experts(block_e, n_used, h2[slot_tok], slot_w, w_gate_up, b_gate_up, w_down, b_down, l)
        y = yb[dest_tk[:, 0]] + yb[dest_tk[:, 1]] + yb[dest_tk[:, 2]] + yb[dest_tk[:, 3]]
        xp = x1p + jnp.repeat(g2p, seq, axis=0) * y[:n_p]
        xs = x1s + jnp.repeat(g2s, tnew, axis=0) * y[n_p:]

        sl = slice(MLA_NOPE, MLA_NOPE + MLA_ROPE)
        lat_p.append(latp.reshape(bp, seq, -1)); kpe_p.append(kpep[:, sl].reshape(bp, seq, -1))
        hg_p.append(s_hg_p); gla_p.append(s_gl_p)
        lat_s.append(lats.reshape(bs, tnew, -1)); kpe_s.append(kpes[:, sl].reshape(bs, tnew, -1))
        hg_s.append(s_hg_s); gla_s.append(s_gl_s)

    y_prompt = _final_norm(xp, final_norm_w, 1024).reshape(bp, seq, D_MODEL)
    y_sample = _final_norm(xs, final_norm_w, n_s).reshape(bs, tnew, D_MODEL)
    st = jnp.stack
    return (y_prompt, y_sample, st(lat_p), st(kpe_p), st(hg_p), st(gla_p),
            st(lat_s), st(kpe_s), st(hg_s), st(gla_s))
```

```python
import functools

import numpy as np
import jax
import jax.numpy as jnp
from jax import lax
from jax.experimental import pallas as pl
from jax.experimental.pallas import tpu as pltpu

F32 = jnp.float32
BF16 = jnp.bfloat16

D_MODEL = 1024
DEPTH = 2
PAGE_SIZE = 128
MLA_HEADS = 8
MLA_NOPE = 64
MLA_ROPE = 32
MLA_V = 64
MLA_Q_LORA = 384
MLA_KV_LORA = 256
MLA_SCALE = (MLA_NOPE + MLA_ROPE) ** -0.5
ROPE_BASE = 10000.0
HG_HEADS = 4
HG_DK = 128
HG_DV = 128
GLA_HEADS = 4
GLA_DK = 64
GLA_DV = 128
GLA_GATE_RANK = 16
GLA_GATE_NORMALIZER = 16.0
N_EXPERTS = 32
TOP_K = 4
D_EXPERT = 1024
SWIGLU_LIMIT = 7.0
SWIGLU_ALPHA = 1.702
EPS = 1e-6
NEG_INF = -1e30
TINY = 1e-30

IN_SPLITS = (MLA_Q_LORA, MLA_KV_LORA, MLA_ROPE, 512, 512, 512, 512, 256, 256, 512, 512, GLA_GATE_RANK,
             3 * D_MODEL)

LANE = 128
SUBLANES = 8
HEAD_SLAB = 128
VMEM_LIMIT = 56 * 1024 * 1024

Z_WIDTH = 8192
Z_CQ, Z_KR, Z_CKV, Z_KRR, Z_GLOW = 0, 384, 512, 768, 896
Z_HQ, Z_HF, Z_HI, Z_HGATE = 1024, 1536, 2048, 2560
Z_GQ, Z_GK, Z_GV, Z_GGATE, Z_BR = 3072, 3584, 4096, 4608, 5120

SCAN_CHUNK = 64
SCAN_SUB = 16
MOE_TM = 256


def _cparams(sem, vmem=VMEM_LIMIT):
    return pltpu.CompilerParams(dimension_semantics=sem, vmem_limit_bytes=vmem)


def _dot(a, b):
    return jnp.dot(a, b, preferred_element_type=F32)


def _dot_nt(a, b):
    return lax.dot_general(a, b, (((1,), (1,)), ((), ())), preferred_element_type=F32)


def _dot_tn(a, b):
    return lax.dot_general(a, b, (((0,), (0,)), ((), ())), preferred_element_type=F32)


def _rms(x):
    return x * lax.rsqrt(jnp.mean(x * x, axis=-1, keepdims=True) + EPS)


def _log_sigmoid(x):
    return jnp.minimum(x, 0.0) - jnp.log1p(jnp.exp(-jnp.abs(x)))


def _adaln_body(c_ref, w_ref, b_ref, o_ref):
    c = c_ref[...]
    a = (c * jax.nn.sigmoid(c)).astype(BF16)
    o_ref[0] = _dot(a, w_ref[0].astype(BF16)) + b_ref[0]


def _adaln(c_all, w_ada, b_ada):
    rows = c_all.shape[0]
    tn = 1536
    n_out = w_ada.shape[-1]
    return pl.pallas_call(
        _adaln_body,
        grid=(DEPTH, n_out // tn),
        in_specs=[pl.BlockSpec((rows, D_MODEL), lambda l, j: (0, 0)),
                  pl.BlockSpec((1, D_MODEL, tn), lambda l, j: (l, 0, j)),
                  pl.BlockSpec((1, 1, tn), lambda l, j: (l, 0, j))],
        out_specs=pl.BlockSpec((1, rows, tn), lambda l, j: (l, 0, j)),
        out_shape=jax.ShapeDtypeStruct((DEPTH, rows, n_out), F32),
        compiler_params=_cparams(("parallel", "parallel")),
        name="adaln",
    )(c_all, w_ada, b_ada.reshape(DEPTH, 1, n_out))


class _Group:
    def __init__(self, batch, time, per_token):
        self.batch, self.time, self.per_token = batch, time, per_token
        self.n_tok = batch * time

    def tile(self, tm):
        tm = min(tm, self.n_tok)
        assert (self.n_tok if self.per_token else self.time) % tm == 0
        return tm

    def mod(self, m, tm):
        if self.per_token:
            arr = jnp.repeat(m, self.time, axis=0)[None]
            return arr, pl.BlockSpec((1, tm, m.shape[-1]), lambda *g: (0, g[0], 0))
        arr = m[:, None, :]
        per = self.time // tm
        return arr, pl.BlockSpec((1, 1, m.shape[-1]), lambda *g: (g[0] // per, 0, 0))


def _in_proj_body(x_ref, nw_ref, sc_ref, sh_ref, w_ref, z_ref, h_scr):
    @pl.when(pl.program_id(1) == 0)
    def _():
        h = _rms(x_ref[...]) * nw_ref[...]
        h_scr[...] = (h * (1.0 + sc_ref[0]) + sh_ref[0]).astype(BF16)

    z_ref[...] = _dot(h_scr[...], w_ref[...])


def _in_proj(grp, x, norm_w, sc, sh, w_pad):
    tm, tn = grp.tile(1024), 1024
    sc_a, sc_s = grp.mod(sc, tm)
    sh_a, sh_s = grp.mod(sh, tm)
    return pl.pallas_call(
        _in_proj_body,
        grid=(grp.n_tok // tm, Z_WIDTH // tn),
        in_specs=[pl.BlockSpec((tm, D_MODEL), lambda i, j: (i, 0)),
                  pl.BlockSpec((1, D_MODEL), lambda i, j: (0, 0)),
                  sc_s, sh_s,
                  pl.BlockSpec((D_MODEL, tn), lambda i, j: (0, j))],
        out_specs=pl.BlockSpec((tm, tn), lambda i, j: (i, j)),
        out_shape=jax.ShapeDtypeStruct((grp.n_tok, Z_WIDTH), F32),
        scratch_shapes=[pltpu.VMEM((tm, D_MODEL), BF16)],
        compiler_params=_cparams(("parallel", "arbitrary")),
        name="in_proj",
    )(x, norm_w.reshape(1, D_MODEL), sc_a, sh_a, w_pad)


def _pad_w_in(w):
    idx = np.cumsum(IN_SPLITS)[:-1].tolist()
    cq, ckv, kr, hq, hf, hi, hgate, gq, gk, gv, ggate, glow, br = jnp.split(w, idx, axis=1)
    zeros = lambda n: jnp.zeros((w.shape[0], n), w.dtype)
    half = MLA_ROPE // 2
    kr_rot = jnp.concatenate([-kr[:, half:], kr[:, :half]], axis=1)
    slab = lambda a: jnp.concatenate([zeros(MLA_NOPE), a, zeros(HEAD_SLAB - MLA_NOPE - MLA_ROPE)], axis=1)
    pad_heads = lambda a: jnp.pad(a.reshape(-1, GLA_HEADS, GLA_DK),
                                  ((0, 0), (0, 0), (0, HEAD_SLAB - GLA_DK))).reshape(-1, GLA_HEADS * HEAD_SLAB)
    glow_slab = jnp.concatenate([glow, zeros(LANE - GLA_GATE_RANK)], axis=1)
    out = jnp.concatenate([cq, slab(kr), ckv, slab(kr_rot), glow_slab, hq, hf, hi, hgate,
                           pad_heads(gq), pad_heads(gk), gv, ggate, br], axis=1)
    assert out.shape[1] == Z_WIDTH
    return out.astype(BF16)


def _mla_weights(w_uq, w_uk, w_uv):
    hd = MLA_NOPE + MLA_ROPE
    half = MLA_ROPE // 2
    q = w_uq.reshape(MLA_Q_LORA, MLA_HEADS, hd)
    nope, pe = q[..., :MLA_NOPE], q[..., MLA_NOPE:]
    pe_rot = jnp.concatenate([-pe[..., half:], pe[..., :half]], axis=-1)
    z = lambda n: jnp.zeros((MLA_Q_LORA, MLA_HEADS, n), w_uq.dtype)
    wa = jnp.concatenate([nope, pe, z(HEAD_SLAB - hd)], axis=-1).reshape(MLA_Q_LORA, -1)
    wb = jnp.concatenate([z(MLA_NOPE), pe_rot, z(HEAD_SLAB - hd)], axis=-1).reshape(MLA_Q_LORA, -1)
    k = w_uk.reshape(MLA_KV_LORA, MLA_HEADS, MLA_NOPE)
    wka = jnp.pad(k, ((0, 0), (0, 0), (0, HEAD_SLAB - MLA_NOPE))).reshape(MLA_KV_LORA, -1)
    wukt = jnp.pad(k.transpose(1, 2, 0), ((0, 0), (0, HEAD_SLAB - MLA_NOPE), (0, 0)))
    return wa.astype(BF16), wb.astype(BF16), wka.astype(BF16), w_uv.astype(BF16), wukt.astype(BF16)


def _rope_tables(pos):
    half = MLA_ROPE // 2
    inv = ROPE_BASE ** (-jnp.arange(half, dtype=F32) / half)
    ang = pos.astype(F32)[:, None] * inv[None, :]
    cos, sin = jnp.cos(ang), jnp.sin(ang)
    n = pos.shape[0]
    tail = jnp.zeros((n, HEAD_SLAB - MLA_NOPE - MLA_ROPE), F32)
    ct = jnp.concatenate([jnp.ones((n, MLA_NOPE), F32), cos, cos, tail], axis=1)
    st = jnp.concatenate([jnp.zeros((n, MLA_NOPE), F32), sin, sin, tail], axis=1)
    return ct, st


def _mla_common(z_ref, qnw_ref, kvnw_ref, wa_ref, wb_ref, ct_ref, st_ref):
    z = z_ref[...]
    qn = (_rms(z[:, Z_CQ:Z_CQ + MLA_Q_LORA]) * qnw_ref[...]).astype(BF16)
    lat = _rms(z[:, Z_CKV:Z_CKV + MLA_KV_LORA]) * kvnw_ref[...]
    ct, st = ct_ref[...], st_ref[...]
    ct8 = jnp.concatenate([ct] * MLA_HEADS, axis=1)
    st8 = jnp.concatenate([st] * MLA_HEADS, axis=1)
    q_cat = (_dot(qn, wa_ref[...]) * ct8 + _dot(qn, wb_ref[...]) * st8) * MLA_SCALE
    kpe = z[:, Z_KR:Z_KR + HEAD_SLAB] * ct + z[:, Z_KRR:Z_KRR + HEAD_SLAB] * st
    return q_cat, lat, kpe


def _mla_prep_prompt_body(z_ref, qnw_ref, kvnw_ref, wa_ref, wb_ref, ct_ref, st_ref, wka_ref, wv_ref,
                          q_ref, k_ref, v_ref, lat_ref, kpe_ref):
    q_cat, lat, kpe = _mla_common(z_ref, qnw_ref, kvnw_ref, wa_ref, wb_ref, ct_ref, st_ref)
    q_ref[...] = q_cat.astype(BF16)
    lat_ref[...] = lat
    kpe_ref[...] = kpe
    lb = lat.astype(BF16)
    k_ref[...] = (_dot(lb, wka_ref[...]) + jnp.concatenate([kpe] * MLA_HEADS, axis=1)).astype(BF16)
    v_ref[...] = _dot(lb, wv_ref[...]).astype(BF16)


def _mla_prep_sample_body(z_ref, qnw_ref, kvnw_ref, wa_ref, wb_ref, ct_ref, st_ref, wukt_ref,
                          q_ref, qlat_ref, lat_ref, kpe_ref):
    q_cat, lat, kpe = _mla_common(z_ref, qnw_ref, kvnw_ref, wa_ref, wb_ref, ct_ref, st_ref)
    qb = q_cat.astype(BF16)
    q_ref[...] = qb
    lat_ref[...] = lat
    kpe_ref[...] = kpe
    for h in range(MLA_HEADS):
        qlat_ref[:, h * MLA_KV_LORA:(h + 1) * MLA_KV_LORA] = _dot(
            qb[:, h * HEAD_SLAB:(h + 1) * HEAD_SLAB], wukt_ref[h]).astype(BF16)


def _mla_prep(grp, z, q_norm_w, kv_norm_w, mw, ct, st, sample):
    wa, wb, wka, wv, wukt = mw
    tm = grp.tile(512)
    n_tiles = grp.n_tok // tm
    hw = MLA_HEADS * HEAD_SLAB
    full = lambda a: pl.BlockSpec(a.shape, lambda i: (0,) * a.ndim)
    row = lambda w: pl.BlockSpec((tm, w), lambda i: (i, 0))
    if grp.per_token:
        tab = pl.BlockSpec((tm, HEAD_SLAB), lambda i: (i, 0))
    else:
        per = grp.time // tm
        tab = pl.BlockSpec((tm, HEAD_SLAB), lambda i: (i % per, 0))
    qnw = q_norm_w.reshape(1, -1)
    kvnw = kv_norm_w.reshape(1, -1)
    common_in = [pl.BlockSpec((tm, 1024), lambda i: (i, 0)), full(qnw), full(kvnw), full(wa), full(wb), tab, tab]
    n = grp.n_tok
    if sample:
        return pl.pallas_call(
            _mla_prep_sample_body, grid=(n_tiles,),
            in_specs=common_in + [full(wukt)],
            out_specs=[row(hw), row(MLA_HEADS * MLA_KV_LORA), row(MLA_KV_LORA), row(HEAD_SLAB)],
            out_shape=[jax.ShapeDtypeStruct((n, hw), BF16),
                       jax.ShapeDtypeStruct((n, MLA_HEADS * MLA_KV_LORA), BF16),
                       jax.ShapeDtypeStruct((n, MLA_KV_LORA), F32),
                       jax.ShapeDtypeStruct((n, HEAD_SLAB), F32)],
            compiler_params=_cparams(("parallel",)), name="mla_prep_sample",
        )(z, qnw, kvnw, wa, wb, ct, st, wukt)
    return pl.pallas_call(
        _mla_prep_prompt_body, grid=(n_tiles,),
        in_specs=common_in + [full(wka), full(wv)],
        out_specs=[row(hw), row(hw), row(MLA_HEADS * MLA_V), row(MLA_KV_LORA), row(HEAD_SLAB)],
        out_shape=[jax.ShapeDtypeStruct((n, hw), BF16),
                   jax.ShapeDtypeStruct((n, hw), BF16),
                   jax.ShapeDtypeStruct((n, MLA_HEADS * MLA_V), BF16),
                   jax.ShapeDtypeStruct((n, MLA_KV_LORA), F32),
                   jax.ShapeDtypeStruct((n, HEAD_SLAB), F32)],
        compiler_params=_cparams(("parallel",)), name="mla_prep_prompt",
    )(z, qnw, kvnw, wa, wb, ct, st, wka, wv)


def _flash_body(q_ref, k_ref, v_ref, o_ref, m_scr, l_scr, acc_scr, *, tq):
    qi, ki = pl.program_id(1), pl.program_id(2)

    @pl.when(ki == 0)
    def _():
        m_scr[...] = jnp.full(m_scr.shape, NEG_INF, F32)
        l_scr[...] = jnp.zeros(l_scr.shape, F32)
        acc_scr[...] = jnp.zeros(acc_scr.shape, F32)

    low = lax.broadcasted_iota(jnp.int32, (tq, LANE), 1) < MLA_V

    def step(masked):
        if masked:
            keep = (lax.broadcasted_iota(jnp.int32, (tq, tq), 1)
                    <= lax.broadcasted_iota(jnp.int32, (tq, tq), 0))
        for hp in range(MLA_HEADS // 2):
            pv, al = [], []
            for e in range(2):
                h = 2 * hp + e
                s = _dot_nt(q_ref[0, :, h * HEAD_SLAB:(h + 1) * HEAD_SLAB],
                            k_ref[0, :, h * HEAD_SLAB:(h + 1) * HEAD_SLAB])
                if masked:
                    s = jnp.where(keep, s, NEG_INF)
                m_prev = m_scr[h]
                m_new = jnp.maximum(m_prev, jnp.max(s, axis=-1, keepdims=True))
                alpha = jnp.exp(m_prev - m_new)
                p = jnp.exp(s - jnp.concatenate([m_new] * (tq // LANE), axis=1))
                l_scr[h] = alpha * l_scr[h] + jnp.sum(p, axis=-1, keepdims=True)
                m_scr[h] = m_new
                pv.append(_dot(p.astype(BF16), v_ref[0, :, hp * LANE:(hp + 1) * LANE]))
                al.append(alpha)
            sl = slice(hp * LANE, (hp + 1) * LANE)
            acc_scr[:, sl] = jnp.where(low, al[0], al[1]) * acc_scr[:, sl] + jnp.where(low, pv[0], pv[1])

    @pl.when(ki < qi)
    def _():
        step(False)

    @pl.when(ki == qi)
    def _():
        step(True)

    @pl.when(ki == pl.num_programs(2) - 1)
    def _():
        for hp in range(MLA_HEADS // 2):
            sl = slice(hp * LANE, (hp + 1) * LANE)
            o_ref[0, :, sl] = acc_scr[:, sl] / jnp.where(low, l_scr[2 * hp], l_scr[2 * hp + 1])


def _flash(q, k, v, batch, seq, tq):
    hw = MLA_HEADS * HEAD_SLAB
    vw = MLA_HEADS * MLA_V
    nq = seq // tq
    q3, k3, v3 = q.reshape(batch, seq, hw), k.reshape(batch, seq, hw), v.reshape(batch, seq, vw)
    out = pl.pallas_call(
        functools.partial(_flash_body, tq=tq),
        grid=(batch, nq, nq),
        in_specs=[pl.BlockSpec((1, tq, hw), lambda b, i, j: (b, i, 0)),
                  pl.BlockSpec((1, tq, hw), lambda b, i, j: (b, jnp.minimum(i, j), 0)),
                  pl.BlockSpec((1, tq, vw), lambda b, i, j: (b, jnp.minimum(i, j), 0))],
        out_specs=pl.BlockSpec((1, tq, vw), lambda b, i, j: (b, i, 0)),
        out_shape=jax.ShapeDtypeStruct((batch, seq, vw), F32),
        scratch_shapes=[pltpu.VMEM((MLA_HEADS, tq, LANE), F32), pltpu.VMEM((MLA_HEADS, tq, LANE), F32),
                        pltpu.VMEM((tq, vw), F32)],
        compiler_params=_cparams(("parallel", "parallel", "arbitrary")),
        name="flash",
    )(q3, k3, v3)
    return out.reshape(batch * seq, vw)


def _paged_body(pt_ref, q_ref, qlat_ref, nlat_ref, nkpe_ref, lat_hbm, kpe_hbm, o_ref,
                lat_buf, kpe_buf, sem, *, layer, n_pages, n_new):
    b = pl.program_id(0)
    nb = pl.num_programs(0)
    rows = q_ref.shape[1]

    def copies(bb, slot):
        out = []
        for j in range(n_pages):
            pg = pt_ref[bb, j]
            dst = pl.ds(j * PAGE_SIZE, PAGE_SIZE)
            out.append(pltpu.make_async_copy(lat_hbm.at[layer, pg], lat_buf.at[slot, dst], sem.at[0, slot]))
            out.append(pltpu.make_async_copy(kpe_hbm.at[layer, pg], kpe_buf.at[slot, :, dst], sem.at[1, slot]))
        return out

    @pl.when(b == 0)
    def _():
        for c in copies(0, 0):
            c.start()

    slot = b % 2

    @pl.when(b + 1 < nb)
    def _():
        for c in copies(b + 1, 1 - slot):
            c.start()

    for c in copies(b, slot):
        c.wait()

    qlat = qlat_ref[0]
    qpe = q_ref[0][:, MLA_NOPE:MLA_NOPE + MLA_ROPE]
    lat = lat_buf[slot].astype(BF16)
    kpe_t = kpe_buf[slot].astype(BF16)
    s_past = _dot_nt(qlat, lat) + _dot(qpe, kpe_t)
    nlat = nlat_ref[0].astype(BF16)
    nkpe = nkpe_ref[0][:, MLA_NOPE:MLA_NOPE + MLA_ROPE].astype(BF16)
    s_new = _dot_nt(qlat, nlat) + _dot_nt(qpe, nkpe)
    t_of_row = lax.broadcasted_iota(jnp.int32, (rows, n_new), 0) // MLA_HEADS
    s_new = jnp.where(lax.broadcasted_iota(jnp.int32, (rows, n_new), 1) <= t_of_row, s_new, NEG_INF)
    m = jnp.maximum(jnp.max(s_past, axis=-1, keepdims=True), jnp.max(s_new, axis=-1, keepdims=True))
    p_past = jnp.exp(s_past - m)
    p_new = jnp.exp(s_new - m)
    denom = jnp.sum(p_past, axis=-1, keepdims=True) + jnp.sum(p_new, axis=-1, keepdims=True)
    o = _dot(p_past.astype(BF16), lat) + _dot(p_new.astype(BF16), nlat)
    o_ref[0] = o / denom


def _paged(page_table, q_cat, q_lat, lat_new, kpe_new, cache_latent, cache_k_rope, layer, batch, n_new):
    rows = n_new * MLA_HEADS
    n_pages = page_table.shape[1]
    past = n_pages * PAGE_SIZE
    q3 = q_cat.reshape(batch, rows, HEAD_SLAB)
    ql3 = q_lat.reshape(batch, rows, MLA_KV_LORA)
    nl3 = lat_new.reshape(batch, n_new, MLA_KV_LORA)
    nk3 = kpe_new.reshape(batch, n_new, HEAD_SLAB)
    blk = lambda r, w: pl.BlockSpec((1, r, w), lambda b, pt: (b, 0, 0))
    out = pl.pallas_call(
        functools.partial(_paged_body, layer=layer, n_pages=n_pages, n_new=n_new),
        grid_spec=pltpu.PrefetchScalarGridSpec(
            num_scalar_prefetch=1, grid=(batch,),
            in_specs=[blk(rows, HEAD_SLAB), blk(rows, MLA_KV_LORA), blk(n_new, MLA_KV_LORA), blk(n_new, HEAD_SLAB),
                      pl.BlockSpec(memory_space=pl.ANY), pl.BlockSpec(memory_space=pl.ANY)],
            out_specs=blk(rows, MLA_KV_LORA),
            scratch_shapes=[pltpu.VMEM((2, past, MLA_KV_LORA), F32), pltpu.VMEM((2, MLA_ROPE, past), F32),
                            pltpu.SemaphoreType.DMA((2, 2))]),
        out_shape=jax.ShapeDtypeStruct((batch, rows, MLA_KV_LORA), F32),
        compiler_params=_cparams(("arbitrary",)),
        name="paged",
    )(page_table, q3, ql3, nl3, nk3, cache_latent, jnp.swapaxes(cache_k_rope, 2, 3))
    return out.reshape(batch * n_new, MLA_HEADS * MLA_KV_LORA)


def _head_proj_body(o_ref, w_ref, y_ref):
    for h in range(MLA_HEADS):
        y_ref[:, h * MLA_V:(h + 1) * MLA_V] = _dot(
            o_ref[:, h * MLA_KV_LORA:(h + 1) * MLA_KV_LORA].astype(BF16), w_ref[h])


def _head_proj(o_lat, w_uv):
    n = o_lat.shape[0]
    w = w_uv.reshape(MLA_KV_LORA, MLA_HEADS, MLA_V).transpose(1, 0, 2).astype(BF16)
    return pl.pallas_call(
        _head_proj_body, grid=(1,),
        in_specs=[pl.BlockSpec(o_lat.shape, lambda i: (0, 0)), pl.BlockSpec(w.shape, lambda i: (0, 0, 0))],
        out_specs=pl.BlockSpec((n, MLA_HEADS * MLA_V), lambda i: (0, 0)),
        out_shape=jax.ShapeDtypeStruct((n, MLA_HEADS * MLA_V), F32),
        compiler_params=_cparams(("arbitrary",)), name="head_proj",
    )(o_lat, w)


N_REC_HEADS = 4
REC_WIDTH = N_REC_HEADS * HEAD_SLAB


def _split3(x):
    a = x.astype(BF16)
    r = x - a.astype(F32)
    b = r.astype(BF16)
    c = (r - b.astype(F32)).astype(BF16)
    return a, b, c


def _scan_body(*refs, gla, chunk, sub, valid, has_s0, k_dim):
    if gla:
        q_ref, k_ref, v_ref, glow_ref, wg_ref, bg_ref, tri_ref = refs[:7]
        rest = refs[7:]
    else:
        q_ref, k_ref, v_ref, la_ref, l1_ref, tri_ref = refs[:6]
        rest = refs[6:]
    if has_s0:
        s0_ref, o_ref, sfin_ref, st_scr = rest
    else:
        o_ref, sfin_ref, st_scr = rest
    ci = pl.program_id(1)

    @pl.when(ci == 0)
    def _():
        if has_s0:
            for h in range(N_REC_HEADS):
                s0 = s0_ref[0, h]
                if k_dim < HEAD_SLAB:
                    s0 = jnp.concatenate([s0, jnp.zeros((HEAD_SLAB - k_dim, s0.shape[1]), F32)], axis=0)
                st_scr[h] = s0.T
        else:
            st_scr[...] = jnp.zeros(st_scr.shape, F32)

    v = v_ref[...]
    if gla:
        q = q_ref[...] * (GLA_DK ** -0.5)
        k = k_ref[...]
        g = _log_sigmoid(_dot(glow_ref[...].astype(BF16), wg_ref[...]) + bg_ref[...]) * (1.0 / GLA_GATE_NORMALIZER)
    else:
        xq = q_ref[...]
        q = xq * jax.nn.sigmoid(xq) * (HG_DK ** -0.5)
        a = la_ref[...]
        bb = l1_ref[...] + _log_sigmoid(k_ref[...])
        g = jnp.maximum(a, bb) + jnp.log1p(jnp.exp(-jnp.abs(a - bb)))
        k = 1.0 - jnp.exp(g)
    if valid < chunk:
        live = lax.broadcasted_iota(jnp.int32, (chunk, 1), 0) < valid
        g = jnp.where(live, g, 0.0)
        k = jnp.where(live, k, 0.0)

    tri = tri_ref[...]
    g1, g2, g3 = _split3(g)
    b = _dot(tri, g1) + _dot(tri, g2) + _dot(tri, g3)
    b_end = b[chunk - 1:chunk]
    qe = (q * jnp.exp(b)).astype(BF16)
    kd_f = k * jnp.exp(b_end - b)
    kd = kd_f.astype(BF16)
    kd_lo = (kd_f - kd.astype(F32)).astype(BF16)
    vb = v.astype(BF16)
    v_lo = (v - vb.astype(F32)).astype(BF16)
    hs = lambda h: slice(h * HEAD_SLAB, (h + 1) * HEAD_SLAB)

    o_inter = jnp.concatenate(
        [_dot_nt(qe[:, hs(h)], st_scr[h].astype(BF16)) for h in range(N_REC_HEADS)], axis=1)

    row_in_tile = lax.broadcasted_iota(jnp.int32, (SUBLANES, 1), 0)
    blocks = []
    for i in range(chunk // sub):
        lo = i * sub
        if lo >= valid:
            blocks.append(o_inter[lo:lo + sub])
            continue
        bi, qi_, ki_, vi = b[lo:lo + sub], q[lo:lo + sub], k[lo:lo + sub], v[lo:lo + sub]
        blk = o_inter[lo:lo + sub]
        if i > 0:
            r = b[lo - 1:lo]
            qt = (qi_ * jnp.exp(bi - r)).astype(BF16)
            kt = (k[:lo] * jnp.exp(r - b[:lo])).astype(BF16)
            off = []
            for h in range(N_REC_HEADS):
                att = _dot_nt(qt[:, hs(h)], kt[:, hs(h)])
                off.append(_dot(att.astype(BF16), vb[:lo, hs(h)]))
            blk = blk + jnp.concatenate(off, axis=1)
        live_rows = min(sub, valid - lo)
        tiles = []
        for r0 in range(0, sub, SUBLANES):
            acc = jnp.zeros((SUBLANES, REC_WIDTH), F32)
            if r0 < live_rows:
                bt, qt8 = bi[r0:r0 + SUBLANES], qi_[r0:r0 + SUBLANES]
                rows8 = row_in_tile + r0
                for s in range(min(live_rows, r0 + SUBLANES)):
                    e = jnp.exp(jnp.minimum(bt - bi[s:s + 1], 0.0))
                    w = qt8 * (ki_[s:s + 1] * e)
                    parts = []
                    for h in range(N_REC_HEADS):
                        a_ts = jnp.sum(w[:, hs(h)], axis=-1, keepdims=True)
                        if s > r0:
                            a_ts = jnp.where(rows8 >= s, a_ts, 0.0)
                        parts.append(a_ts * vi[s:s + 1, hs(h)])
                    acc = acc + jnp.concatenate(parts, axis=1)
            tiles.append(acc)
        blocks.append(blk + jnp.concatenate(tiles, axis=0))
    o_ref[...] = jnp.concatenate(blocks, axis=0)

    decay = jnp.exp(b_end)
    for h in range(N_REC_HEADS):
        upd = (_dot_tn(vb[:, hs(h)], kd[:, hs(h)]) + _dot_tn(vb[:, hs(h)], kd_lo[:, hs(h)])
               + _dot_tn(v_lo[:, hs(h)], kd[:, hs(h)]))
        st_scr[h] = st_scr[h] * decay[:, hs(h)] + upd

    @pl.when(ci == pl.num_programs(1) - 1)
    def _():
        for h in range(N_REC_HEADS):
            sfin_ref[0, h] = st_scr[h].T[:k_dim]


def _scan(z, batch, time, chunk, sub, valid, gla, extra, s0):
    k_dim = GLA_DK if gla else HG_DK
    nck = time // chunk
    cb = lambda col: pl.BlockSpec((chunk, REC_WIDTH), lambda b, c: (b * nck + c, col // REC_WIDTH))
    full = lambda a: pl.BlockSpec(a.shape, lambda b, c: (0,) * a.ndim)
    tri = jnp.tril(jnp.ones((chunk, chunk), F32)).astype(BF16)
    if gla:
        wg, bg = extra
        ins = [z, z, z, z, wg, bg, tri]
        specs = [cb(Z_GQ), cb(Z_GK), cb(Z_GV),
                 pl.BlockSpec((chunk, LANE), lambda b, c: (b * nck + c, Z_GLOW // LANE)), full(wg), full(bg), full(tri)]
    else:
        la, l1 = extra
        ins = [z, z, z, la, l1, tri]
        specs = [cb(Z_HQ), cb(Z_HF), cb(Z_HI), full(la), full(l1), full(tri)]
    if s0 is not None:
        ins.append(s0)
        specs.append(pl.BlockSpec((1, N_REC_HEADS, k_dim, HEAD_SLAB), lambda b, c: (b, 0, 0, 0)))
    return pl.pallas_call(
        functools.partial(_scan_body, gla=gla, chunk=chunk, sub=sub, valid=valid, has_s0=s0 is not None,
                          k_dim=k_dim),
        grid=(batch, nck),
        in_specs=specs,
        out_specs=[pl.BlockSpec((chunk, REC_WIDTH), lambda b, c: (b * nck + c, 0)),
                   pl.BlockSpec((1, N_REC_HEADS, k_dim, HEAD_SLAB), lambda b, c: (b, 0, 0, 0))],
        out_shape=[jax.ShapeDtypeStruct((batch * time, REC_WIDTH), F32),
                   jax.ShapeDtypeStruct((batch, N_REC_HEADS, k_dim, HEAD_SLAB), F32)],
        scratch_shapes=[pltpu.VMEM((N_REC_HEADS, HEAD_SLAB, HEAD_SLAB), F32)],
        compiler_params=_cparams(("parallel", "arbitrary")),
        name="scan_gla" if gla else "scan_hgrn",
    )(*ins)


def _merge_body(x_ref, ym_ref, oh_ref, og_ref, hgate_ref, ggate_ref, br0_ref, br1_ref, br2_ref,
                g1_ref, sc2_ref, sh2_ref, hgw_ref, glw_ref, nfw_ref, wbm_ref, wbh_ref, wbg_ref, wout_ref,
                rwh_ref, rwl_ref, rb_ref, tri_ref, cnt0_ref, x1_ref, h2_ref, te_ref, tw_ref, rk_ref, cnt_ref,
                carry_scr):
    silu = lambda t: t * jax.nn.sigmoid(t)
    yh = _rms(oh_ref[...]) * hgw_ref[...] * silu(hgate_ref[...])
    og = og_ref[...]
    glw = glw_ref[...]
    yg = jnp.concatenate([_rms(og[:, h * GLA_DV:(h + 1) * GLA_DV]) * glw for h in range(GLA_HEADS)], axis=1)
    yg = yg * silu(ggate_ref[...])
    m = (jax.nn.sigmoid(br0_ref[...]) * _dot(ym_ref[...].astype(BF16), wbm_ref[...])
         + jax.nn.sigmoid(br1_ref[...]) * _dot(yh.astype(BF16), wbh_ref[...])
         + jax.nn.sigmoid(br2_ref[...]) * _dot(yg.astype(BF16), wbg_ref[...]))
    x1 = x_ref[...] + g1_ref[0] * _dot(m.astype(BF16), wout_ref[...])
    x1_ref[...] = x1
    h2 = _rms(x1) * nfw_ref[...] * (1.0 + sc2_ref[0]) + sh2_ref[0]
    h2_ref[...] = h2
    hh = h2.astype(BF16)
    hl = (h2 - hh.astype(F32)).astype(BF16)
    rwh = rwh_ref[...]
    logits = _dot(hh, rwh) + _dot(hh, rwl_ref[...]) + _dot(hl, rwh) + rb_ref[...]

    lane = lax.broadcasted_iota(jnp.int32, logits.shape, 1).astype(F32)
    work = jnp.where(lane < N_EXPERTS, logits, NEG_INF)
    vals, idxs, hots = [], [], []
    for _ in range(TOP_K):
        mx = jnp.max(work, axis=1, keepdims=True)
        idx = jnp.min(jnp.where(work == mx, lane, float(LANE)), axis=1, keepdims=True)
        hot = lane == idx
        work = jnp.where(hot, NEG_INF, work)
        vals.append(mx)
        idxs.append(idx)
        hots.append(hot)
    ex = [jnp.exp(v - vals[0]) for v in vals]
    den = ex[0] + ex[1] + ex[2] + ex[3]

    @pl.when(pl.program_id(0) == 0)
    def _():
        carry_scr[...] = cnt0_ref[...]

    cnt = sum(h.astype(F32) for h in hots)
    before = _dot(tri_ref[...], cnt.astype(BF16)) + carry_scr[...]
    te = jnp.zeros(logits.shape, F32)
    tw = jnp.zeros(logits.shape, F32)
    rk = jnp.zeros(logits.shape, F32)
    for k in range(TOP_K):
        sel = lane == float(k)
        te = jnp.where(sel, idxs[k], te)
        tw = jnp.where(sel, ex[k] / den, tw)
        rk = jnp.where(sel, jnp.sum(jnp.where(hots[k], before, 0.0), axis=1, keepdims=True), rk)
    te_ref[...] = te.astype(jnp.int32)
    tw_ref[...] = tw
    rk_ref[...] = rk.astype(jnp.int32)
    carry_scr[...] = carry_scr[...] + jnp.sum(cnt, axis=0, keepdims=True)
    cnt_ref[...] = carry_scr[...]


def _merge(grp, x, y_mla, o_hg, o_gla, z, g1, sc2, sh2, lw, cnt0):
    tm = grp.tile(256)
    tri = jnp.tril(jnp.ones((tm, tm), F32), k=-1).astype(BF16)
    row = lambda w: pl.BlockSpec((tm, w), lambda i: (i, 0))
    zc = lambda col, w: pl.BlockSpec((tm, w), lambda i: (i, col // w))
    full = lambda a: pl.BlockSpec(a.shape, lambda i: (0,) * a.ndim)
    g1_a, g1_s = grp.mod(g1, tm)
    sc_a, sc_s = grp.mod(sc2, tm)
    sh_a, sh_s = grp.mod(sh2, tm)
    ws = [lw["hg_norm_w"], lw["gla_norm_w"], lw["norm_ffn_w"], lw["w_br_mla"], lw["w_br_hg"], lw["w_br_gla"],
          lw["w_out"], lw["rw_hi"], lw["rw_lo"], lw["rb"], tri, cnt0]
    n = grp.n_tok
    one = pl.BlockSpec((1, LANE), lambda i: (0, 0))
    return pl.pallas_call(
        _merge_body, grid=(grp.n_tok // tm,),
        in_specs=[row(D_MODEL), row(512), row(512), row(512), zc(Z_HGATE, 512), zc(Z_GGATE, 512),
                  zc(Z_BR, 1024), zc(Z_BR + 1024, 1024), zc(Z_BR + 2048, 1024),
                  g1_s, sc_s, sh_s] + [full(w) for w in ws],
        out_specs=[row(D_MODEL), row(D_MODEL), row(LANE), row(LANE), row(LANE), one],
        out_shape=[jax.ShapeDtypeStruct((n, D_MODEL), F32), jax.ShapeDtypeStruct((n, D_MODEL), F32),
                   jax.ShapeDtypeStruct((n, LANE), jnp.int32), jax.ShapeDtypeStruct((n, LANE), F32),
                   jax.ShapeDtypeStruct((n, LANE), jnp.int32), jax.ShapeDtypeStruct((1, LANE), F32)],
        scratch_shapes=[pltpu.VMEM((1, LANE), F32)],
        compiler_params=_cparams(("arbitrary",)), name="merge",
    )(x, y_mla, o_hg, o_gla, z, z, z, z, z, g1_a, sc_a, sh_a, *ws)


def _experts_body(be_ref, nu_ref, x_ref, wgu_ref, bgu_ref, wd_ref, bd_ref, y_ref, wgu_scr, wd_scr):
    i = pl.program_id(0)
    e = be_ref[i]
    prev = be_ref[jnp.maximum(i - 1, 0)]

    @pl.when(jnp.logical_or(i == 0, e != prev))
    def _():
        wgu_scr[...] = wgu_ref[0].astype(BF16)
        wd_scr[...] = wd_ref[0].astype(BF16)

    @pl.when(i < nu_ref[0])
    def _():
        gu = _dot(x_ref[...].astype(BF16), wgu_scr[...]) + bgu_ref[0]
        gate = jnp.minimum(gu[:, :D_EXPERT], SWIGLU_LIMIT)
        up = jnp.clip(gu[:, D_EXPERT:], -SWIGLU_LIMIT, SWIGLU_LIMIT)
        act = (up + 1.0) * gate * jax.nn.sigmoid(SWIGLU_ALPHA * gate)
        y_ref[...] = _dot(act.astype(BF16), wd_scr[...]) + bd_ref[0]

    @pl.when(i >= nu_ref[0])
    def _():
        y_ref[...] = jnp.zeros(y_ref.shape, F32)


def _experts(block_e, n_used, xb, w_gate_up, b_gate_up, w_down, b_down, layer):
    n_slots = xb.shape[0]
    tm = MOE_TM
    ne = N_EXPERTS
    return pl.pallas_call(
        _experts_body,
        grid_spec=pltpu.PrefetchScalarGridSpec(
            num_scalar_prefetch=2, grid=(n_slots // tm,),
            in_specs=[pl.BlockSpec((tm, D_MODEL), lambda i, be, nu: (i, 0)),
                      pl.BlockSpec((1, D_MODEL, 2 * D_EXPERT), lambda i, be, nu: (layer * ne + be[i], 0, 0)),
                      pl.BlockSpec((1, 1, 2 * D_EXPERT), lambda i, be, nu: (layer * ne + be[i], 0, 0)),
                      pl.BlockSpec((1, D_EXPERT, D_MODEL), lambda i, be, nu: (layer * ne + be[i], 0, 0)),
                      pl.BlockSpec((1, 1, D_MODEL), lambda i, be, nu: (layer * ne + be[i], 0, 0))],
            out_specs=pl.BlockSpec((tm, D_MODEL), lambda i, be, nu: (i, 0)),
            scratch_shapes=[pltpu.VMEM((D_MODEL, 2 * D_EXPERT), BF16), pltpu.VMEM((D_EXPERT, D_MODEL), BF16)]),
        out_shape=jax.ShapeDtypeStruct((n_slots, D_MODEL), F32),
        compiler_params=_cparams(("arbitrary",)),
        name="experts",
    )(block_e, n_used, xb,
      w_gate_up.reshape(DEPTH * ne, D_MODEL, 2 * D_EXPERT), b_gate_up.reshape(DEPTH * ne, 1, 2 * D_EXPERT),
      w_down.reshape(DEPTH * ne, D_EXPERT, D_MODEL), b_down.reshape(DEPTH * ne, 1, D_MODEL))


def _route(top_e, rank, counts):
    n = top_e.shape[0]
    nk = n * TOP_K
    tm = MOE_TM
    experts = jnp.arange(N_EXPERTS, dtype=jnp.int32)
    padded = (counts + tm - 1) // tm * tm
    pad_end = jnp.cumsum(padded)
    pad_start = pad_end - padded
    start = jnp.cumsum(counts) - counts
    dest = jnp.sum(jnp.where(top_e[..., None] == experts, pad_start, 0), axis=-1) + rank
    flat_tok = jnp.arange(nk, dtype=jnp.int32) // TOP_K
    _, stok = lax.sort((dest.reshape(-1), flat_tok), num_keys=1)
    n_blocks = (nk + N_EXPERTS * (tm - 1) + tm - 1) // tm
    block_lo = jnp.arange(n_blocks, dtype=jnp.int32) * tm
    block_e = jnp.minimum(jnp.sum((pad_end[None, :] <= block_lo[:, None]).astype(jnp.int32), axis=1),
                          N_EXPERTS - 1)
    off = (block_lo - pad_start[block_e])[:, None] + jnp.arange(tm, dtype=jnp.int32)[None, :]
    live = off < counts[block_e][:, None]
    src = jnp.clip(start[block_e][:, None] + off, 0, nk - 1)
    slot_tok = jnp.where(live, stok[src.reshape(-1)].reshape(n_blocks, tm), 0).reshape(-1)
    n_used = (pad_end[-1] // tm).astype(jnp.int32).reshape(1)
    return slot_tok, block_e.astype(jnp.int32), n_used, dest


def _final_norm_body(x_ref, w_ref, o_ref):
    o_ref[...] = _rms(x_ref[...]) * w_ref[...]


def _final_norm(x, w, tm):
    n = x.shape[0]
    return pl.pallas_call(
        _final_norm_body, grid=(n // tm,),
        in_specs=[pl.BlockSpec((tm, D_MODEL), lambda i: (i, 0)), pl.BlockSpec((1, D_MODEL), lambda i: (0, 0))],
        out_specs=pl.BlockSpec((tm, D_MODEL), lambda i: (i, 0)),
        out_shape=jax.ShapeDtypeStruct((n, D_MODEL), F32),
        compiler_params=_cparams(("parallel",)), name="final_norm",
    )(x, w.reshape(1, D_MODEL))


def kernel(x_prompt, x_sample, cache_latent, cache_k_rope, state_hgrn, state_gla, page_table, c_prompt, c_sample, norm_mix_w, norm_ffn_w, final_norm_w, w_ada, b_ada, w_in, mla_q_norm_w, mla_w_uq, mla_kv_norm_w, mla_w_uk, mla_w_uv, hgrn_lower_bounds, hgrn_norm_w, gla_w_gate, gla_b_gate, gla_norm_w, w_branch_mla, w_branch_hgrn, w_branch_gla, w_out, router_w, router_b, w_gate_up, b_gate_up, w_down, b_down):
    bp, seq, _ = x_prompt.shape
    bs, tnew, _ = x_sample.shape
    n_p, n_s = bp * seq, bs * tnew
    past_len = page_table.shape[1] * cache_latent.shape[2]
    gp = _Group(bp, seq, per_token=False)
    gs = _Group(bs, tnew, per_token=True)
    s_pad = SCAN_SUB

    rows = bp + bs
    rows_pad = -(-rows // 8) * 8
    c_all = jnp.pad(jnp.concatenate([c_prompt, c_sample], axis=0), ((0, rows_pad - rows), (0, 0)))
    mod = _adaln(c_all, w_ada, b_ada)

    lbs = jax.nn.softmax(hgrn_lower_bounds.astype(F32), axis=0)
    lbs = jnp.cumsum(lbs, axis=0) - lbs[0]
    log_lb = jnp.log(jnp.maximum(lbs, TINY))
    log_1m_lb = jnp.log1p(-lbs)

    ct_p, st_p = _rope_tables(jnp.arange(seq, dtype=jnp.int32))
    pos_s = past_len + jnp.arange(tnew, dtype=jnp.int32)
    ct_s, st_s = _rope_tables(jnp.tile(pos_s, bs))

    xp = x_prompt.reshape(n_p, D_MODEL)
    xs = x_sample.reshape(n_s, D_MODEL)
    lat_p, kpe_p, hg_p, gla_p, lat_s, kpe_s, hg_s, gla_s = [], [], [], [], [], [], [], []

    for l in range(DEPTH):
        w_pad = _pad_w_in(w_in[l])
        mw = _mla_weights(mla_w_uq[l], mla_w_uk[l], mla_w_uv[l])
        rw = jnp.pad(router_w[l], ((0, 0), (0, LANE - N_EXPERTS)))
        rw_hi = rw.astype(BF16)
        lw = dict(hg_norm_w=hgrn_norm_w[l].reshape(1, -1), gla_norm_w=gla_norm_w[l].reshape(1, -1),
                  norm_ffn_w=norm_ffn_w[l].reshape(1, -1), w_br_mla=w_branch_mla[l].astype(BF16),
                  w_br_hg=w_branch_hgrn[l].astype(BF16), w_br_gla=w_branch_gla[l].astype(BF16),
                  w_out=w_out[l].astype(BF16), rw_hi=rw_hi, rw_lo=(rw - rw_hi.astype(F32)).astype(BF16),
                  rb=jnp.pad(router_b[l], (0, LANE - N_EXPERTS)).reshape(1, LANE))
        wg = jnp.pad(gla_w_gate[l].reshape(GLA_GATE_RANK, GLA_HEADS, GLA_DK),
                     ((0, LANE - GLA_GATE_RANK), (0, 0), (0, HEAD_SLAB - GLA_DK))).reshape(LANE, REC_WIDTH).astype(BF16)
        bg = jnp.pad(gla_b_gate[l].reshape(GLA_HEADS, GLA_DK), ((0, 0), (0, HEAD_SLAB - GLA_DK))).reshape(1, REC_WIDTH)
        hg_extra = (log_lb[l].reshape(1, -1), log_1m_lb[l].reshape(1, -1))
        split6 = lambda m: jnp.split(m, 6, axis=-1)
        sh1p, sc1p, g1p, sh2p, sc2p, g2p = split6(mod[l, :bp])
        sh1s, sc1s, g1s, sh2s, sc2s, g2s = split6(mod[l, bp:rows])

        zp = _in_proj(gp, xp, norm_mix_w[l], sc1p, sh1p, w_pad)
        q, k, v, latp, kpep = _mla_prep(gp, zp, mla_q_norm_w[l], mla_kv_norm_w[l], mw, ct_p, st_p, sample=False)
        y_mla_p = _flash(q, k, v, bp, seq, tq=512)
        o_hg_p, s_hg_p = _scan(zp, bp, seq, SCAN_CHUNK, SCAN_SUB, SCAN_CHUNK, False, hg_extra, None)
        o_gl_p, s_gl_p = _scan(zp, bp, seq, SCAN_CHUNK, SCAN_SUB, SCAN_CHUNK, True, (wg, bg), None)
        x1p, h2p, te_p, tw_p, rk_p, cnt_p = _merge(gp, xp, y_mla_p, o_hg_p, o_gl_p, zp, g1p, sc2p, sh2p, lw,
                                                   jnp.zeros((1, LANE), F32))

        zs = _in_proj(gs, xs, norm_mix_w[l], sc1s, sh1s, w_pad)
        qs, qlat, lats, kpes = _mla_prep(gs, zs, mla_q_norm_w[l], mla_kv_norm_w[l], mw, ct_s, st_s, sample=True)
        o_lat = _paged(page_table, qs, qlat, lats, kpes, cache_latent, cache_k_rope, l, bs, tnew)
        y_mla_s = _head_proj(o_lat, mla_w_uv[l])
        zs_pad = jnp.pad(zs.reshape(bs, tnew, Z_WIDTH), ((0, 0), (0, s_pad - tnew), (0, 0))).reshape(bs * s_pad, Z_WIDTH)
        unpad = lambda o: o.reshape(bs, s_pad, -1)[:, :tnew].reshape(n_s, -1)
        o_hg_s, s_hg_s = _scan(zs_pad, bs, s_pad, s_pad, SCAN_SUB, tnew, False, hg_extra, state_hgrn[l])
        o_gl_s, s_gl_s = _scan(zs_pad, bs, s_pad, s_pad, SCAN_SUB, tnew, True, (wg, bg), state_gla[l])
        x1s, h2s, te_s, tw_s, rk_s, cnt_all = _merge(gs, xs, y_mla_s, unpad(o_hg_s), unpad(o_gl_s), zs, g1s, sc2s,
                                                     sh2s, lw, cnt_p)

        h2 = jnp.concatenate([h2p, h2s], axis=0)
        both = lambda a, b: jnp.concatenate([a, b], axis=0)[:, :TOP_K]
        top_w = both(tw_p, tw_s)
        slot_tok, block_e, n_used, dest = _route(both(te_p, te_s), both(rk_p, rk_s),
                                                 cnt_all[0, :N_EXPERTS].astype(jnp.int32))
        yb = _experts(block_e, n_used, h2[slot_tok], w_gate_up, b_gate_up, w_down, b_down, l)
        y = sum(top_w[:, k:k + 1] * yb[dest[:, k]] for k in range(TOP_K))
        xp = x1p + jnp.repeat(g2p, seq, axis=0) * y[:n_p]
        xs = x1s + jnp.repeat(g2s, tnew, axis=0) * y[n_p:]

        sl = slice(MLA_NOPE, MLA_NOPE + MLA_ROPE)
        lat_p.append(latp.reshape(bp, seq, -1)); kpe_p.append(kpep[:, sl].reshape(bp, seq, -1))
        hg_p.append(s_hg_p); gla_p.append(s_gl_p)
        lat_s.append(lats.reshape(bs, tnew, -1)); kpe_s.append(kpes[:, sl].reshape(bs, tnew, -1))
        hg_s.append(s_hg_s); gla_s.append(s_gl_s)

    y_prompt = _final_norm(xp, final_norm_w, 1024).reshape(bp, seq, D_MODEL)
    y_sample = _final_norm(xs, final_norm_w, n_s).reshape(bs, tnew, D_MODEL)
    st = jnp.stack
    return (y_prompt, y_sample, st(lat_p), st(kpe_p), st(hg_p), st(gla_p),
            st(lat_s), st(kpe_s), st(hg_s), st(gla_s))
```

```python
import functools

import numpy as np
import jax
import jax.numpy as jnp
from jax import lax
from jax.experimental import pallas as pl
from jax.experimental.pallas import tpu as pltpu

F32 = jnp.float32
BF16 = jnp.bfloat16

D_MODEL = 1024
DEPTH = 2
PAGE_SIZE = 128
MLA_HEADS = 8
MLA_NOPE = 64
MLA_ROPE = 32
MLA_V = 64
MLA_Q_LORA = 384
MLA_KV_LORA = 256
MLA_SCALE = (MLA_NOPE + MLA_ROPE) ** -0.5
ROPE_BASE = 10000.0
HG_HEADS = 4
HG_DK = 128
HG_DV = 128
GLA_HEADS = 4
GLA_DK = 64
GLA_DV = 128
GLA_GATE_RANK = 16
GLA_GATE_NORMALIZER = 16.0
N_EXPERTS = 32
TOP_K = 4
D_EXPERT = 1024
SWIGLU_LIMIT = 7.0
SWIGLU_ALPHA = 1.702
EPS = 1e-6
NEG_INF = -1e30
TINY = 1e-30
LOG2_E = 1.4426950408889634

IN_SPLITS = (MLA_Q_LORA, MLA_KV_LORA, MLA_ROPE, 512, 512, 512, 512, 256, 256, 512, 512, GLA_GATE_RANK,
             3 * D_MODEL)

LANE = 128
SUBLANES = 8
HEAD_SLAB = 128
VMEM_LIMIT = 56 * 1024 * 1024

Z_WIDTH = 8192
Z_CQ, Z_KR, Z_CKV, Z_KRR, Z_GLOW = 0, 384, 512, 768, 896
Z_HQ, Z_HF, Z_HI, Z_HGATE = 1024, 1536, 2048, 2560
Z_GQ, Z_GK, Z_GV, Z_GGATE, Z_BR = 3072, 3584, 4096, 4608, 5120

SCAN_CHUNK = 64
SCAN_SUB = 16
MOE_TM = 256
SAMPLE_ROWS_PER_STEP = 4


def _cparams(sem, vmem=VMEM_LIMIT):
    return pltpu.CompilerParams(dimension_semantics=sem, vmem_limit_bytes=vmem)


def _dot(a, b):
    return jnp.dot(a, b, preferred_element_type=F32)


def _dot_nt(a, b):
    return lax.dot_general(a, b, (((1,), (1,)), ((), ())), preferred_element_type=F32)


def _dot_tn(a, b):
    return lax.dot_general(a, b, (((0,), (0,)), ((), ())), preferred_element_type=F32)


def _rms(x):
    return x * lax.rsqrt(jnp.mean(x * x, axis=-1, keepdims=True) + EPS)


def _log_sigmoid(x):
    return jnp.minimum(x, 0.0) - jnp.log1p(jnp.exp(-jnp.abs(x)))


def _adaln_body(c_ref, w_ref, b_ref, o_ref):
    c = c_ref[...]
    a = (c * jax.nn.sigmoid(c)).astype(BF16)
    o_ref[0] = _dot(a, w_ref[0].astype(BF16)) + b_ref[0]


def _adaln(c_all, w_ada, b_ada):
    rows = c_all.shape[0]
    tn = 1536
    n_out = w_ada.shape[-1]
    return pl.pallas_call(
        _adaln_body,
        grid=(DEPTH, n_out // tn),
        in_specs=[pl.BlockSpec((rows, D_MODEL), lambda l, j: (0, 0)),
                  pl.BlockSpec((1, D_MODEL, tn), lambda l, j: (l, 0, j)),
                  pl.BlockSpec((1, 1, tn), lambda l, j: (l, 0, j))],
        out_specs=pl.BlockSpec((1, rows, tn), lambda l, j: (l, 0, j)),
        out_shape=jax.ShapeDtypeStruct((DEPTH, rows, n_out), F32),
        compiler_params=_cparams(("parallel", "parallel")),
        name="adaln",
    )(c_all, w_ada, b_ada.reshape(DEPTH, 1, n_out))


class _Group:
    def __init__(self, batch, time, per_token):
        self.batch, self.time, self.per_token = batch, time, per_token
        self.n_tok = batch * time

    def tile(self, tm):
        tm = min(tm, self.n_tok)
        assert (self.n_tok if self.per_token else self.time) % tm == 0
        return tm

    def mod(self, m, tm):
        if self.per_token:
            arr = jnp.repeat(m, self.time, axis=0)[None]
            return arr, pl.BlockSpec((1, tm, m.shape[-1]), lambda *g: (0, g[0], 0))
        arr = m[:, None, :]
        per = self.time // tm
        return arr, pl.BlockSpec((1, 1, m.shape[-1]), lambda *g: (g[0] // per, 0, 0))


def _in_proj_body(x_ref, nw_ref, sc_ref, sh_ref, w_ref, z_ref, h_scr):
    @pl.when(pl.program_id(1) == 0)
    def _():
        h = _rms(x_ref[...]) * nw_ref[...]
        h_scr[...] = (h * (1.0 + sc_ref[0]) + sh_ref[0]).astype(BF16)

    z_ref[...] = _dot(h_scr[...], w_ref[...])


def _in_proj(grp, x, norm_w, sc, sh, w_pad):
    tm, tn = grp.tile(1024), 1024
    sc_a, sc_s = grp.mod(sc, tm)
    sh_a, sh_s = grp.mod(sh, tm)
    return pl.pallas_call(
        _in_proj_body,
        grid=(grp.n_tok // tm, Z_WIDTH // tn),
        in_specs=[pl.BlockSpec((tm, D_MODEL), lambda i, j: (i, 0)),
                  pl.BlockSpec((1, D_MODEL), lambda i, j: (0, 0)),
                  sc_s, sh_s,
                  pl.BlockSpec((D_MODEL, tn), lambda i, j: (0, j))],
        out_specs=pl.BlockSpec((tm, tn), lambda i, j: (i, j)),
        out_shape=jax.ShapeDtypeStruct((grp.n_tok, Z_WIDTH), F32),
        scratch_shapes=[pltpu.VMEM((tm, D_MODEL), BF16)],
        compiler_params=_cparams(("parallel", "arbitrary")),
        name="in_proj",
    )(x, norm_w.reshape(1, D_MODEL), sc_a, sh_a, w_pad)


def _pad_w_in(w):
    idx = np.cumsum(IN_SPLITS)[:-1].tolist()
    cq, ckv, kr, hq, hf, hi, hgate, gq, gk, gv, ggate, glow, br = jnp.split(w, idx, axis=1)
    zeros = lambda n: jnp.zeros((w.shape[0], n), w.dtype)
    half = MLA_ROPE // 2
    kr_rot = jnp.concatenate([-kr[:, half:], kr[:, :half]], axis=1)
    slab = lambda a: jnp.concatenate([zeros(MLA_NOPE), a, zeros(HEAD_SLAB - MLA_NOPE - MLA_ROPE)], axis=1)
    pad_heads = lambda a: jnp.pad(a.reshape(-1, GLA_HEADS, GLA_DK),
                                  ((0, 0), (0, 0), (0, HEAD_SLAB - GLA_DK))).reshape(-1, GLA_HEADS * HEAD_SLAB)
    glow_slab = jnp.concatenate([glow, zeros(LANE - GLA_GATE_RANK)], axis=1)
    out = jnp.concatenate([cq, slab(kr), ckv, slab(kr_rot), glow_slab, hq, hf, hi, hgate,
                           pad_heads(gq), pad_heads(gk), gv, ggate, br], axis=1)
    assert out.shape[1] == Z_WIDTH
    return out.astype(BF16)


def _mla_weights(w_uq, w_uk, w_uv):
    hd = MLA_NOPE + MLA_ROPE
    half = MLA_ROPE // 2
    q = w_uq.reshape(MLA_Q_LORA, MLA_HEADS, hd)
    nope, pe = q[..., :MLA_NOPE], q[..., MLA_NOPE:]
    pe_rot = jnp.concatenate([-pe[..., half:], pe[..., :half]], axis=-1)
    z = lambda n: jnp.zeros((MLA_Q_LORA, MLA_HEADS, n), w_uq.dtype)
    wa = jnp.concatenate([nope, pe, z(HEAD_SLAB - hd)], axis=-1).reshape(MLA_Q_LORA, -1)
    wb = jnp.concatenate([z(MLA_NOPE), pe_rot, z(HEAD_SLAB - hd)], axis=-1).reshape(MLA_Q_LORA, -1)
    k = w_uk.reshape(MLA_KV_LORA, MLA_HEADS, MLA_NOPE)
    wka = jnp.pad(k, ((0, 0), (0, 0), (0, HEAD_SLAB - MLA_NOPE))).reshape(MLA_KV_LORA, -1)
    wukt = jnp.pad(k.transpose(1, 2, 0), ((0, 0), (0, HEAD_SLAB - MLA_NOPE), (0, 0)))
    return wa.astype(BF16), wb.astype(BF16), wka.astype(BF16), w_uv.astype(BF16), wukt.astype(BF16)


def _rope_tables(pos):
    half = MLA_ROPE // 2
    inv = ROPE_BASE ** (-jnp.arange(half, dtype=F32) / half)
    ang = pos.astype(F32)[:, None] * inv[None, :]
    cos, sin = jnp.cos(ang), jnp.sin(ang)
    n = pos.shape[0]
    tail = jnp.zeros((n, HEAD_SLAB - MLA_NOPE - MLA_ROPE), F32)
    ct = jnp.concatenate([jnp.ones((n, MLA_NOPE), F32), cos, cos, tail], axis=1)
    st = jnp.concatenate([jnp.zeros((n, MLA_NOPE), F32), sin, sin, tail], axis=1)
    return ct, st


def _mla_common(z_ref, qnw_ref, kvnw_ref, wa_ref, wb_ref, ct_ref, st_ref):
    z = z_ref[...]
    qn = (_rms(z[:, Z_CQ:Z_CQ + MLA_Q_LORA]) * qnw_ref[...]).astype(BF16)
    lat = _rms(z[:, Z_CKV:Z_CKV + MLA_KV_LORA]) * kvnw_ref[...]
    ct, st = ct_ref[...], st_ref[...]
    ct8 = jnp.concatenate([ct] * MLA_HEADS, axis=1)
    st8 = jnp.concatenate([st] * MLA_HEADS, axis=1)
    q_cat = (_dot(qn, wa_ref[...]) * ct8 + _dot(qn, wb_ref[...]) * st8) * MLA_SCALE
    kpe = z[:, Z_KR:Z_KR + HEAD_SLAB] * ct + z[:, Z_KRR:Z_KRR + HEAD_SLAB] * st
    return q_cat, lat, kpe


def _mla_prep_prompt_body(z_ref, qnw_ref, kvnw_ref, wa_ref, wb_ref, ct_ref, st_ref, wka_ref, wv_ref,
                          q_ref, k_ref, v_ref, lat_ref, kpe_ref):
    q_cat, lat, kpe = _mla_common(z_ref, qnw_ref, kvnw_ref, wa_ref, wb_ref, ct_ref, st_ref)
    q_ref[...] = q_cat.astype(BF16)
    lat_ref[...] = lat
    kpe_ref[...] = kpe
    lb = lat.astype(BF16)
    k_ref[...] = (_dot(lb, wka_ref[...]) + jnp.concatenate([kpe] * MLA_HEADS, axis=1)).astype(BF16)
    v_ref[...] = _dot(lb, wv_ref[...]).astype(BF16)


def _mla_prep_sample_body(z_ref, qnw_ref, kvnw_ref, wa_ref, wb_ref, ct_ref, st_ref, wukt_ref,
                          q_ref, qlat_ref, lat_ref, kpe_ref):
    q_cat, lat, kpe = _mla_common(z_ref, qnw_ref, kvnw_ref, wa_ref, wb_ref, ct_ref, st_ref)
    qb = q_cat.astype(BF16)
    q_ref[...] = qb
    lat_ref[...] = lat
    kpe_ref[...] = kpe
    for h in range(MLA_HEADS):
        qlat_ref[:, h * MLA_KV_LORA:(h + 1) * MLA_KV_LORA] = _dot(
            qb[:, h * HEAD_SLAB:(h + 1) * HEAD_SLAB], wukt_ref[h]).astype(BF16)


def _mla_prep(grp, z, q_norm_w, kv_norm_w, mw, ct, st, sample):
    wa, wb, wka, wv, wukt = mw
    tm = grp.tile(512)
    n_tiles = grp.n_tok // tm
    hw = MLA_HEADS * HEAD_SLAB
    full = lambda a: pl.BlockSpec(a.shape, lambda i: (0,) * a.ndim)
    row = lambda w: pl.BlockSpec((tm, w), lambda i: (i, 0))
    if grp.per_token:
        tab = pl.BlockSpec((tm, HEAD_SLAB), lambda i: (i, 0))
    else:
        per = grp.time // tm
        tab = pl.BlockSpec((tm, HEAD_SLAB), lambda i: (i % per, 0))
    qnw = q_norm_w.reshape(1, -1)
    kvnw = kv_norm_w.reshape(1, -1)
    common_in = [pl.BlockSpec((tm, 1024), lambda i: (i, 0)), full(qnw), full(kvnw), full(wa), full(wb), tab, tab]
    n = grp.n_tok
    if sample:
        return pl.pallas_call(
            _mla_prep_sample_body, grid=(n_tiles,),
            in_specs=common_in + [full(wukt)],
            out_specs=[row(hw), row(MLA_HEADS * MLA_KV_LORA), row(MLA_KV_LORA), row(HEAD_SLAB)],
            out_shape=[jax.ShapeDtypeStruct((n, hw), BF16),
                       jax.ShapeDtypeStruct((n, MLA_HEADS * MLA_KV_LORA), BF16),
                       jax.ShapeDtypeStruct((n, MLA_KV_LORA), F32),
                       jax.ShapeDtypeStruct((n, HEAD_SLAB), F32)],
            compiler_params=_cparams(("parallel",)), name="mla_prep_sample",
        )(z, qnw, kvnw, wa, wb, ct, st, wukt)
    return pl.pallas_call(
        _mla_prep_prompt_body, grid=(n_tiles,),
        in_specs=common_in + [full(wka), full(wv)],
        out_specs=[row(hw), row(hw), row(MLA_HEADS * MLA_V), row(MLA_KV_LORA), row(HEAD_SLAB)],
        out_shape=[jax.ShapeDtypeStruct((n, hw), BF16),
                   jax.ShapeDtypeStruct((n, hw), BF16),
                   jax.ShapeDtypeStruct((n, MLA_HEADS * MLA_V), BF16),
                   jax.ShapeDtypeStruct((n, MLA_KV_LORA), F32),
                   jax.ShapeDtypeStruct((n, HEAD_SLAB), F32)],
        compiler_params=_cparams(("parallel",)), name="mla_prep_prompt",
    )(z, qnw, kvnw, wa, wb, ct, st, wka, wv)


def _flash_body(q_ref, k_ref, v_ref, o_ref, m_scr, l_scr, acc_scr, *, tq):
    qi, ki = pl.program_id(1), pl.program_id(2)

    @pl.when(ki == 0)
    def _():
        m_scr[...] = jnp.full(m_scr.shape, NEG_INF, F32)
        l_scr[...] = jnp.zeros(l_scr.shape, F32)
        acc_scr[...] = jnp.zeros(acc_scr.shape, F32)

    low = lax.broadcasted_iota(jnp.int32, (tq, LANE), 1) < MLA_V

    def step(masked):
        if masked:
            keep = (lax.broadcasted_iota(jnp.int32, (tq, tq), 1)
                    <= lax.broadcasted_iota(jnp.int32, (tq, tq), 0))
        for hp in range(MLA_HEADS // 2):
            pv, al = [], []
            for e in range(2):
                h = 2 * hp + e
                s = _dot_nt(q_ref[0, :, h * HEAD_SLAB:(h + 1) * HEAD_SLAB],
                            k_ref[0, :, h * HEAD_SLAB:(h + 1) * HEAD_SLAB])
                if masked:
                    s = jnp.where(keep, s, NEG_INF)
                m_prev = m_scr[h]
                m_new = jnp.maximum(m_prev, jnp.max(s, axis=-1, keepdims=True))
                alpha = jnp.exp(m_prev - m_new)
                p = jnp.exp(s - jnp.concatenate([m_new] * (tq // LANE), axis=1))
                l_scr[h] = alpha * l_scr[h] + jnp.sum(p, axis=-1, keepdims=True)
                m_scr[h] = m_new
                pv.append(_dot(p.astype(BF16), v_ref[0, :, hp * LANE:(hp + 1) * LANE]))
                al.append(alpha)
            sl = slice(hp * LANE, (hp + 1) * LANE)
            acc_scr[:, sl] = jnp.where(low, al[0], al[1]) * acc_scr[:, sl] + jnp.where(low, pv[0], pv[1])

    @pl.when(ki < qi)
    def _():
        step(False)

    @pl.when(ki == qi)
    def _():
        step(True)

    @pl.when(ki == pl.num_programs(2) - 1)
    def _():
        for hp in range(MLA_HEADS // 2):
            sl = slice(hp * LANE, (hp + 1) * LANE)
            o_ref[0, :, sl] = acc_scr[:, sl] / jnp.where(low, l_scr[2 * hp], l_scr[2 * hp + 1])


def _flash(q, k, v, batch, seq, tq):
    hw = MLA_HEADS * HEAD_SLAB
    vw = MLA_HEADS * MLA_V
    nq = seq // tq
    q3, k3, v3 = q.reshape(batch, seq, hw), k.reshape(batch, seq, hw), v.reshape(batch, seq, vw)
    out = pl.pallas_call(
        functools.partial(_flash_body, tq=tq),
        grid=(batch, nq, nq),
        in_specs=[pl.BlockSpec((1, tq, hw), lambda b, i, j: (b, i, 0)),
                  pl.BlockSpec((1, tq, hw), lambda b, i, j: (b, jnp.minimum(i, j), 0)),
                  pl.BlockSpec((1, tq, vw), lambda b, i, j: (b, jnp.minimum(i, j), 0))],
        out_specs=pl.BlockSpec((1, tq, vw), lambda b, i, j: (b, i, 0)),
        out_shape=jax.ShapeDtypeStruct((batch, seq, vw), F32),
        scratch_shapes=[pltpu.VMEM((MLA_HEADS, tq, LANE), F32), pltpu.VMEM((MLA_HEADS, tq, LANE), F32),
                        pltpu.VMEM((tq, vw), F32)],
        compiler_params=_cparams(("parallel", "parallel", "arbitrary")),
        name="flash",
    )(q3, k3, v3)
    return out.reshape(batch * seq, vw)


def _paged_body(pt_ref, q_ref, qlat_ref, nlat_ref, nkpe_ref, lat_hbm, kpe_hbm, o_ref,
                lat_buf, kpe_buf, sem, *, layer, n_pages, n_new):
    b = pl.program_id(0)
    nb = pl.num_programs(0)
    rows = q_ref.shape[1]

    def copies(bb, slot):
        out = []
        for j in range(n_pages):
            pg = pt_ref[bb, j]
            dst = pl.ds(j * PAGE_SIZE, PAGE_SIZE)
            out.append(pltpu.make_async_copy(lat_hbm.at[layer, pg], lat_buf.at[slot, dst], sem.at[0, slot]))
            out.append(pltpu.make_async_copy(kpe_hbm.at[layer, pg], kpe_buf.at[slot, :, dst], sem.at[1, slot]))
        return out

    @pl.when(b == 0)
    def _():
        for c in copies(0, 0):
            c.start()

    slot = b % 2

    @pl.when(b + 1 < nb)
    def _():
        for c in copies(b + 1, 1 - slot):
            c.start()

    for c in copies(b, slot):
        c.wait()

    qlat = qlat_ref[0]
    qpe = q_ref[0][:, MLA_NOPE:MLA_NOPE + MLA_ROPE]
    kpe_t = kpe_buf[slot]
    s_past = _dot_nt(qlat.astype(F32), lat_buf[slot]) + _dot(qpe.astype(F32), kpe_t)
    nlat = nlat_ref[0].astype(BF16)
    nkpe = nkpe_ref[0][:, MLA_NOPE:MLA_NOPE + MLA_ROPE].astype(BF16)
    s_new = _dot_nt(qlat, nlat) + _dot_nt(qpe, nkpe)
    t_of_row = lax.broadcasted_iota(jnp.int32, (rows, n_new), 0) // MLA_HEADS
    s_new = jnp.where(lax.broadcasted_iota(jnp.int32, (rows, n_new), 1) <= t_of_row, s_new, NEG_INF)
    m = jnp.maximum(jnp.max(s_past, axis=-1, keepdims=True), jnp.max(s_new, axis=-1, keepdims=True))
    p_past = jnp.exp(s_past - m)
    p_new = jnp.exp(s_new - m)
    denom = jnp.sum(p_past, axis=-1, keepdims=True) + jnp.sum(p_new, axis=-1, keepdims=True)
    o = _dot(p_past.astype(BF16).astype(F32), lat_buf[slot]) + _dot(p_new.astype(BF16), nlat)
    o_ref[0] = o / denom


def _paged(page_table, q_cat, q_lat, lat_new, kpe_new, cache_latent, cache_k_rope, layer, batch, n_new):
    rows = n_new * MLA_HEADS
    n_pages = page_table.shape[1]
    past = n_pages * PAGE_SIZE
    q3 = q_cat.reshape(batch, rows, HEAD_SLAB)
    ql3 = q_lat.reshape(batch, rows, MLA_KV_LORA)
    nl3 = lat_new.reshape(batch, n_new, MLA_KV_LORA)
    nk3 = kpe_new.reshape(batch, n_new, HEAD_SLAB)
    blk = lambda r, w: pl.BlockSpec((1, r, w), lambda b, pt: (b, 0, 0))
    out = pl.pallas_call(
        functools.partial(_paged_body, layer=layer, n_pages=n_pages, n_new=n_new),
        grid_spec=pltpu.PrefetchScalarGridSpec(
            num_scalar_prefetch=1, grid=(batch,),
            in_specs=[blk(rows, HEAD_SLAB), blk(rows, MLA_KV_LORA), blk(n_new, MLA_KV_LORA), blk(n_new, HEAD_SLAB),
                      pl.BlockSpec(memory_space=pl.ANY), pl.BlockSpec(memory_space=pl.ANY)],
            out_specs=blk(rows, MLA_KV_LORA),
            scratch_shapes=[pltpu.VMEM((2, past, MLA_KV_LORA), F32), pltpu.VMEM((2, MLA_ROPE, past), F32),
                            pltpu.SemaphoreType.DMA((2, 2))]),
        out_shape=jax.ShapeDtypeStruct((batch, rows, MLA_KV_LORA), F32),
        compiler_params=_cparams(("arbitrary",)),
        name="paged",
    )(page_table, q3, ql3, nl3, nk3, cache_latent, jnp.swapaxes(cache_k_rope, 2, 3))
    return out.reshape(batch * n_new, MLA_HEADS * MLA_KV_LORA)


def _head_proj_body(o_ref, w_ref, y_ref):
    for h in range(MLA_HEADS):
        y_ref[:, h * MLA_V:(h + 1) * MLA_V] = _dot(
            o_ref[:, h * MLA_KV_LORA:(h + 1) * MLA_KV_LORA].astype(BF16), w_ref[h])


def _head_proj(o_lat, w_uv):
    n = o_lat.shape[0]
    w = w_uv.reshape(MLA_KV_LORA, MLA_HEADS, MLA_V).transpose(1, 0, 2).astype(BF16)
    return pl.pallas_call(
        _head_proj_body, grid=(1,),
        in_specs=[pl.BlockSpec(o_lat.shape, lambda i: (0, 0)), pl.BlockSpec(w.shape, lambda i: (0, 0, 0))],
        out_specs=pl.BlockSpec((n, MLA_HEADS * MLA_V), lambda i: (0, 0)),
        out_shape=jax.ShapeDtypeStruct((n, MLA_HEADS * MLA_V), F32),
        compiler_params=_cparams(("arbitrary",)), name="head_proj",
    )(o_lat, w)


N_REC_HEADS = 4
REC_WIDTH = N_REC_HEADS * HEAD_SLAB


def _split3(x):
    a = x.astype(BF16)
    r = x - a.astype(F32)
    b = r.astype(BF16)
    c = (r - b.astype(F32)).astype(BF16)
    return a, b, c


def _scan_body(*refs, gla, chunk, sub, valid, has_s0, k_dim, bb):
    if gla:
        q_ref, k_ref, v_ref, glow_ref, wg_ref, bg_ref, tri_ref = refs[:7]
        rest = refs[7:]
    else:
        q_ref, k_ref, v_ref, la_ref, l1_ref, tri_ref = refs[:6]
        rest = refs[6:]
    if has_s0:
        s0_ref, o_ref, sfin_ref, st_scr = rest
    else:
        o_ref, sfin_ref, st_scr = rest
    ci = pl.program_id(1)
    nh = N_REC_HEADS

    @pl.when(ci == 0)
    def _():
        if has_s0:
            for i in range(bb):
                for h in range(nh):
                    s0 = s0_ref[i, h]
                    if k_dim < HEAD_SLAB:
                        s0 = jnp.concatenate([s0, jnp.zeros((HEAD_SLAB - k_dim, s0.shape[1]), F32)], axis=0)
                    st_scr[i * nh + h] = s0.T
        else:
            st_scr[...] = jnp.zeros(st_scr.shape, F32)

    for i in range(bb):
        _scan_chunk(i, q_ref, k_ref, v_ref, refs, o_ref, st_scr, tri_ref,
                    gla=gla, chunk=chunk, sub=sub, valid=valid)

    @pl.when(ci == pl.num_programs(1) - 1)
    def _():
        for i in range(bb):
            for h in range(nh):
                sfin_ref[i, h] = st_scr[i * nh + h].T[:k_dim]


def _scan_chunk(row, q_ref, k_ref, v_ref, refs, o_ref, st_scr, tri_ref, *, gla, chunk, sub, valid):
    nh = N_REC_HEADS
    v = v_ref[row]
    if gla:
        glow_ref, wg_ref, bg_ref = refs[3:6]
        q = q_ref[row] * (GLA_DK ** -0.5)
        k = k_ref[row]
        g = _log_sigmoid(_dot(glow_ref[row].astype(BF16), wg_ref[...]) + bg_ref[...]) * (1.0 / GLA_GATE_NORMALIZER)
    else:
        la_ref, l1_ref = refs[3:5]
        xq = q_ref[row]
        q = xq * jax.nn.sigmoid(xq) * (HG_DK ** -0.5)
        a = la_ref[...]
        bb = l1_ref[...] + _log_sigmoid(k_ref[row])
        g = jnp.maximum(a, bb) + jnp.log1p(jnp.exp(-jnp.abs(a - bb)))
        k = 1.0 - jnp.exp(g)
    if valid < chunk:
        live = lax.broadcasted_iota(jnp.int32, (chunk, 1), 0) < valid
        g = jnp.where(live, g, 0.0)
        k = jnp.where(live, k, 0.0)

    tri = tri_ref[...]
    g1, g2, g3 = _split3(g)
    b = _dot(tri, g1) + _dot(tri, g2) + _dot(tri, g3)
    b_end = b[chunk - 1:chunk]
    hs = lambda h: slice(h * HEAD_SLAB, (h + 1) * HEAD_SLAB)
    b2 = b * LOG2_E
    qe = (q * jnp.exp(b)).astype(BF16)
    kd_f = k * jnp.exp(b_end - b)
    kd = kd_f.astype(BF16)
    kd_lo = (kd_f - kd.astype(F32)).astype(BF16)
    vb = v.astype(BF16)
    v_lo = (v - vb.astype(F32)).astype(BF16)

    o_inter = jnp.concatenate(
        [_dot_nt(qe[:, hs(h)], st_scr[row * nh + h].astype(BF16)) for h in range(nh)], axis=1)

    row_in_tile = lax.broadcasted_iota(jnp.int32, (SUBLANES, 1), 0)
    blocks = []
    for i in range(chunk // sub):
        lo = i * sub
        if lo >= valid:
            blocks.append(o_inter[lo:lo + sub])
            continue
        bi, qi_ = b[lo:lo + sub], q[lo:lo + sub]
        blk = o_inter[lo:lo + sub]
        if i > 0:
            r = b[lo - 1:lo]
            qt = (qi_ * jnp.exp(bi - r)).astype(BF16)
            kt = (k[:lo] * jnp.exp(r - b[:lo])).astype(BF16)
            off = []
            for h in range(nh):
                att = _dot_nt(qt[:, hs(h)], kt[:, hs(h)])
                off.append(_dot(att.astype(BF16), vb[:lo, hs(h)]))
            blk = blk + jnp.concatenate(off, axis=1)
        live_rows = min(sub, valid - lo)
        tiles = []
        for r0 in range(0, sub, SUBLANES):
            acc = jnp.zeros((SUBLANES, REC_WIDTH), F32)
            if r0 < live_rows:
                bt = b2[lo + r0:lo + r0 + SUBLANES]
                qt8 = qi_[r0:r0 + SUBLANES]
                rows8 = row_in_tile + r0
                for s in range(min(live_rows, r0 + SUBLANES)):
                    d = bt - b2[lo + s:lo + s + 1]
                    if s > r0:
                        d = jnp.minimum(d, 0.0)
                    w = qt8 * (k[lo + s:lo + s + 1] * jnp.exp2(d))
                    v_s = v[lo + s:lo + s + 1]
                    parts = []
                    for h in range(nh):
                        a_ts = jnp.sum(w[:, hs(h)], axis=-1, keepdims=True)
                        if s > r0:
                            a_ts = jnp.where(rows8 >= s, a_ts, 0.0)
                        parts.append(a_ts * v_s[:, hs(h)])
                    acc = acc + jnp.concatenate(parts, axis=1)
            tiles.append(acc)
        blocks.append(blk + jnp.concatenate(tiles, axis=0))
    o_ref[row] = jnp.concatenate(blocks, axis=0)

    decay = jnp.exp(b_end)
    for h in range(nh):
        upd = (_dot_tn(vb[:, hs(h)], kd[:, hs(h)]) + _dot_tn(vb[:, hs(h)], kd_lo[:, hs(h)])
               + _dot_tn(v_lo[:, hs(h)], kd[:, hs(h)]))
        st_scr[row * nh + h] = st_scr[row * nh + h] * decay[:, hs(h)] + upd


def _scan(z, batch, time, chunk, sub, valid, gla, extra, s0, bb):
    k_dim = GLA_DK if gla else HG_DK
    nck = time // chunk
    nh = N_REC_HEADS
    z3 = z.reshape(batch, time, Z_WIDTH)
    cb = lambda col: pl.BlockSpec((bb, chunk, REC_WIDTH), lambda b, c: (b, c, col // REC_WIDTH))
    full = lambda a: pl.BlockSpec(a.shape, lambda b, c: (0,) * a.ndim)
    state = pl.BlockSpec((bb, nh, k_dim, HEAD_SLAB), lambda b, c: (b, 0, 0, 0))
    tri = jnp.tril(jnp.ones((chunk, chunk), F32)).astype(BF16)
    if gla:
        wg, bg = extra
        ins = [z3, z3, z3, z3, wg, bg, tri]
        specs = [cb(Z_GQ), cb(Z_GK), cb(Z_GV),
                 pl.BlockSpec((bb, chunk, LANE), lambda b, c: (b, c, Z_GLOW // LANE)), full(wg), full(bg), full(tri)]
    else:
        la, l1 = extra
        ins = [z3, z3, z3, la, l1, tri]
        specs = [cb(Z_HQ), cb(Z_HF), cb(Z_HI), full(la), full(l1), full(tri)]
    if s0 is not None:
        ins.append(s0)
        specs.append(state)
    o, s_fin = pl.pallas_call(
        functools.partial(_scan_body, gla=gla, chunk=chunk, sub=sub, valid=valid, has_s0=s0 is not None,
                          k_dim=k_dim, bb=bb),
        grid=(batch // bb, nck),
        in_specs=specs,
        out_specs=[pl.BlockSpec((bb, chunk, REC_WIDTH), lambda b, c: (b, c, 0)), state],
        out_shape=[jax.ShapeDtypeStruct((batch, time, REC_WIDTH), F32),
                   jax.ShapeDtypeStruct((batch, nh, k_dim, HEAD_SLAB), F32)],
        scratch_shapes=[pltpu.VMEM((bb * nh, HEAD_SLAB, HEAD_SLAB), F32)],
        compiler_params=_cparams(("parallel", "arbitrary")),
        name="scan_gla" if gla else "scan_hgrn",
    )(*ins)
    return o.reshape(batch * time, REC_WIDTH), s_fin


def _merge_body(x_ref, ym_ref, oh_ref, og_ref, hgate_ref, ggate_ref, br0_ref, br1_ref, br2_ref,
                g1_ref, sc2_ref, sh2_ref, hgw_ref, glw_ref, nfw_ref, wbm_ref, wbh_ref, wbg_ref, wout_ref,
                rwh_ref, rwl_ref, rb_ref, tri_ref, cnt0_ref, x1_ref, h2_ref, te_ref, tw_ref, rk_ref, cnt_ref,
                carry_scr):
    silu = lambda t: t * jax.nn.sigmoid(t)
    yh = _rms(oh_ref[...]) * hgw_ref[...] * silu(hgate_ref[...])
    og = og_ref[...]
    glw = glw_ref[...]
    yg = jnp.concatenate([_rms(og[:, h * GLA_DV:(h + 1) * GLA_DV]) * glw for h in range(GLA_HEADS)], axis=1)
    yg = yg * silu(ggate_ref[...])
    m = (jax.nn.sigmoid(br0_ref[...]) * _dot(ym_ref[...].astype(BF16), wbm_ref[...])
         + jax.nn.sigmoid(br1_ref[...]) * _dot(yh.astype(BF16), wbh_ref[...])
         + jax.nn.sigmoid(br2_ref[...]) * _dot(yg.astype(BF16), wbg_ref[...]))
    x1 = x_ref[...] + g1_ref[0] * _dot(m.astype(BF16), wout_ref[...])
    x1_ref[...] = x1
    h2 = _rms(x1) * nfw_ref[...] * (1.0 + sc2_ref[0]) + sh2_ref[0]
    h2_ref[...] = h2
    hh = h2.astype(BF16)
    hl = (h2 - hh.astype(F32)).astype(BF16)
    rwh = rwh_ref[...]
    logits = _dot(hh, rwh) + _dot(hh, rwl_ref[...]) + _dot(hl, rwh) + rb_ref[...]

    lane = lax.broadcasted_iota(jnp.int32, logits.shape, 1).astype(F32)
    work = jnp.where(lane < N_EXPERTS, logits, NEG_INF)
    vals, idxs, hots = [], [], []
    for _ in range(TOP_K):
        mx = jnp.max(work, axis=1, keepdims=True)
        idx = jnp.min(jnp.where(work == mx, lane, float(LANE)), axis=1, keepdims=True)
        hot = lane == idx
        work = jnp.where(hot, NEG_INF, work)
        vals.append(mx)
        idxs.append(idx)
        hots.append(hot)
    ex = [jnp.exp(v - vals[0]) for v in vals]
    den = ex[0] + ex[1] + ex[2] + ex[3]

    @pl.when(pl.program_id(0) == 0)
    def _():
        carry_scr[...] = cnt0_ref[...]

    cnt = sum(h.astype(F32) for h in hots)
    before = _dot(tri_ref[...], cnt.astype(BF16)) + carry_scr[...]
    te = jnp.zeros(logits.shape, F32)
    tw = jnp.zeros(logits.shape, F32)
    rk = jnp.zeros(logits.shape, F32)
    for k in range(TOP_K):
        sel = lane == float(k)
        te = jnp.where(sel, idxs[k], te)
        tw = jnp.where(sel, ex[k] / den, tw)
        rk = jnp.where(sel, jnp.sum(jnp.where(hots[k], before, 0.0), axis=1, keepdims=True), rk)
    te_ref[...] = te.astype(jnp.int32)
    tw_ref[...] = tw
    rk_ref[...] = rk.astype(jnp.int32)
    carry_scr[...] = carry_scr[...] + jnp.sum(cnt, axis=0, keepdims=True)
    cnt_ref[...] = carry_scr[...]


def _merge(grp, x, y_mla, o_hg, o_gla, z, g1, sc2, sh2, lw, cnt0):
    tm = grp.tile(256)
    tri = jnp.tril(jnp.ones((tm, tm), F32), k=-1).astype(BF16)
    row = lambda w: pl.BlockSpec((tm, w), lambda i: (i, 0))
    zc = lambda col, w: pl.BlockSpec((tm, w), lambda i: (i, col // w))
    full = lambda a: pl.BlockSpec(a.shape, lambda i: (0,) * a.ndim)
    g1_a, g1_s = grp.mod(g1, tm)
    sc_a, sc_s = grp.mod(sc2, tm)
    sh_a, sh_s = grp.mod(sh2, tm)
    ws = [lw["hg_norm_w"], lw["gla_norm_w"], lw["norm_ffn_w"], lw["w_br_mla"], lw["w_br_hg"], lw["w_br_gla"],
          lw["w_out"], lw["rw_hi"], lw["rw_lo"], lw["rb"], tri, cnt0]
    n = grp.n_tok
    one = pl.BlockSpec((1, LANE), lambda i: (0, 0))
    return pl.pallas_call(
        _merge_body, grid=(grp.n_tok // tm,),
        in_specs=[row(D_MODEL), row(512), row(512), row(512), zc(Z_HGATE, 512), zc(Z_GGATE, 512),
                  zc(Z_BR, 1024), zc(Z_BR + 1024, 1024), zc(Z_BR + 2048, 1024),
                  g1_s, sc_s, sh_s] + [full(w) for w in ws],
        out_specs=[row(D_MODEL), row(D_MODEL), row(LANE), row(LANE), row(LANE), one],
        out_shape=[jax.ShapeDtypeStruct((n, D_MODEL), F32), jax.ShapeDtypeStruct((n, D_MODEL), F32),
                   jax.ShapeDtypeStruct((n, LANE), jnp.int32), jax.ShapeDtypeStruct((n, LANE), F32),
                   jax.ShapeDtypeStruct((n, LANE), jnp.int32), jax.ShapeDtypeStruct((1, LANE), F32)],
        scratch_shapes=[pltpu.VMEM((1, LANE), F32)],
        compiler_params=_cparams(("arbitrary",)), name="merge",
    )(x, y_mla, o_hg, o_gla, z, z, z, z, z, g1_a, sc_a, sh_a, *ws)


def _experts_body(be_ref, nu_ref, x_ref, wgu_ref, bgu_ref, wd_ref, bd_ref, y_ref, wgu_scr, wd_scr):
    i = pl.program_id(0)
    e = be_ref[i]
    prev = be_ref[jnp.maximum(i - 1, 0)]

    @pl.when(jnp.logical_or(i == 0, e != prev))
    def _():
        wgu_scr[...] = wgu_ref[0].astype(BF16)
        wd_scr[...] = wd_ref[0].astype(BF16)

    @pl.when(i < nu_ref[0])
    def _():
        gu = _dot(x_ref[...].astype(BF16), wgu_scr[...]) + bgu_ref[0]
        gate = jnp.minimum(gu[:, :D_EXPERT], SWIGLU_LIMIT)
        up = jnp.clip(gu[:, D_EXPERT:], -SWIGLU_LIMIT, SWIGLU_LIMIT)
        act = (up + 1.0) * gate * jax.nn.sigmoid(SWIGLU_ALPHA * gate)
        y_ref[...] = _dot(act.astype(BF16), wd_scr[...]) + bd_ref[0]

    @pl.when(i >= nu_ref[0])
    def _():
        y_ref[...] = jnp.zeros(y_ref.shape, F32)


def _experts(block_e, n_used, xb, w_gate_up, b_gate_up, w_down, b_down, layer):
    n_slots = xb.shape[0]
    tm = MOE_TM
    ne = N_EXPERTS
    return pl.pallas_call(
        _experts_body,
        grid_spec=pltpu.PrefetchScalarGridSpec(
            num_scalar_prefetch=2, grid=(n_slots // tm,),
            in_specs=[pl.BlockSpec((tm, D_MODEL), lambda i, be, nu: (i, 0)),
                      pl.BlockSpec((1, D_MODEL, 2 * D_EXPERT), lambda i, be, nu: (layer * ne + be[i], 0, 0)),
                      pl.BlockSpec((1, 1, 2 * D_EXPERT), lambda i, be, nu: (layer * ne + be[i], 0, 0)),
                      pl.BlockSpec((1, D_EXPERT, D_MODEL), lambda i, be, nu: (layer * ne + be[i], 0, 0)),
                      pl.BlockSpec((1, 1, D_MODEL), lambda i, be, nu: (layer * ne + be[i], 0, 0))],
            out_specs=pl.BlockSpec((tm, D_MODEL), lambda i, be, nu: (i, 0)),
            scratch_shapes=[pltpu.VMEM((D_MODEL, 2 * D_EXPERT), BF16), pltpu.VMEM((D_EXPERT, D_MODEL), BF16)]),
        out_shape=jax.ShapeDtypeStruct((n_slots, D_MODEL), F32),
        compiler_params=_cparams(("arbitrary",)),
        name="experts",
    )(block_e, n_used, xb,
      w_gate_up.reshape(DEPTH * ne, D_MODEL, 2 * D_EXPERT), b_gate_up.reshape(DEPTH * ne, 1, 2 * D_EXPERT),
      w_down.reshape(DEPTH * ne, D_EXPERT, D_MODEL), b_down.reshape(DEPTH * ne, 1, D_MODEL))


def _route(top_e, rank, counts):
    n = top_e.shape[0]
    nk = n * TOP_K
    tm = MOE_TM
    experts = jnp.arange(N_EXPERTS, dtype=jnp.int32)
    padded = (counts + tm - 1) // tm * tm
    pad_end = jnp.cumsum(padded)
    pad_start = pad_end - padded
    start = jnp.cumsum(counts) - counts
    dest = jnp.sum(jnp.where(top_e[..., None] == experts, pad_start, 0), axis=-1) + rank
    flat_tok = jnp.arange(nk, dtype=jnp.int32) // TOP_K
    _, stok = lax.sort((dest.reshape(-1), flat_tok), num_keys=1)
    n_blocks = (nk + N_EXPERTS * (tm - 1) + tm - 1) // tm
    block_lo = jnp.arange(n_blocks, dtype=jnp.int32) * tm
    block_e = jnp.minimum(jnp.sum((pad_end[None, :] <= block_lo[:, None]).astype(jnp.int32), axis=1),
                          N_EXPERTS - 1)
    off = (block_lo - pad_start[block_e])[:, None] + jnp.arange(tm, dtype=jnp.int32)[None, :]
    live = off < counts[block_e][:, None]
    src = jnp.clip(start[block_e][:, None] + off, 0, nk - 1)
    slot_tok = jnp.where(live, stok[src.reshape(-1)].reshape(n_blocks, tm), 0).reshape(-1)
    n_used = (pad_end[-1] // tm).astype(jnp.int32).reshape(1)
    return slot_tok, block_e.astype(jnp.int32), n_used, dest


def _final_norm_body(x_ref, w_ref, o_ref):
    o_ref[...] = _rms(x_ref[...]) * w_ref[...]


def _final_norm(x, w, tm):
    n = x.shape[0]
    return pl.pallas_call(
        _final_norm_body, grid=(n // tm,),
        in_specs=[pl.BlockSpec((tm, D_MODEL), lambda i: (i, 0)), pl.BlockSpec((1, D_MODEL), lambda i: (0, 0))],
        out_specs=pl.BlockSpec((tm, D_MODEL), lambda i: (i, 0)),
        out_shape=jax.ShapeDtypeStruct((n, D_MODEL), F32),
        compiler_params=_cparams(("parallel",)), name="final_norm",
    )(x, w.reshape(1, D_MODEL))


def kernel(x_prompt, x_sample, cache_latent, cache_k_rope, state_hgrn, state_gla, page_table, c_prompt, c_sample, norm_mix_w, norm_ffn_w, final_norm_w, w_ada, b_ada, w_in, mla_q_norm_w, mla_w_uq, mla_kv_norm_w, mla_w_uk, mla_w_uv, hgrn_lower_bounds, hgrn_norm_w, gla_w_gate, gla_b_gate, gla_norm_w, w_branch_mla, w_branch_hgrn, w_branch_gla, w_out, router_w, router_b, w_gate_up, b_gate_up, w_down, b_down):
    bp, seq, _ = x_prompt.shape
    bs, tnew, _ = x_sample.shape
    n_p, n_s = bp * seq, bs * tnew
    past_len = page_table.shape[1] * cache_latent.shape[2]
    gp = _Group(bp, seq, per_token=False)
    gs = _Group(bs, tnew, per_token=True)
    s_pad = SCAN_SUB

    rows = bp + bs
    rows_pad = -(-rows // 8) * 8
    c_all = jnp.pad(jnp.concatenate([c_prompt, c_sample], axis=0), ((0, rows_pad - rows), (0, 0)))
    mod = _adaln(c_all, w_ada, b_ada)

    lbs = jax.nn.softmax(hgrn_lower_bounds.astype(F32), axis=0)
    lbs = jnp.cumsum(lbs, axis=0) - lbs[0]
    log_lb = jnp.log(jnp.maximum(lbs, TINY))
    log_1m_lb = jnp.log1p(-lbs)

    ct_p, st_p = _rope_tables(jnp.arange(seq, dtype=jnp.int32))
    pos_s = past_len + jnp.arange(tnew, dtype=jnp.int32)
    ct_s, st_s = _rope_tables(jnp.tile(pos_s, bs))

    xp = x_prompt.reshape(n_p, D_MODEL)
    xs = x_sample.reshape(n_s, D_MODEL)
    lat_p, kpe_p, hg_p, gla_p, lat_s, kpe_s, hg_s, gla_s = [], [], [], [], [], [], [], []

    for l in range(DEPTH):
        w_pad = _pad_w_in(w_in[l])
        mw = _mla_weights(mla_w_uq[l], mla_w_uk[l], mla_w_uv[l])
        rw = jnp.pad(router_w[l], ((0, 0), (0, LANE - N_EXPERTS)))
        rw_hi = rw.astype(BF16)
        lw = dict(hg_norm_w=hgrn_norm_w[l].reshape(1, -1), gla_norm_w=gla_norm_w[l].reshape(1, -1),
                  norm_ffn_w=norm_ffn_w[l].reshape(1, -1), w_br_mla=w_branch_mla[l].astype(BF16),
                  w_br_hg=w_branch_hgrn[l].astype(BF16), w_br_gla=w_branch_gla[l].astype(BF16),
                  w_out=w_out[l].astype(BF16), rw_hi=rw_hi, rw_lo=(rw - rw_hi.astype(F32)).astype(BF16),
                  rb=jnp.pad(router_b[l], (0, LANE - N_EXPERTS)).reshape(1, LANE))
        wg = jnp.pad(gla_w_gate[l].reshape(GLA_GATE_RANK, GLA_HEADS, GLA_DK),
                     ((0, LANE - GLA_GATE_RANK), (0, 0), (0, HEAD_SLAB - GLA_DK))).reshape(LANE, REC_WIDTH).astype(BF16)
        bg = jnp.pad(gla_b_gate[l].reshape(GLA_HEADS, GLA_DK), ((0, 0), (0, HEAD_SLAB - GLA_DK))).reshape(1, REC_WIDTH)
        hg_extra = (log_lb[l].reshape(1, -1), log_1m_lb[l].reshape(1, -1))
        split6 = lambda m: jnp.split(m, 6, axis=-1)
        sh1p, sc1p, g1p, sh2p, sc2p, g2p = split6(mod[l, :bp])
        sh1s, sc1s, g1s, sh2s, sc2s, g2s = split6(mod[l, bp:rows])

        zp = _in_proj(gp, xp, norm_mix_w[l], sc1p, sh1p, w_pad)
        q, k, v, latp, kpep = _mla_prep(gp, zp, mla_q_norm_w[l], mla_kv_norm_w[l], mw, ct_p, st_p, sample=False)
        y_mla_p = _flash(q, k, v, bp, seq, tq=512)
        o_hg_p, s_hg_p = _scan(zp, bp, seq, SCAN_CHUNK, SCAN_SUB, SCAN_CHUNK, False, hg_extra, None, bp)
        o_gl_p, s_gl_p = _scan(zp, bp, seq, SCAN_CHUNK, SCAN_SUB, SCAN_CHUNK, True, (wg, bg), None, bp)
        x1p, h2p, te_p, tw_p, rk_p, cnt_p = _merge(gp, xp, y_mla_p, o_hg_p, o_gl_p, zp, g1p, sc2p, sh2p, lw,
                                                   jnp.zeros((1, LANE), F32))

        zs = _in_proj(gs, xs, norm_mix_w[l], sc1s, sh1s, w_pad)
        qs, qlat, lats, kpes = _mla_prep(gs, zs, mla_q_norm_w[l], mla_kv_norm_w[l], mw, ct_s, st_s, sample=True)
        o_lat = _paged(page_table, qs, qlat, lats, kpes, cache_latent, cache_k_rope, l, bs, tnew)
        y_mla_s = _head_proj(o_lat, mla_w_uv[l])
        zs_pad = jnp.pad(zs.reshape(bs, tnew, Z_WIDTH), ((0, 0), (0, s_pad - tnew), (0, 0))).reshape(bs * s_pad, Z_WIDTH)
        unpad = lambda o: o.reshape(bs, s_pad, -1)[:, :tnew].reshape(n_s, -1)
        o_hg_s, s_hg_s = _scan(zs_pad, bs, s_pad, s_pad, SCAN_SUB, tnew, False, hg_extra, state_hgrn[l], SAMPLE_ROWS_PER_STEP)
        o_gl_s, s_gl_s = _scan(zs_pad, bs, s_pad, s_pad, SCAN_SUB, tnew, True, (wg, bg), state_gla[l], SAMPLE_ROWS_PER_STEP)
        x1s, h2s, te_s, tw_s, rk_s, cnt_all = _merge(gs, xs, y_mla_s, unpad(o_hg_s), unpad(o_gl_s), zs, g1s, sc2s,
                                                     sh2s, lw, cnt_p)

        h2 = jnp.concatenate([h2p, h2s], axis=0)
        both = lambda a, b: jnp.concatenate([a, b], axis=0)[:, :TOP_K]
        top_w = both(tw_p, tw_s)
        slot_tok, block_e, n_used, dest = _route(both(te_p, te_s), both(rk_p, rk_s),
                                                 cnt_all[0, :N_EXPERTS].astype(jnp.int32))
        yb = _experts(block_e, n_used, h2[slot_tok], w_gate_up, b_gate_up, w_down, b_down, l)
        combine = lambda w, d: sum(w[:, k:k + 1] * yb[d[:, k]] for k in range(TOP_K))
        xp = x1p + jnp.repeat(g2p, seq, axis=0) * combine(top_w[:n_p], dest[:n_p])
        xs = x1s + jnp.repeat(g2s, tnew, axis=0) * combine(top_w[n_p:], dest[n_p:])

        sl = slice(MLA_NOPE, MLA_NOPE + MLA_ROPE)
        lat_p.append(latp.reshape(bp, seq, -1)); kpe_p.append(kpep[:, sl].reshape(bp, seq, -1))
        hg_p.append(s_hg_p); gla_p.append(s_gl_p)
        lat_s.append(lats.reshape(bs, tnew, -1)); kpe_s.append(kpes[:, sl].reshape(bs, tnew, -1))
        hg_s.append(s_hg_s); gla_s.append(s_gl_s)

    y_prompt = _final_norm(xp, final_norm_w, 1024).reshape(bp, seq, D_MODEL)
    y_sample = _final_norm(xs, final_norm_w, n_s).reshape(bs, tnew, D_MODEL)
    st = jnp.stack
    return (y_prompt, y_sample, st(lat_p), st(kpe_p), st(hg_p), st(gla_p),
            st(lat_s), st(kpe_s), st(hg_s), st(gla_s))
```

```python
import functools

import numpy as np
import jax
import jax.numpy as jnp
from jax import lax
from jax.experimental import pallas as pl
from jax.experimental.pallas import tpu as pltpu

F32 = jnp.float32
BF16 = jnp.bfloat16

D_MODEL = 1024
DEPTH = 2
PAGE_SIZE = 128
MLA_HEADS = 8
MLA_NOPE = 64
MLA_ROPE = 32
MLA_V = 64
MLA_Q_LORA = 384
MLA_KV_LORA = 256
MLA_SCALE = (MLA_NOPE + MLA_ROPE) ** -0.5
ROPE_BASE = 10000.0
HG_HEADS = 4
HG_DK = 128
HG_DV = 128
GLA_HEADS = 4
GLA_DK = 64
GLA_DV = 128
GLA_GATE_RANK = 16
GLA_GATE_NORMALIZER = 16.0
N_EXPERTS = 32
TOP_K = 4
D_EXPERT = 1024
SWIGLU_LIMIT = 7.0
SWIGLU_ALPHA = 1.702
EPS = 1e-6
NEG_INF = -1e30
TINY = 1e-30
LOG2_E = 1.4426950408889634

IN_SPLITS = (MLA_Q_LORA, MLA_KV_LORA, MLA_ROPE, 512, 512, 512, 512, 256, 256, 512, 512, GLA_GATE_RANK,
             3 * D_MODEL)

LANE = 128
SUBLANES = 8
HEAD_SLAB = 128
VMEM_LIMIT = 56 * 1024 * 1024

Z_WIDTH = 8192
Z_CQ, Z_KR, Z_CKV, Z_KRR, Z_GLOW = 0, 384, 512, 768, 896
Z_HQ, Z_HF, Z_HI, Z_HGATE = 1024, 1536, 2048, 2560
Z_GQ, Z_GK, Z_GV, Z_GGATE, Z_BR = 3072, 3584, 4096, 4608, 5120

SCAN_CHUNK = 64
SCAN_SUB = 16
MOE_TM = 256
SAMPLE_ROWS_PER_STEP = 4
PAGED_ROWS_PER_STEP = 2


def _cparams(sem, vmem=VMEM_LIMIT):
    return pltpu.CompilerParams(dimension_semantics=sem, vmem_limit_bytes=vmem)


def _dot(a, b):
    return jnp.dot(a, b, preferred_element_type=F32)


def _dot_nt(a, b):
    return lax.dot_general(a, b, (((1,), (1,)), ((), ())), preferred_element_type=F32)


def _dot_tn(a, b):
    return lax.dot_general(a, b, (((0,), (0,)), ((), ())), preferred_element_type=F32)


def _rms(x):
    return x * lax.rsqrt(jnp.mean(x * x, axis=-1, keepdims=True) + EPS)


def _log_sigmoid(x):
    return jnp.minimum(x, 0.0) - jnp.log1p(jnp.exp(-jnp.abs(x)))


def _adaln_body(c_ref, w_ref, b_ref, o_ref):
    c = c_ref[...]
    a = (c * jax.nn.sigmoid(c)).astype(BF16)
    o_ref[0] = _dot(a, w_ref[0].astype(BF16)) + b_ref[0]


def _adaln(c_all, w_ada, b_ada):
    rows = c_all.shape[0]
    tn = 1536
    n_out = w_ada.shape[-1]
    return pl.pallas_call(
        _adaln_body,
        grid=(DEPTH, n_out // tn),
        in_specs=[pl.BlockSpec((rows, D_MODEL), lambda l, j: (0, 0)),
                  pl.BlockSpec((1, D_MODEL, tn), lambda l, j: (l, 0, j)),
                  pl.BlockSpec((1, 1, tn), lambda l, j: (l, 0, j))],
        out_specs=pl.BlockSpec((1, rows, tn), lambda l, j: (l, 0, j)),
        out_shape=jax.ShapeDtypeStruct((DEPTH, rows, n_out), F32),
        compiler_params=_cparams(("parallel", "parallel")),
        name="adaln",
    )(c_all, w_ada, b_ada.reshape(DEPTH, 1, n_out))


class _Group:
    def __init__(self, batch, time, per_token):
        self.batch, self.time, self.per_token = batch, time, per_token
        self.n_tok = batch * time

    def tile(self, tm):
        tm = min(tm, self.n_tok)
        assert (self.n_tok if self.per_token else self.time) % tm == 0
        return tm

    def mod(self, m, tm):
        if self.per_token:
            arr = jnp.repeat(m, self.time, axis=0)[None]
            return arr, pl.BlockSpec((1, tm, m.shape[-1]), lambda *g: (0, g[0], 0))
        arr = m[:, None, :]
        per = self.time // tm
        return arr, pl.BlockSpec((1, 1, m.shape[-1]), lambda *g: (g[0] // per, 0, 0))


def _in_proj_body(x_ref, nw_ref, sc_ref, sh_ref, w_ref, z_ref, h_scr):
    @pl.when(pl.program_id(1) == 0)
    def _():
        h = _rms(x_ref[...]) * nw_ref[...]
        h_scr[...] = (h * (1.0 + sc_ref[0]) + sh_ref[0]).astype(BF16)

    z_ref[...] = _dot(h_scr[...], w_ref[...])


def _in_proj(grp, x, norm_w, sc, sh, w_pad):
    tm, tn = grp.tile(1024), 1024
    sc_a, sc_s = grp.mod(sc, tm)
    sh_a, sh_s = grp.mod(sh, tm)
    return pl.pallas_call(
        _in_proj_body,
        grid=(grp.n_tok // tm, Z_WIDTH // tn),
        in_specs=[pl.BlockSpec((tm, D_MODEL), lambda i, j: (i, 0)),
                  pl.BlockSpec((1, D_MODEL), lambda i, j: (0, 0)),
                  sc_s, sh_s,
                  pl.BlockSpec((D_MODEL, tn), lambda i, j: (0, j))],
        out_specs=pl.BlockSpec((tm, tn), lambda i, j: (i, j)),
        out_shape=jax.ShapeDtypeStruct((grp.n_tok, Z_WIDTH), F32),
        scratch_shapes=[pltpu.VMEM((tm, D_MODEL), BF16)],
        compiler_params=_cparams(("parallel", "arbitrary")),
        name="in_proj",
    )(x, norm_w.reshape(1, D_MODEL), sc_a, sh_a, w_pad)


def _pad_w_in(w):
    idx = np.cumsum(IN_SPLITS)[:-1].tolist()
    cq, ckv, kr, hq, hf, hi, hgate, gq, gk, gv, ggate, glow, br = jnp.split(w, idx, axis=1)
    zeros = lambda n: jnp.zeros((w.shape[0], n), w.dtype)
    half = MLA_ROPE // 2
    kr_rot = jnp.concatenate([-kr[:, half:], kr[:, :half]], axis=1)
    slab = lambda a: jnp.concatenate([zeros(MLA_NOPE), a, zeros(HEAD_SLAB - MLA_NOPE - MLA_ROPE)], axis=1)
    pad_heads = lambda a: jnp.pad(a.reshape(-1, GLA_HEADS, GLA_DK),
                                  ((0, 0), (0, 0), (0, HEAD_SLAB - GLA_DK))).reshape(-1, GLA_HEADS * HEAD_SLAB)
    glow_slab = jnp.concatenate([glow, zeros(LANE - GLA_GATE_RANK)], axis=1)
    out = jnp.concatenate([cq, slab(kr), ckv, slab(kr_rot), glow_slab, hq, hf, hi, hgate,
                           pad_heads(gq), pad_heads(gk), gv, ggate, br], axis=1)
    assert out.shape[1] == Z_WIDTH
    return out.astype(BF16)


def _mla_weights(w_uq, w_uk, w_uv):
    hd = MLA_NOPE + MLA_ROPE
    half = MLA_ROPE // 2
    q = w_uq.reshape(MLA_Q_LORA, MLA_HEADS, hd)
    nope, pe = q[..., :MLA_NOPE], q[..., MLA_NOPE:]
    pe_rot = jnp.concatenate([-pe[..., half:], pe[..., :half]], axis=-1)
    z = lambda n: jnp.zeros((MLA_Q_LORA, MLA_HEADS, n), w_uq.dtype)
    wa = jnp.concatenate([nope, pe, z(HEAD_SLAB - hd)], axis=-1).reshape(MLA_Q_LORA, -1)
    wb = jnp.concatenate([z(MLA_NOPE), pe_rot, z(HEAD_SLAB - hd)], axis=-1).reshape(MLA_Q_LORA, -1)
    k = w_uk.reshape(MLA_KV_LORA, MLA_HEADS, MLA_NOPE)
    wka = jnp.pad(k, ((0, 0), (0, 0), (0, HEAD_SLAB - MLA_NOPE))).reshape(MLA_KV_LORA, -1)
    wukt = jnp.pad(k.transpose(1, 2, 0), ((0, 0), (0, HEAD_SLAB - MLA_NOPE), (0, 0)))
    return wa.astype(BF16), wb.astype(BF16), wka.astype(BF16), w_uv.astype(BF16), wukt.astype(BF16)


def _rope_tables(pos):
    half = MLA_ROPE // 2
    inv = ROPE_BASE ** (-jnp.arange(half, dtype=F32) / half)
    ang = pos.astype(F32)[:, None] * inv[None, :]
    cos, sin = jnp.cos(ang), jnp.sin(ang)
    n = pos.shape[0]
    tail = jnp.zeros((n, HEAD_SLAB - MLA_NOPE - MLA_ROPE), F32)
    ct = jnp.concatenate([jnp.ones((n, MLA_NOPE), F32), cos, cos, tail], axis=1)
    st = jnp.concatenate([jnp.zeros((n, MLA_NOPE), F32), sin, sin, tail], axis=1)
    return ct, st


def _mla_common(z_ref, qnw_ref, kvnw_ref, wa_ref, wb_ref, ct_ref, st_ref):
    z = z_ref[...]
    qn = (_rms(z[:, Z_CQ:Z_CQ + MLA_Q_LORA]) * qnw_ref[...]).astype(BF16)
    lat = _rms(z[:, Z_CKV:Z_CKV + MLA_KV_LORA]) * kvnw_ref[...]
    ct, st = ct_ref[...], st_ref[...]
    ct8 = jnp.concatenate([ct] * MLA_HEADS, axis=1)
    st8 = jnp.concatenate([st] * MLA_HEADS, axis=1)
    q_cat = (_dot(qn, wa_ref[...]) * ct8 + _dot(qn, wb_ref[...]) * st8) * (MLA_SCALE * LOG2_E)
    kpe = z[:, Z_KR:Z_KR + HEAD_SLAB] * ct + z[:, Z_KRR:Z_KRR + HEAD_SLAB] * st
    return q_cat, lat, kpe


def _mla_prep_prompt_body(z_ref, qnw_ref, kvnw_ref, wa_ref, wb_ref, ct_ref, st_ref, wka_ref, wv_ref,
                          q_ref, k_ref, v_ref, lat_ref, kpe_ref):
    q_cat, lat, kpe = _mla_common(z_ref, qnw_ref, kvnw_ref, wa_ref, wb_ref, ct_ref, st_ref)
    q_ref[...] = q_cat.astype(BF16)
    lat_ref[...] = lat
    kpe_ref[...] = kpe
    lb = lat.astype(BF16)
    k_ref[...] = (_dot(lb, wka_ref[...]) + jnp.concatenate([kpe] * MLA_HEADS, axis=1)).astype(BF16)
    v_ref[...] = _dot(lb, wv_ref[...]).astype(BF16)


def _mla_prep_sample_body(z_ref, qnw_ref, kvnw_ref, wa_ref, wb_ref, ct_ref, st_ref, wukt_ref,
                          q_ref, qlat_ref, lat_ref, kpe_ref):
    q_cat, lat, kpe = _mla_common(z_ref, qnw_ref, kvnw_ref, wa_ref, wb_ref, ct_ref, st_ref)
    qb = q_cat.astype(BF16)
    q_ref[...] = qb
    lat_ref[...] = lat
    kpe_ref[...] = kpe
    for h in range(MLA_HEADS):
        qlat_ref[:, h * MLA_KV_LORA:(h + 1) * MLA_KV_LORA] = _dot(
            qb[:, h * HEAD_SLAB:(h + 1) * HEAD_SLAB], wukt_ref[h]).astype(BF16)


def _mla_prep(grp, z, q_norm_w, kv_norm_w, mw, ct, st, sample):
    wa, wb, wka, wv, wukt = mw
    tm = grp.tile(512)
    n_tiles = grp.n_tok // tm
    hw = MLA_HEADS * HEAD_SLAB
    full = lambda a: pl.BlockSpec(a.shape, lambda i: (0,) * a.ndim)
    row = lambda w: pl.BlockSpec((tm, w), lambda i: (i, 0))
    if grp.per_token:
        tab = pl.BlockSpec((tm, HEAD_SLAB), lambda i: (i, 0))
    else:
        per = grp.time // tm
        tab = pl.BlockSpec((tm, HEAD_SLAB), lambda i: (i % per, 0))
    qnw = q_norm_w.reshape(1, -1)
    kvnw = kv_norm_w.reshape(1, -1)
    common_in = [pl.BlockSpec((tm, 1024), lambda i: (i, 0)), full(qnw), full(kvnw), full(wa), full(wb), tab, tab]
    n = grp.n_tok
    if sample:
        return pl.pallas_call(
            _mla_prep_sample_body, grid=(n_tiles,),
            in_specs=common_in + [full(wukt)],
            out_specs=[row(hw), row(MLA_HEADS * MLA_KV_LORA), row(MLA_KV_LORA), row(HEAD_SLAB)],
            out_shape=[jax.ShapeDtypeStruct((n, hw), BF16),
                       jax.ShapeDtypeStruct((n, MLA_HEADS * MLA_KV_LORA), BF16),
                       jax.ShapeDtypeStruct((n, MLA_KV_LORA), F32),
                       jax.ShapeDtypeStruct((n, HEAD_SLAB), F32)],
            compiler_params=_cparams(("parallel",)), name="mla_prep_sample",
        )(z, qnw, kvnw, wa, wb, ct, st, wukt)
    return pl.pallas_call(
        _mla_prep_prompt_body, grid=(n_tiles,),
        in_specs=common_in + [full(wka), full(wv)],
        out_specs=[row(hw), row(hw), row(MLA_HEADS * MLA_V), row(MLA_KV_LORA), row(HEAD_SLAB)],
        out_shape=[jax.ShapeDtypeStruct((n, hw), BF16),
                   jax.ShapeDtypeStruct((n, hw), BF16),
                   jax.ShapeDtypeStruct((n, MLA_HEADS * MLA_V), BF16),
                   jax.ShapeDtypeStruct((n, MLA_KV_LORA), F32),
                   jax.ShapeDtypeStruct((n, HEAD_SLAB), F32)],
        compiler_params=_cparams(("parallel",)), name="mla_prep_prompt",
    )(z, qnw, kvnw, wa, wb, ct, st, wka, wv)


def _flash_body(q_ref, k_ref, v_ref, o_ref, m_scr, l_scr, acc_scr, *, tq):
    qi, ki = pl.program_id(1), pl.program_id(2)

    @pl.when(ki == 0)
    def _():
        m_scr[...] = jnp.full(m_scr.shape, NEG_INF, F32)
        l_scr[...] = jnp.zeros(l_scr.shape, F32)
        acc_scr[...] = jnp.zeros(acc_scr.shape, F32)

    low = lax.broadcasted_iota(jnp.int32, (tq, LANE), 1) < MLA_V

    def step(masked):
        if masked:
            keep = (lax.broadcasted_iota(jnp.int32, (tq, tq), 1)
                    <= lax.broadcasted_iota(jnp.int32, (tq, tq), 0))
        for hp in range(MLA_HEADS // 2):
            pv, al = [], []
            for e in range(2):
                h = 2 * hp + e
                s = _dot_nt(q_ref[0, :, h * HEAD_SLAB:(h + 1) * HEAD_SLAB],
                            k_ref[0, :, h * HEAD_SLAB:(h + 1) * HEAD_SLAB])
                if masked:
                    s = jnp.where(keep, s, NEG_INF)
                m_prev = m_scr[h]
                m_new = jnp.maximum(m_prev, jnp.max(s, axis=-1, keepdims=True))
                alpha = jnp.exp2(m_prev - m_new)
                p = jnp.exp2(s - jnp.concatenate([m_new] * (tq // LANE), axis=1))
                l_scr[h] = alpha * l_scr[h] + jnp.sum(p, axis=-1, keepdims=True)
                m_scr[h] = m_new
                pv.append(_dot(p.astype(BF16), v_ref[0, :, hp * LANE:(hp + 1) * LANE]))
                al.append(alpha)
            sl = slice(hp * LANE, (hp + 1) * LANE)
            acc_scr[:, sl] = jnp.where(low, al[0], al[1]) * acc_scr[:, sl] + jnp.where(low, pv[0], pv[1])

    @pl.when(ki < qi)
    def _():
        step(False)

    @pl.when(ki == qi)
    def _():
        step(True)

    @pl.when(ki == pl.num_programs(2) - 1)
    def _():
        for hp in range(MLA_HEADS // 2):
            sl = slice(hp * LANE, (hp + 1) * LANE)
            o_ref[0, :, sl] = acc_scr[:, sl] / jnp.where(low, l_scr[2 * hp], l_scr[2 * hp + 1])


def _flash(q, k, v, batch, seq, tq):
    hw = MLA_HEADS * HEAD_SLAB
    vw = MLA_HEADS * MLA_V
    nq = seq // tq
    q3, k3, v3 = q.reshape(batch, seq, hw), k.reshape(batch, seq, hw), v.reshape(batch, seq, vw)
    out = pl.pallas_call(
        functools.partial(_flash_body, tq=tq),
        grid=(batch, nq, nq),
        in_specs=[pl.BlockSpec((1, tq, hw), lambda b, i, j: (b, i, 0)),
                  pl.BlockSpec((1, tq, hw), lambda b, i, j: (b, jnp.minimum(i, j), 0)),
                  pl.BlockSpec((1, tq, vw), lambda b, i, j: (b, jnp.minimum(i, j), 0))],
        out_specs=pl.BlockSpec((1, tq, vw), lambda b, i, j: (b, i, 0)),
        out_shape=jax.ShapeDtypeStruct((batch, seq, vw), F32),
        scratch_shapes=[pltpu.VMEM((MLA_HEADS, tq, LANE), F32), pltpu.VMEM((MLA_HEADS, tq, LANE), F32),
                        pltpu.VMEM((tq, vw), F32)],
        compiler_params=_cparams(("parallel", "parallel", "arbitrary")),
        name="flash",
    )(q3, k3, v3)
    return out.reshape(batch * seq, vw)


def _paged_body(pt_ref, q_ref, qlat_ref, nlat_ref, nkpe_ref, lat_hbm, kpe_hbm, o_ref,
                lat_buf, kpe_buf, sem, *, layer, n_pages, n_new, rows_per_step):
    g = pl.program_id(0)
    ng = pl.num_programs(0)
    rows = q_ref.shape[1]
    rps = rows_per_step

    def copies(step, slot):
        out = []
        for r in range(rps):
            for j in range(n_pages):
                pg = pt_ref[step * rps + r, j]
                dst = pl.ds(j * PAGE_SIZE, PAGE_SIZE)
                out.append(pltpu.make_async_copy(lat_hbm.at[layer, pg], lat_buf.at[slot, r, dst], sem.at[0, slot]))
                out.append(pltpu.make_async_copy(kpe_hbm.at[layer, pg], kpe_buf.at[slot, r, :, dst], sem.at[1, slot]))
        return out

    @pl.when(g == 0)
    def _():
        for c in copies(0, 0):
            c.start()

    slot = g % 2

    @pl.when(g + 1 < ng)
    def _():
        for c in copies(g + 1, 1 - slot):
            c.start()

    for c in copies(g, slot):
        c.wait()

    for r in range(rps):
        qlat = qlat_ref[r].astype(F32)
        qpe = q_ref[r][:, MLA_NOPE:MLA_NOPE + MLA_ROPE].astype(F32)
        s_past = _dot_nt(qlat, lat_buf[slot, r]) + _dot(qpe, kpe_buf[slot, r])
        nlat = nlat_ref[r].astype(BF16)
        nkpe = nkpe_ref[r][:, MLA_NOPE:MLA_NOPE + MLA_ROPE].astype(BF16)
        s_new = _dot_nt(qlat_ref[r], nlat) + _dot_nt(q_ref[r][:, MLA_NOPE:MLA_NOPE + MLA_ROPE], nkpe)
        t_of_row = lax.broadcasted_iota(jnp.int32, (rows, n_new), 0) // MLA_HEADS
        s_new = jnp.where(lax.broadcasted_iota(jnp.int32, (rows, n_new), 1) <= t_of_row, s_new, NEG_INF)
        m = jnp.maximum(jnp.max(s_past, axis=-1, keepdims=True), jnp.max(s_new, axis=-1, keepdims=True))
        p_past = jnp.exp2(s_past - m)
        p_new = jnp.exp2(s_new - m)
        denom = jnp.sum(p_past, axis=-1, keepdims=True) + jnp.sum(p_new, axis=-1, keepdims=True)
        o = _dot(p_past.astype(BF16).astype(F32), lat_buf[slot, r]) + _dot(p_new.astype(BF16), nlat)
        o_ref[r] = o / denom


def _paged(page_table, q_cat, q_lat, lat_new, kpe_new, cache_latent, cache_k_rope, layer, batch, n_new):
    rows = n_new * MLA_HEADS
    n_pages = page_table.shape[1]
    past = n_pages * PAGE_SIZE
    rps = PAGED_ROWS_PER_STEP
    q3 = q_cat.reshape(batch, rows, HEAD_SLAB)
    ql3 = q_lat.reshape(batch, rows, MLA_KV_LORA)
    nl3 = lat_new.reshape(batch, n_new, MLA_KV_LORA)
    nk3 = kpe_new.reshape(batch, n_new, HEAD_SLAB)
    blk = lambda r, w: pl.BlockSpec((rps, r, w), lambda b, pt: (b, 0, 0))
    out = pl.pallas_call(
        functools.partial(_paged_body, layer=layer, n_pages=n_pages, n_new=n_new, rows_per_step=rps),
        grid_spec=pltpu.PrefetchScalarGridSpec(
            num_scalar_prefetch=1, grid=(batch // rps,),
            in_specs=[blk(rows, HEAD_SLAB), blk(rows, MLA_KV_LORA), blk(n_new, MLA_KV_LORA), blk(n_new, HEAD_SLAB),
                      pl.BlockSpec(memory_space=pl.ANY), pl.BlockSpec(memory_space=pl.ANY)],
            out_specs=blk(rows, MLA_KV_LORA),
            scratch_shapes=[pltpu.VMEM((2, rps, past, MLA_KV_LORA), F32), pltpu.VMEM((2, rps, MLA_ROPE, past), F32),
                            pltpu.SemaphoreType.DMA((2, 2))]),
        out_shape=jax.ShapeDtypeStruct((batch, rows, MLA_KV_LORA), F32),
        compiler_params=_cparams(("arbitrary",)),
        name="paged",
    )(page_table, q3, ql3, nl3, nk3, cache_latent, jnp.swapaxes(cache_k_rope, 2, 3))
    return out.reshape(batch * n_new, MLA_HEADS * MLA_KV_LORA)


def _head_proj_body(o_ref, w_ref, y_ref):
    for h in range(MLA_HEADS):
        y_ref[:, h * MLA_V:(h + 1) * MLA_V] = _dot(
            o_ref[:, h * MLA_KV_LORA:(h + 1) * MLA_KV_LORA].astype(BF16), w_ref[h])


def _head_proj(o_lat, w_uv):
    n = o_lat.shape[0]
    w = w_uv.reshape(MLA_KV_LORA, MLA_HEADS, MLA_V).transpose(1, 0, 2).astype(BF16)
    return pl.pallas_call(
        _head_proj_body, grid=(1,),
        in_specs=[pl.BlockSpec(o_lat.shape, lambda i: (0, 0)), pl.BlockSpec(w.shape, lambda i: (0, 0, 0))],
        out_specs=pl.BlockSpec((n, MLA_HEADS * MLA_V), lambda i: (0, 0)),
        out_shape=jax.ShapeDtypeStruct((n, MLA_HEADS * MLA_V), F32),
        compiler_params=_cparams(("arbitrary",)), name="head_proj",
    )(o_lat, w)


N_REC_HEADS = 4
REC_WIDTH = N_REC_HEADS * HEAD_SLAB


def _split3(x):
    a = x.astype(BF16)
    r = x - a.astype(F32)
    b = r.astype(BF16)
    c = (r - b.astype(F32)).astype(BF16)
    return a, b, c


def _scan_body(*refs, gla, chunk, sub, valid, has_s0, k_dim, bb):
    if gla:
        q_ref, k_ref, v_ref, glow_ref, wg_ref, bg_ref, tri_ref = refs[:7]
        rest = refs[7:]
    else:
        q_ref, k_ref, v_ref, la_ref, l1_ref, tri_ref = refs[:6]
        rest = refs[6:]
    if has_s0:
        s0_ref, _, o_ref, sfin_ref, st_scr = rest
    else:
        _, o_ref, sfin_ref, st_scr = rest
    ci = pl.program_id(1)
    nh = N_REC_HEADS

    @pl.when(ci == 0)
    def _():
        if has_s0:
            for i in range(bb):
                for h in range(nh):
                    s0 = s0_ref[i, h]
                    if k_dim < HEAD_SLAB:
                        s0 = jnp.concatenate([s0, jnp.zeros((HEAD_SLAB - k_dim, s0.shape[1]), F32)], axis=0)
                    st_scr[i * nh + h] = s0.T
        else:
            st_scr[...] = jnp.zeros(st_scr.shape, F32)

    for i in range(bb):
        _scan_chunk(i, q_ref, k_ref, v_ref, refs, o_ref, st_scr, tri_ref,
                    gla=gla, chunk=chunk, sub=sub, valid=valid)

    @pl.when(ci == pl.num_programs(1) - 1)
    def _():
        for i in range(bb):
            for h in range(nh):
                sfin_ref[0, i, h] = st_scr[i * nh + h].T[:k_dim]


def _scan_chunk(row, q_ref, k_ref, v_ref, refs, o_ref, st_scr, tri_ref, *, gla, chunk, sub, valid):
    nh = N_REC_HEADS
    v = v_ref[row]
    if gla:
        glow_ref, wg_ref, bg_ref = refs[3:6]
        q = q_ref[row] * (GLA_DK ** -0.5)
        k = k_ref[row]
        g = _log_sigmoid(_dot(glow_ref[row].astype(BF16), wg_ref[...]) + bg_ref[...]) * (1.0 / GLA_GATE_NORMALIZER)
    else:
        la_ref, l1_ref = refs[3:5]
        xq = q_ref[row]
        q = xq * jax.nn.sigmoid(xq) * (HG_DK ** -0.5)
        a = la_ref[...]
        bb = l1_ref[...] + _log_sigmoid(k_ref[row])
        g = jnp.maximum(a, bb) + jnp.log1p(jnp.exp(-jnp.abs(a - bb)))
        k = 1.0 - jnp.exp(g)
    if valid < chunk:
        live = lax.broadcasted_iota(jnp.int32, (chunk, 1), 0) < valid
        g = jnp.where(live, g, 0.0)
        k = jnp.where(live, k, 0.0)

    tri = tri_ref[...]
    g1, g2, g3 = _split3(g)
    b = _dot(tri, g1) + _dot(tri, g2) + _dot(tri, g3)
    b_end = b[chunk - 1:chunk]
    hs = lambda h: slice(h * HEAD_SLAB, (h + 1) * HEAD_SLAB)
    b2 = b * LOG2_E
    qe = (q * jnp.exp(b)).astype(BF16)
    kd_f = k * jnp.exp(b_end - b)
    kd = kd_f.astype(BF16)
    kd_lo = (kd_f - kd.astype(F32)).astype(BF16)
    vb = v.astype(BF16)
    v_lo = (v - vb.astype(F32)).astype(BF16)

    o_inter = jnp.concatenate(
        [_dot_nt(qe[:, hs(h)], st_scr[row * nh + h].astype(BF16)) for h in range(nh)], axis=1)

    row_in_tile = lax.broadcasted_iota(jnp.int32, (SUBLANES, 1), 0)
    blocks = []
    for i in range(chunk // sub):
        lo = i * sub
        if lo >= valid:
            blocks.append(o_inter[lo:lo + sub])
            continue
        bi, qi_ = b[lo:lo + sub], q[lo:lo + sub]
        blk = o_inter[lo:lo + sub]
        if i > 0:
            r = b[lo - 1:lo]
            qt = (qi_ * jnp.exp(bi - r)).astype(BF16)
            kt = (k[:lo] * jnp.exp(r - b[:lo])).astype(BF16)
            off = []
            for h in range(nh):
                att = _dot_nt(qt[:, hs(h)], kt[:, hs(h)])
                off.append(_dot(att.astype(BF16), vb[:lo, hs(h)]))
            blk = blk + jnp.concatenate(off, axis=1)
        live_rows = min(sub, valid - lo)
        tiles = []
        for r0 in range(0, sub, SUBLANES):
            acc = jnp.zeros((SUBLANES, REC_WIDTH), F32)
            if r0 < live_rows:
                bt = b2[lo + r0:lo + r0 + SUBLANES]
                qt8 = qi_[r0:r0 + SUBLANES]
                rows8 = row_in_tile + r0
                for s in range(min(live_rows, r0 + SUBLANES)):
                    d = bt - b2[lo + s:lo + s + 1]
                    if s > r0:
                        d = jnp.minimum(d, 0.0)
                    w = qt8 * (k[lo + s:lo + s + 1] * jnp.exp2(d))
                    v_s = v[lo + s:lo + s + 1]
                    parts = []
                    for h in range(nh):
                        a_ts = jnp.sum(w[:, hs(h)], axis=-1, keepdims=True)
                        if s > r0:
                            a_ts = jnp.where(rows8 >= s, a_ts, 0.0)
                        parts.append(a_ts * v_s[:, hs(h)])
                    acc = acc + jnp.concatenate(parts, axis=1)
            tiles.append(acc)
        blocks.append(blk + jnp.concatenate(tiles, axis=0))
    o_ref[row] = jnp.concatenate(blocks, axis=0)

    decay = jnp.exp(b_end)
    for h in range(nh):
        upd = (_dot_tn(vb[:, hs(h)], kd[:, hs(h)]) + _dot_tn(vb[:, hs(h)], kd_lo[:, hs(h)])
               + _dot_tn(v_lo[:, hs(h)], kd[:, hs(h)]))
        st_scr[row * nh + h] = st_scr[row * nh + h] * decay[:, hs(h)] + upd


def _scan(z, batch, time, chunk, sub, valid, gla, extra, s0, bb, layer, stacked):
    k_dim = GLA_DK if gla else HG_DK
    nck = time // chunk
    nh = N_REC_HEADS
    z3 = z.reshape(batch, time, Z_WIDTH)
    cb = lambda col: pl.BlockSpec((bb, chunk, REC_WIDTH), lambda b, c: (b, c, col // REC_WIDTH))
    full = lambda a: pl.BlockSpec(a.shape, lambda b, c: (0,) * a.ndim)
    state = pl.BlockSpec((bb, nh, k_dim, HEAD_SLAB), lambda b, c: (b, 0, 0, 0))
    tri = jnp.tril(jnp.ones((chunk, chunk), F32)).astype(BF16)
    if gla:
        wg, bg = extra
        ins = [z3, z3, z3, z3, wg, bg, tri]
        specs = [cb(Z_GQ), cb(Z_GK), cb(Z_GV),
                 pl.BlockSpec((bb, chunk, LANE), lambda b, c: (b, c, Z_GLOW // LANE)), full(wg), full(bg), full(tri)]
    else:
        la, l1 = extra
        ins = [z3, z3, z3, la, l1, tri]
        specs = [cb(Z_HQ), cb(Z_HF), cb(Z_HI), full(la), full(l1), full(tri)]
    if s0 is not None:
        ins.append(s0)
        specs.append(state)
    ins.append(stacked)
    specs.append(pl.BlockSpec(memory_space=pl.ANY))
    o, stacked = pl.pallas_call(
        functools.partial(_scan_body, gla=gla, chunk=chunk, sub=sub, valid=valid, has_s0=s0 is not None,
                          k_dim=k_dim, bb=bb),
        grid=(batch // bb, nck),
        in_specs=specs,
        out_specs=[pl.BlockSpec((bb, chunk, REC_WIDTH), lambda b, c: (b, c, 0)),
                   pl.BlockSpec((1, bb, nh, k_dim, HEAD_SLAB), lambda b, c: (layer, b, 0, 0, 0))],
        out_shape=[jax.ShapeDtypeStruct((batch, time, REC_WIDTH), F32),
                   jax.ShapeDtypeStruct(stacked.shape, F32)],
        input_output_aliases={len(ins) - 1: 1},
        scratch_shapes=[pltpu.VMEM((bb * nh, HEAD_SLAB, HEAD_SLAB), F32)],
        compiler_params=_cparams(("parallel", "arbitrary")),
        name="scan_gla" if gla else "scan_hgrn",
    )(*ins)
    return o.reshape(batch * time, REC_WIDTH), stacked


def _merge_body(x_ref, ym_ref, oh_ref, og_ref, hgate_ref, ggate_ref, br0_ref, br1_ref, br2_ref,
                g1_ref, sc2_ref, sh2_ref, hgw_ref, glw_ref, nfw_ref, wbm_ref, wbh_ref, wbg_ref, wout_ref,
                rwh_ref, rwl_ref, rb_ref, tri_ref, cnt0_ref, x1_ref, h2_ref, te_ref, tw_ref, rk_ref, cnt_ref,
                carry_scr):
    silu = lambda t: t * jax.nn.sigmoid(t)
    yh = _rms(oh_ref[...]) * hgw_ref[...] * silu(hgate_ref[...])
    og = og_ref[...]
    glw = glw_ref[...]
    yg = jnp.concatenate([_rms(og[:, h * GLA_DV:(h + 1) * GLA_DV]) * glw for h in range(GLA_HEADS)], axis=1)
    yg = yg * silu(ggate_ref[...])
    m = (jax.nn.sigmoid(br0_ref[...]) * _dot(ym_ref[...].astype(BF16), wbm_ref[...])
         + jax.nn.sigmoid(br1_ref[...]) * _dot(yh.astype(BF16), wbh_ref[...])
         + jax.nn.sigmoid(br2_ref[...]) * _dot(yg.astype(BF16), wbg_ref[...]))
    x1 = x_ref[...] + g1_ref[0] * _dot(m.astype(BF16), wout_ref[...])
    x1_ref[...] = x1
    h2 = _rms(x1) * nfw_ref[...] * (1.0 + sc2_ref[0]) + sh2_ref[0]
    h2_ref[...] = h2
    hh = h2.astype(BF16)
    hl = (h2 - hh.astype(F32)).astype(BF16)
    rwh = rwh_ref[...]
    logits = _dot(hh, rwh) + _dot(hh, rwl_ref[...]) + _dot(hl, rwh) + rb_ref[...]

    lane = lax.broadcasted_iota(jnp.int32, logits.shape, 1).astype(F32)
    work = jnp.where(lane < N_EXPERTS, logits, NEG_INF)
    vals, idxs, hots = [], [], []
    for _ in range(TOP_K):
        mx = jnp.max(work, axis=1, keepdims=True)
        idx = jnp.min(jnp.where(work == mx, lane, float(LANE)), axis=1, keepdims=True)
        hot = lane == idx
        work = jnp.where(hot, NEG_INF, work)
        vals.append(mx)
        idxs.append(idx)
        hots.append(hot)
    ex = [jnp.exp(v - vals[0]) for v in vals]
    den = ex[0] + ex[1] + ex[2] + ex[3]

    @pl.when(pl.program_id(0) == 0)
    def _():
        carry_scr[...] = cnt0_ref[...]

    cnt = sum(h.astype(F32) for h in hots)
    before = _dot(tri_ref[...], cnt.astype(BF16)) + carry_scr[...]
    te = jnp.zeros(logits.shape, F32)
    tw = jnp.zeros(logits.shape, F32)
    rk = jnp.zeros(logits.shape, F32)
    for k in range(TOP_K):
        sel = lane == float(k)
        te = jnp.where(sel, idxs[k], te)
        tw = jnp.where(sel, ex[k] / den, tw)
        rk = jnp.where(sel, jnp.sum(jnp.where(hots[k], before, 0.0), axis=1, keepdims=True), rk)
    te_ref[...] = te.astype(jnp.int32)
    tw_ref[...] = tw
    rk_ref[...] = rk.astype(jnp.int32)
    carry_scr[...] = carry_scr[...] + jnp.sum(cnt, axis=0, keepdims=True)
    cnt_ref[...] = carry_scr[...]


def _merge(grp, x, y_mla, o_hg, o_gla, z, g1, sc2, sh2, lw, cnt0):
    tm = grp.tile(256)
    tri = jnp.tril(jnp.ones((tm, tm), F32), k=-1).astype(BF16)
    row = lambda w: pl.BlockSpec((tm, w), lambda i: (i, 0))
    zc = lambda col, w: pl.BlockSpec((tm, w), lambda i: (i, col // w))
    full = lambda a: pl.BlockSpec(a.shape, lambda i: (0,) * a.ndim)
    g1_a, g1_s = grp.mod(g1, tm)
    sc_a, sc_s = grp.mod(sc2, tm)
    sh_a, sh_s = grp.mod(sh2, tm)
    ws = [lw["hg_norm_w"], lw["gla_norm_w"], lw["norm_ffn_w"], lw["w_br_mla"], lw["w_br_hg"], lw["w_br_gla"],
          lw["w_out"], lw["rw_hi"], lw["rw_lo"], lw["rb"], tri, cnt0]
    n = grp.n_tok
    one = pl.BlockSpec((1, LANE), lambda i: (0, 0))
    return pl.pallas_call(
        _merge_body, grid=(grp.n_tok // tm,),
        in_specs=[row(D_MODEL), row(512), row(512), row(512), zc(Z_HGATE, 512), zc(Z_GGATE, 512),
                  zc(Z_BR, 1024), zc(Z_BR + 1024, 1024), zc(Z_BR + 2048, 1024),
                  g1_s, sc_s, sh_s] + [full(w) for w in ws],
        out_specs=[row(D_MODEL), row(D_MODEL), row(LANE), row(LANE), row(LANE), one],
        out_shape=[jax.ShapeDtypeStruct((n, D_MODEL), F32), jax.ShapeDtypeStruct((n, D_MODEL), F32),
                   jax.ShapeDtypeStruct((n, LANE), jnp.int32), jax.ShapeDtypeStruct((n, LANE), F32),
                   jax.ShapeDtypeStruct((n, LANE), jnp.int32), jax.ShapeDtypeStruct((1, LANE), F32)],
        scratch_shapes=[pltpu.VMEM((1, LANE), F32)],
        compiler_params=_cparams(("arbitrary",)), name="merge",
    )(x, y_mla, o_hg, o_gla, z, z, z, z, z, g1_a, sc_a, sh_a, *ws)


def _experts_body(be_ref, grp_ref, nxt_ref, nu_ref, x_ref, bgu_ref, bd_ref, wgu_hbm, wd_hbm, y_ref,
                  wgu_buf, wd_buf, wgu_bf, wd_bf, sem, *, layer):
    i = pl.program_id(0)
    e = be_ref[i]
    slot = grp_ref[i] % 2
    first = jnp.logical_or(i == 0, e != be_ref[jnp.maximum(i - 1, 0)])

    def fetch(expert, s):
        w = layer * N_EXPERTS + expert
        return (pltpu.make_async_copy(wgu_hbm.at[w], wgu_buf.at[s], sem.at[0, s]),
                pltpu.make_async_copy(wd_hbm.at[w], wd_buf.at[s], sem.at[1, s]))

    @pl.when(i == 0)
    def _():
        for c in fetch(e, 0):
            c.start()

    @pl.when(first)
    def _():
        for c in fetch(e, slot):
            c.wait()

        @pl.when(nxt_ref[i] >= 0)
        def _():
            for c in fetch(nxt_ref[i], 1 - slot):
                c.start()

        wgu_bf[...] = wgu_buf[slot].astype(BF16)
        wd_bf[...] = wd_buf[slot].astype(BF16)

    @pl.when(i < nu_ref[0])
    def _():
        gu = _dot(x_ref[...].astype(BF16), wgu_bf[...]) + bgu_ref[0]
        gate = jnp.minimum(gu[:, :D_EXPERT], SWIGLU_LIMIT)
        up = jnp.clip(gu[:, D_EXPERT:], -SWIGLU_LIMIT, SWIGLU_LIMIT)
        act = (up + 1.0) * gate * jax.nn.sigmoid(SWIGLU_ALPHA * gate)
        y_ref[...] = _dot(act.astype(BF16), wd_bf[...]) + bd_ref[0]

    @pl.when(i >= nu_ref[0])
    def _():
        y_ref[...] = jnp.zeros(y_ref.shape, F32)


def _experts(block_e, block_grp, block_nxt, n_used, xb, w_gate_up, b_gate_up, w_down, b_down, layer):
    n_slots = xb.shape[0]
    tm = MOE_TM
    ne = N_EXPERTS
    bias = lambda w: pl.BlockSpec((1, 1, w), lambda i, be, grp, nxt, nu: (layer * ne + be[i], 0, 0))
    return pl.pallas_call(
        functools.partial(_experts_body, layer=layer),
        grid_spec=pltpu.PrefetchScalarGridSpec(
            num_scalar_prefetch=4, grid=(n_slots // tm,),
            in_specs=[pl.BlockSpec((tm, D_MODEL), lambda i, be, grp, nxt, nu: (i, 0)),
                      bias(2 * D_EXPERT), bias(D_MODEL),
                      pl.BlockSpec(memory_space=pl.ANY), pl.BlockSpec(memory_space=pl.ANY)],
            out_specs=pl.BlockSpec((tm, D_MODEL), lambda i, be, grp, nxt, nu: (i, 0)),
            scratch_shapes=[pltpu.VMEM((2, D_MODEL, 2 * D_EXPERT), F32), pltpu.VMEM((2, D_EXPERT, D_MODEL), F32),
                            pltpu.VMEM((D_MODEL, 2 * D_EXPERT), BF16), pltpu.VMEM((D_EXPERT, D_MODEL), BF16),
                            pltpu.SemaphoreType.DMA((2, 2))]),
        out_shape=jax.ShapeDtypeStruct((n_slots, D_MODEL), F32),
        compiler_params=_cparams(("arbitrary",)),
        name="experts",
    )(block_e, block_grp, block_nxt, n_used, xb,
      b_gate_up.reshape(DEPTH * ne, 1, 2 * D_EXPERT), b_down.reshape(DEPTH * ne, 1, D_MODEL),
      w_gate_up.reshape(DEPTH * ne, D_MODEL, 2 * D_EXPERT), w_down.reshape(DEPTH * ne, D_EXPERT, D_MODEL))


def _route(top_e, rank, counts):
    n = top_e.shape[0]
    nk = n * TOP_K
    tm = MOE_TM
    experts = jnp.arange(N_EXPERTS, dtype=jnp.int32)
    padded = (counts + tm - 1) // tm * tm
    pad_end = jnp.cumsum(padded)
    pad_start = pad_end - padded
    start = jnp.cumsum(counts) - counts
    dest = jnp.sum(jnp.where(top_e[..., None] == experts, pad_start, 0), axis=-1) + rank
    flat_tok = jnp.arange(nk, dtype=jnp.int32) // TOP_K
    _, stok = lax.sort((dest.reshape(-1), flat_tok), num_keys=1)
    n_blocks = (nk + N_EXPERTS * (tm - 1) + tm - 1) // tm
    block_lo = jnp.arange(n_blocks, dtype=jnp.int32) * tm
    block_e = jnp.minimum(jnp.sum((pad_end[None, :] <= block_lo[:, None]).astype(jnp.int32), axis=1),
                          N_EXPERTS - 1)
    off = (block_lo - pad_start[block_e])[:, None] + jnp.arange(tm, dtype=jnp.int32)[None, :]
    live = off < counts[block_e][:, None]
    src = jnp.clip(start[block_e][:, None] + off, 0, nk - 1)
    slot_tok = jnp.where(live, stok[src.reshape(-1)].reshape(n_blocks, tm), 0).reshape(-1)
    n_used = (pad_end[-1] // tm).astype(jnp.int32).reshape(1)
    block_e = block_e.astype(jnp.int32)
    first = jnp.concatenate([jnp.ones((1,), jnp.bool_), block_e[1:] != block_e[:-1]])
    block_grp = jnp.cumsum(first.astype(jnp.int32)) - 1
    runs = jnp.arange(n_blocks, dtype=jnp.int32)
    run_e = jnp.sum(jnp.where(first[:, None] & (block_grp[:, None] == runs[None, :]), block_e[:, None], 0), axis=0)
    following = jnp.minimum(block_grp + 1, n_blocks - 1)
    block_nxt = jnp.where(block_grp + 1 <= block_grp[-1], run_e[following], -1).astype(jnp.int32)
    return slot_tok, (block_e, block_grp, block_nxt), n_used, dest


def _final_norm_body(x_ref, w_ref, o_ref):
    o_ref[...] = _rms(x_ref[...]) * w_ref[...]


def _final_norm(x, w, tm):
    n = x.shape[0]
    return pl.pallas_call(
        _final_norm_body, grid=(n // tm,),
        in_specs=[pl.BlockSpec((tm, D_MODEL), lambda i: (i, 0)), pl.BlockSpec((1, D_MODEL), lambda i: (0, 0))],
        out_specs=pl.BlockSpec((tm, D_MODEL), lambda i: (i, 0)),
        out_shape=jax.ShapeDtypeStruct((n, D_MODEL), F32),
        compiler_params=_cparams(("parallel",)), name="final_norm",
    )(x, w.reshape(1, D_MODEL))


def kernel(x_prompt, x_sample, cache_latent, cache_k_rope, state_hgrn, state_gla, page_table, c_prompt, c_sample, norm_mix_w, norm_ffn_w, final_norm_w, w_ada, b_ada, w_in, mla_q_norm_w, mla_w_uq, mla_kv_norm_w, mla_w_uk, mla_w_uv, hgrn_lower_bounds, hgrn_norm_w, gla_w_gate, gla_b_gate, gla_norm_w, w_branch_mla, w_branch_hgrn, w_branch_gla, w_out, router_w, router_b, w_gate_up, b_gate_up, w_down, b_down):
    bp, seq, _ = x_prompt.shape
    bs, tnew, _ = x_sample.shape
    n_p, n_s = bp * seq, bs * tnew
    past_len = page_table.shape[1] * cache_latent.shape[2]
    gp = _Group(bp, seq, per_token=False)
    gs = _Group(bs, tnew, per_token=True)
    s_pad = SCAN_SUB

    rows = bp + bs
    rows_pad = -(-rows // 8) * 8
    c_all = jnp.pad(jnp.concatenate([c_prompt, c_sample], axis=0), ((0, rows_pad - rows), (0, 0)))
    mod = _adaln(c_all, w_ada, b_ada)

    lbs = jax.nn.softmax(hgrn_lower_bounds.astype(F32), axis=0)
    lbs = jnp.cumsum(lbs, axis=0) - lbs[0]
    log_lb = jnp.log(jnp.maximum(lbs, TINY))
    log_1m_lb = jnp.log1p(-lbs)

    ct_p, st_p = _rope_tables(jnp.arange(seq, dtype=jnp.int32))
    pos_s = past_len + jnp.arange(tnew, dtype=jnp.int32)
    ct_s, st_s = _rope_tables(jnp.tile(pos_s, bs))

    xp = x_prompt.reshape(n_p, D_MODEL)
    xs = x_sample.reshape(n_s, D_MODEL)
    lat_p, kpe_p, lat_s, kpe_s = [], [], [], []
    state_buf = lambda b, k: jnp.zeros((DEPTH, b, N_REC_HEADS, k, HEAD_SLAB), F32)
    hg_p, gla_p, hg_s, gla_s = state_buf(bp, HG_DK), state_buf(bp, GLA_DK), state_buf(bs, HG_DK), state_buf(bs, GLA_DK)

    for l in range(DEPTH):
        w_pad = _pad_w_in(w_in[l])
        mw = _mla_weights(mla_w_uq[l], mla_w_uk[l], mla_w_uv[l])
        rw = jnp.pad(router_w[l], ((0, 0), (0, LANE - N_EXPERTS)))
        rw_hi = rw.astype(BF16)
        lw = dict(hg_norm_w=hgrn_norm_w[l].reshape(1, -1), gla_norm_w=gla_norm_w[l].reshape(1, -1),
                  norm_ffn_w=norm_ffn_w[l].reshape(1, -1), w_br_mla=w_branch_mla[l].astype(BF16),
                  w_br_hg=w_branch_hgrn[l].astype(BF16), w_br_gla=w_branch_gla[l].astype(BF16),
                  w_out=w_out[l].astype(BF16), rw_hi=rw_hi, rw_lo=(rw - rw_hi.astype(F32)).astype(BF16),
                  rb=jnp.pad(router_b[l], (0, LANE - N_EXPERTS)).reshape(1, LANE))
        wg = jnp.pad(gla_w_gate[l].reshape(GLA_GATE_RANK, GLA_HEADS, GLA_DK),
                     ((0, LANE - GLA_GATE_RANK), (0, 0), (0, HEAD_SLAB - GLA_DK))).reshape(LANE, REC_WIDTH).astype(BF16)
        bg = jnp.pad(gla_b_gate[l].reshape(GLA_HEADS, GLA_DK), ((0, 0), (0, HEAD_SLAB - GLA_DK))).reshape(1, REC_WIDTH)
        hg_extra = (log_lb[l].reshape(1, -1), log_1m_lb[l].reshape(1, -1))
        split6 = lambda m: jnp.split(m, 6, axis=-1)
        sh1p, sc1p, g1p, sh2p, sc2p, g2p = split6(mod[l, :bp])
        sh1s, sc1s, g1s, sh2s, sc2s, g2s = split6(mod[l, bp:rows])

        zp = _in_proj(gp, xp, norm_mix_w[l], sc1p, sh1p, w_pad)
        q, k, v, latp, kpep = _mla_prep(gp, zp, mla_q_norm_w[l], mla_kv_norm_w[l], mw, ct_p, st_p, sample=False)
        y_mla_p = _flash(q, k, v, bp, seq, tq=512)
        o_hg_p, hg_p = _scan(zp, bp, seq, SCAN_CHUNK, SCAN_SUB, SCAN_CHUNK, False, hg_extra, None, bp, l, hg_p)
        o_gl_p, gla_p = _scan(zp, bp, seq, SCAN_CHUNK, SCAN_SUB, SCAN_CHUNK, True, (wg, bg), None, bp, l, gla_p)
        x1p, h2p, te_p, tw_p, rk_p, cnt_p = _merge(gp, xp, y_mla_p, o_hg_p, o_gl_p, zp, g1p, sc2p, sh2p, lw,
                                                   jnp.zeros((1, LANE), F32))

        zs = _in_proj(gs, xs, norm_mix_w[l], sc1s, sh1s, w_pad)
        qs, qlat, lats, kpes = _mla_prep(gs, zs, mla_q_norm_w[l], mla_kv_norm_w[l], mw, ct_s, st_s, sample=True)
        o_lat = _paged(page_table, qs, qlat, lats, kpes, cache_latent, cache_k_rope, l, bs, tnew)
        y_mla_s = _head_proj(o_lat, mla_w_uv[l])
        zs_pad = jnp.pad(zs.reshape(bs, tnew, Z_WIDTH), ((0, 0), (0, s_pad - tnew), (0, 0))).reshape(bs * s_pad, Z_WIDTH)
        unpad = lambda o: o.reshape(bs, s_pad, -1)[:, :tnew].reshape(n_s, -1)
        o_hg_s, hg_s = _scan(zs_pad, bs, s_pad, s_pad, SCAN_SUB, tnew, False, hg_extra, state_hgrn[l],
                             SAMPLE_ROWS_PER_STEP, l, hg_s)
        o_gl_s, gla_s = _scan(zs_pad, bs, s_pad, s_pad, SCAN_SUB, tnew, True, (wg, bg), state_gla[l],
                              SAMPLE_ROWS_PER_STEP, l, gla_s)
        x1s, h2s, te_s, tw_s, rk_s, cnt_all = _merge(gs, xs, y_mla_s, unpad(o_hg_s), unpad(o_gl_s), zs, g1s, sc2s,
                                                     sh2s, lw, cnt_p)

        h2 = jnp.concatenate([h2p, h2s], axis=0)
        both = lambda a, b: jnp.concatenate([a, b], axis=0)[:, :TOP_K]
        top_w = both(tw_p, tw_s)
        slot_tok, block_e, n_used, dest = _route(both(te_p, te_s), both(rk_p, rk_s),
                                                 cnt_all[0, :N_EXPERTS].astype(jnp.int32))
        yb = _experts(*block_e, n_used, h2[slot_tok], w_gate_up, b_gate_up, w_down, b_down, l)
        combine = lambda w, d: sum(w[:, k:k + 1] * yb[d[:, k]] for k in range(TOP_K))
        xp = x1p + jnp.repeat(g2p, seq, axis=0) * combine(top_w[:n_p], dest[:n_p])
        xs = x1s + jnp.repeat(g2s, tnew, axis=0) * combine(top_w[n_p:], dest[n_p:])

        sl = slice(MLA_NOPE, MLA_NOPE + MLA_ROPE)
        lat_p.append(latp.reshape(bp, seq, -1)); kpe_p.append(kpep[:, sl].reshape(bp, seq, -1))
        lat_s.append(lats.reshape(bs, tnew, -1)); kpe_s.append(kpes[:, sl].reshape(bs, tnew, -1))

    y_prompt = _final_norm(xp, final_norm_w, 1024).reshape(bp, seq, D_MODEL)
    y_sample = _final_norm(xs, final_norm_w, n_s).reshape(bs, tnew, D_MODEL)
    st = jnp.stack
    return (y_prompt, y_sample, st(lat_p), st(kpe_p), hg_p, gla_p, st(lat_s), st(kpe_s), hg_s, gla_s)
```

```python
import functools

import numpy as np
import jax
import jax.numpy as jnp
from jax import lax
from jax.experimental import pallas as pl
from jax.experimental.pallas import tpu as pltpu

F32 = jnp.float32
BF16 = jnp.bfloat16

D_MODEL = 1024
DEPTH = 2
PAGE_SIZE = 128
MLA_HEADS = 8
MLA_NOPE = 64
MLA_ROPE = 32
MLA_V = 64
MLA_Q_LORA = 384
MLA_KV_LORA = 256
MLA_SCALE = (MLA_NOPE + MLA_ROPE) ** -0.5
ROPE_BASE = 10000.0
HG_HEADS = 4
HG_DK = 128
HG_DV = 128
GLA_HEADS = 4
GLA_DK = 64
GLA_DV = 128
GLA_GATE_RANK = 16
GLA_GATE_NORMALIZER = 16.0
N_EXPERTS = 32
TOP_K = 4
D_EXPERT = 1024
SWIGLU_LIMIT = 7.0
SWIGLU_ALPHA = 1.702
EPS = 1e-6
NEG_INF = -1e30
TINY = 1e-30
LOG2_E = 1.4426950408889634

IN_SPLITS = (MLA_Q_LORA, MLA_KV_LORA, MLA_ROPE, 512, 512, 512, 512, 256, 256, 512, 512, GLA_GATE_RANK,
             3 * D_MODEL)

LANE = 128
SUBLANES = 8
HEAD_SLAB = 128
VMEM_LIMIT = 56 * 1024 * 1024

Z_WIDTH = 8192
Z_CQ, Z_KR, Z_CKV, Z_KRR, Z_GLOW = 0, 384, 512, 768, 896
Z_HQ, Z_HF, Z_HI, Z_HGATE = 1024, 1536, 2048, 2560
Z_GQ, Z_GK, Z_GV, Z_GGATE, Z_BR = 3072, 3584, 4096, 4608, 5120

SCAN_CHUNK = 64
SCAN_SUB = 16
MOE_TM = 256
SAMPLE_ROWS_PER_STEP = 4
PAGED_ROWS_PER_STEP = 2


def _cparams(sem, vmem=VMEM_LIMIT):
    return pltpu.CompilerParams(dimension_semantics=sem, vmem_limit_bytes=vmem)


def _dot(a, b):
    return jnp.dot(a, b, preferred_element_type=F32)


def _dot_nt(a, b):
    return lax.dot_general(a, b, (((1,), (1,)), ((), ())), preferred_element_type=F32)


def _dot_tn(a, b):
    return lax.dot_general(a, b, (((0,), (0,)), ((), ())), preferred_element_type=F32)


def _rms(x):
    return x * lax.rsqrt(jnp.mean(x * x, axis=-1, keepdims=True) + EPS)


def _log_sigmoid(x):
    return jnp.minimum(x, 0.0) - jnp.log1p(jnp.exp(-jnp.abs(x)))


def _adaln_body(c_ref, w_ref, b_ref, o_ref):
    c = c_ref[...]
    a = (c * jax.nn.sigmoid(c)).astype(BF16)
    o_ref[0] = _dot(a, w_ref[0].astype(BF16)) + b_ref[0]


def _adaln(c_all, w_ada, b_ada):
    rows = c_all.shape[0]
    tn = 1536
    n_out = w_ada.shape[-1]
    return pl.pallas_call(
        _adaln_body,
        grid=(DEPTH, n_out // tn),
        in_specs=[pl.BlockSpec((rows, D_MODEL), lambda l, j: (0, 0)),
                  pl.BlockSpec((1, D_MODEL, tn), lambda l, j: (l, 0, j)),
                  pl.BlockSpec((1, 1, tn), lambda l, j: (l, 0, j))],
        out_specs=pl.BlockSpec((1, rows, tn), lambda l, j: (l, 0, j)),
        out_shape=jax.ShapeDtypeStruct((DEPTH, rows, n_out), F32),
        compiler_params=_cparams(("parallel", "parallel")),
        name="adaln",
    )(c_all, w_ada, b_ada.reshape(DEPTH, 1, n_out))


MOD_SHIFT1, MOD_SCALE1, MOD_GATE1, MOD_SHIFT2, MOD_SCALE2, MOD_GATE2 = range(6)


class _Group:
    def __init__(self, batch, time, per_token, mod):
        self.batch, self.time, self.per_token = batch, time, per_token
        self.n_tok = batch * time
        if per_token:
            self.mod = jnp.repeat(mod, time, axis=1)[:, None]
        else:
            self.mod = mod[:, :, None, :]

    def tile(self, tm):
        tm = min(tm, self.n_tok)
        assert (self.n_tok if self.per_token else self.time) % tm == 0
        return tm

    def mod_spec(self, layer, col, tm):
        if self.per_token:
            return pl.BlockSpec((1, 1, tm, D_MODEL), lambda *g: (layer, 0, g[0], col))
        per = self.time // tm
        return pl.BlockSpec((1, 1, 1, D_MODEL), lambda *g: (layer, g[0] // per, 0, col))


def _in_proj_body(x_ref, nw_ref, sc_ref, sh_ref, w_ref, z_ref, h_scr):
    @pl.when(pl.program_id(1) == 0)
    def _():
        h = _rms(x_ref[...]) * nw_ref[...]
        h_scr[...] = (h * (1.0 + sc_ref[0, 0]) + sh_ref[0, 0]).astype(BF16)

    z_ref[...] = _dot(h_scr[...], w_ref[...])


def _in_proj(grp, layer, x, norm_w, w_pad):
    tm, tn = grp.tile(1024), 1024
    return pl.pallas_call(
        _in_proj_body,
        grid=(grp.n_tok // tm, Z_WIDTH // tn),
        in_specs=[pl.BlockSpec((tm, D_MODEL), lambda i, j: (i, 0)),
                  pl.BlockSpec((1, D_MODEL), lambda i, j: (0, 0)),
                  grp.mod_spec(layer, MOD_SCALE1, tm), grp.mod_spec(layer, MOD_SHIFT1, tm),
                  pl.BlockSpec((D_MODEL, tn), lambda i, j: (0, j))],
        out_specs=pl.BlockSpec((tm, tn), lambda i, j: (i, j)),
        out_shape=jax.ShapeDtypeStruct((grp.n_tok, Z_WIDTH), F32),
        scratch_shapes=[pltpu.VMEM((tm, D_MODEL), BF16)],
        compiler_params=_cparams(("parallel", "arbitrary")),
        name="in_proj",
    )(x, norm_w.reshape(1, D_MODEL), grp.mod, grp.mod, w_pad)


def _pad_w_in(w):
    idx = np.cumsum(IN_SPLITS)[:-1].tolist()
    cq, ckv, kr, hq, hf, hi, hgate, gq, gk, gv, ggate, glow, br = jnp.split(w, idx, axis=1)
    zeros = lambda n: jnp.zeros((w.shape[0], n), w.dtype)
    half = MLA_ROPE // 2
    kr_rot = jnp.concatenate([-kr[:, half:], kr[:, :half]], axis=1)
    slab = lambda a: jnp.concatenate([zeros(MLA_NOPE), a, zeros(HEAD_SLAB - MLA_NOPE - MLA_ROPE)], axis=1)
    pad_heads = lambda a: jnp.pad(a.reshape(-1, GLA_HEADS, GLA_DK),
                                  ((0, 0), (0, 0), (0, HEAD_SLAB - GLA_DK))).reshape(-1, GLA_HEADS * HEAD_SLAB)
    glow_slab = jnp.concatenate([glow, zeros(LANE - GLA_GATE_RANK)], axis=1)
    out = jnp.concatenate([cq, slab(kr), ckv, slab(kr_rot), glow_slab, hq, hf, hi, hgate,
                           pad_heads(gq), pad_heads(gk), gv, ggate, br], axis=1)
    assert out.shape[1] == Z_WIDTH
    return out.astype(BF16)


def _mla_weights(w_uq, w_uk, w_uv):
    hd = MLA_NOPE + MLA_ROPE
    half = MLA_ROPE // 2
    q = w_uq.reshape(MLA_Q_LORA, MLA_HEADS, hd)
    nope, pe = q[..., :MLA_NOPE], q[..., MLA_NOPE:]
    pe_rot = jnp.concatenate([-pe[..., half:], pe[..., :half]], axis=-1)
    z = lambda n: jnp.zeros((MLA_Q_LORA, MLA_HEADS, n), w_uq.dtype)
    wa = jnp.concatenate([nope, pe, z(HEAD_SLAB - hd)], axis=-1).reshape(MLA_Q_LORA, -1)
    wb = jnp.concatenate([z(MLA_NOPE), pe_rot, z(HEAD_SLAB - hd)], axis=-1).reshape(MLA_Q_LORA, -1)
    k = w_uk.reshape(MLA_KV_LORA, MLA_HEADS, MLA_NOPE)
    wka = jnp.pad(k, ((0, 0), (0, 0), (0, HEAD_SLAB - MLA_NOPE))).reshape(MLA_KV_LORA, -1)
    wukt = jnp.pad(k.transpose(1, 2, 0), ((0, 0), (0, HEAD_SLAB - MLA_NOPE), (0, 0)))
    return wa.astype(BF16), wb.astype(BF16), wka.astype(BF16), w_uv.astype(BF16), wukt.astype(BF16)


def _rope_tables(pos):
    half = MLA_ROPE // 2
    inv = ROPE_BASE ** (-jnp.arange(half, dtype=F32) / half)
    ang = pos.astype(F32)[:, None] * inv[None, :]
    cos, sin = jnp.cos(ang), jnp.sin(ang)
    n = pos.shape[0]
    tail = jnp.zeros((n, HEAD_SLAB - MLA_NOPE - MLA_ROPE), F32)
    ct = jnp.concatenate([jnp.ones((n, MLA_NOPE), F32), cos, cos, tail], axis=1)
    st = jnp.concatenate([jnp.zeros((n, MLA_NOPE), F32), sin, sin, tail], axis=1)
    return ct, st


def _mla_common(z_ref, qnw_ref, kvnw_ref, wa_ref, wb_ref, ct_ref, st_ref):
    z = z_ref[...]
    qn = (_rms(z[:, Z_CQ:Z_CQ + MLA_Q_LORA]) * qnw_ref[...]).astype(BF16)
    lat = _rms(z[:, Z_CKV:Z_CKV + MLA_KV_LORA]) * kvnw_ref[...]
    ct, st = ct_ref[...], st_ref[...]
    ct8 = jnp.concatenate([ct] * MLA_HEADS, axis=1)
    st8 = jnp.concatenate([st] * MLA_HEADS, axis=1)
    q_cat = (_dot(qn, wa_ref[...]) * ct8 + _dot(qn, wb_ref[...]) * st8) * (MLA_SCALE * LOG2_E)
    kpe = z[:, Z_KR:Z_KR + HEAD_SLAB] * ct + z[:, Z_KRR:Z_KRR + HEAD_SLAB] * st
    return q_cat, lat, kpe


def _mla_prep_prompt_body(z_ref, qnw_ref, kvnw_ref, wa_ref, wb_ref, ct_ref, st_ref, wka_ref, wv_ref,
                          q_ref, k_ref, v_ref, lat_ref, kpe_ref):
    q_cat, lat, kpe = _mla_common(z_ref, qnw_ref, kvnw_ref, wa_ref, wb_ref, ct_ref, st_ref)
    q_ref[...] = q_cat.astype(BF16)
    lat_ref[...] = lat
    kpe_ref[...] = kpe
    lb = lat.astype(BF16)
    k_ref[...] = (_dot(lb, wka_ref[...]) + jnp.concatenate([kpe] * MLA_HEADS, axis=1)).astype(BF16)
    v_ref[...] = _dot(lb, wv_ref[...]).astype(BF16)


def _mla_prep_sample_body(z_ref, qnw_ref, kvnw_ref, wa_ref, wb_ref, ct_ref, st_ref, wukt_ref,
                          q_ref, qlat_ref, lat_ref, kpe_ref):
    q_cat, lat, kpe = _mla_common(z_ref, qnw_ref, kvnw_ref, wa_ref, wb_ref, ct_ref, st_ref)
    qb = q_cat.astype(BF16)
    q_ref[...] = qb
    lat_ref[...] = lat
    kpe_ref[...] = kpe
    for h in range(MLA_HEADS):
        qlat_ref[:, h * MLA_KV_LORA:(h + 1) * MLA_KV_LORA] = _dot(
            qb[:, h * HEAD_SLAB:(h + 1) * HEAD_SLAB], wukt_ref[h]).astype(BF16)


def _mla_prep(grp, z, q_norm_w, kv_norm_w, mw, ct, st, sample):
    wa, wb, wka, wv, wukt = mw
    tm = grp.tile(512)
    n_tiles = grp.n_tok // tm
    hw = MLA_HEADS * HEAD_SLAB
    full = lambda a: pl.BlockSpec(a.shape, lambda i: (0,) * a.ndim)
    row = lambda w: pl.BlockSpec((tm, w), lambda i: (i, 0))
    if grp.per_token:
        tab = pl.BlockSpec((tm, HEAD_SLAB), lambda i: (i, 0))
    else:
        per = grp.time // tm
        tab = pl.BlockSpec((tm, HEAD_SLAB), lambda i: (i % per, 0))
    qnw = q_norm_w.reshape(1, -1)
    kvnw = kv_norm_w.reshape(1, -1)
    common_in = [pl.BlockSpec((tm, 1024), lambda i: (i, 0)), full(qnw), full(kvnw), full(wa), full(wb), tab, tab]
    n = grp.n_tok
    if sample:
        return pl.pallas_call(
            _mla_prep_sample_body, grid=(n_tiles,),
            in_specs=common_in + [full(wukt)],
            out_specs=[row(hw), row(MLA_HEADS * MLA_KV_LORA), row(MLA_KV_LORA), row(HEAD_SLAB)],
            out_shape=[jax.ShapeDtypeStruct((n, hw), BF16),
                       jax.ShapeDtypeStruct((n, MLA_HEADS * MLA_KV_LORA), BF16),
                       jax.ShapeDtypeStruct((n, MLA_KV_LORA), F32),
                       jax.ShapeDtypeStruct((n, HEAD_SLAB), F32)],
            compiler_params=_cparams(("parallel",)), name="mla_prep_sample",
        )(z, qnw, kvnw, wa, wb, ct, st, wukt)
    return pl.pallas_call(
        _mla_prep_prompt_body, grid=(n_tiles,),
        in_specs=common_in + [full(wka), full(wv)],
        out_specs=[row(hw), row(hw), row(MLA_HEADS * MLA_V), row(MLA_KV_LORA), row(HEAD_SLAB)],
        out_shape=[jax.ShapeDtypeStruct((n, hw), BF16),
                   jax.ShapeDtypeStruct((n, hw), BF16),
                   jax.ShapeDtypeStruct((n, MLA_HEADS * MLA_V), BF16),
                   jax.ShapeDtypeStruct((n, MLA_KV_LORA), F32),
                   jax.ShapeDtypeStruct((n, HEAD_SLAB), F32)],
        compiler_params=_cparams(("parallel",)), name="mla_prep_prompt",
    )(z, qnw, kvnw, wa, wb, ct, st, wka, wv)


def _flash_body(q_ref, k_ref, v_ref, o_ref, m_scr, l_scr, acc_scr, *, tq):
    qi, ki = pl.program_id(1), pl.program_id(2)

    @pl.when(ki == 0)
    def _():
        m_scr[...] = jnp.full(m_scr.shape, NEG_INF, F32)
        l_scr[...] = jnp.zeros(l_scr.shape, F32)
        acc_scr[...] = jnp.zeros(acc_scr.shape, F32)

    low = lax.broadcasted_iota(jnp.int32, (tq, LANE), 1) < MLA_V

    def step(masked):
        if masked:
            keep = (lax.broadcasted_iota(jnp.int32, (tq, tq), 1)
                    <= lax.broadcasted_iota(jnp.int32, (tq, tq), 0))
        for hp in range(MLA_HEADS // 2):
            pv, al = [], []
            for e in range(2):
                h = 2 * hp + e
                s = _dot_nt(q_ref[0, :, h * HEAD_SLAB:(h + 1) * HEAD_SLAB],
                            k_ref[0, :, h * HEAD_SLAB:(h + 1) * HEAD_SLAB])
                if masked:
                    s = jnp.where(keep, s, NEG_INF)
                m_prev = m_scr[h]
                m_new = jnp.maximum(m_prev, jnp.max(s, axis=-1, keepdims=True))
                alpha = jnp.exp2(m_prev - m_new)
                p = jnp.exp2(s - jnp.concatenate([m_new] * (tq // LANE), axis=1))
                l_scr[h] = alpha * l_scr[h] + jnp.sum(p, axis=-1, keepdims=True)
                m_scr[h] = m_new
                pv.append(_dot(p.astype(BF16), v_ref[0, :, hp * LANE:(hp + 1) * LANE]))
                al.append(alpha)
            sl = slice(hp * LANE, (hp + 1) * LANE)
            acc_scr[:, sl] = jnp.where(low, al[0], al[1]) * acc_scr[:, sl] + jnp.where(low, pv[0], pv[1])

    @pl.when(ki < qi)
    def _():
        step(False)

    @pl.when(ki == qi)
    def _():
        step(True)

    @pl.when(ki == pl.num_programs(2) - 1)
    def _():
        for hp in range(MLA_HEADS // 2):
            sl = slice(hp * LANE, (hp + 1) * LANE)
            o_ref[0, :, sl] = acc_scr[:, sl] / jnp.where(low, l_scr[2 * hp], l_scr[2 * hp + 1])


def _flash(q, k, v, batch, seq, tq):
    hw = MLA_HEADS * HEAD_SLAB
    vw = MLA_HEADS * MLA_V
    nq = seq // tq
    q3, k3, v3 = q.reshape(batch, seq, hw), k.reshape(batch, seq, hw), v.reshape(batch, seq, vw)
    out = pl.pallas_call(
        functools.partial(_flash_body, tq=tq),
        grid=(batch, nq, nq),
        in_specs=[pl.BlockSpec((1, tq, hw), lambda b, i, j: (b, i, 0)),
                  pl.BlockSpec((1, tq, hw), lambda b, i, j: (b, jnp.minimum(i, j), 0)),
                  pl.BlockSpec((1, tq, vw), lambda b, i, j: (b, jnp.minimum(i, j), 0))],
        out_specs=pl.BlockSpec((1, tq, vw), lambda b, i, j: (b, i, 0)),
        out_shape=jax.ShapeDtypeStruct((batch, seq, vw), F32),
        scratch_shapes=[pltpu.VMEM((MLA_HEADS, tq, LANE), F32), pltpu.VMEM((MLA_HEADS, tq, LANE), F32),
                        pltpu.VMEM((tq, vw), F32)],
        compiler_params=_cparams(("parallel", "parallel", "arbitrary")),
        name="flash",
    )(q3, k3, v3)
    return out.reshape(batch * seq, vw)


def _paged_body(pt_ref, q_ref, qlat_ref, nlat_ref, nkpe_ref, lat_hbm, kpe_hbm, o_ref,
                lat_buf, kpe_buf, sem, *, layer, n_pages, n_new, rows_per_step):
    g = pl.program_id(0)
    ng = pl.num_programs(0)
    rows = q_ref.shape[1]
    rps = rows_per_step

    def copies(step, slot):
        out = []
        for r in range(rps):
            for j in range(n_pages):
                pg = pt_ref[step * rps + r, j]
                dst = pl.ds(j * PAGE_SIZE, PAGE_SIZE)
                out.append(pltpu.make_async_copy(lat_hbm.at[layer, pg], lat_buf.at[slot, r, dst], sem.at[0, slot]))
                out.append(pltpu.make_async_copy(kpe_hbm.at[layer, pg], kpe_buf.at[slot, r, :, dst], sem.at[1, slot]))
        return out

    @pl.when(g == 0)
    def _():
        for c in copies(0, 0):
            c.start()

    slot = g % 2

    @pl.when(g + 1 < ng)
    def _():
        for c in copies(g + 1, 1 - slot):
            c.start()

    for c in copies(g, slot):
        c.wait()

    for r in range(rps):
        qlat = qlat_ref[r].astype(F32)
        qpe = q_ref[r][:, MLA_NOPE:MLA_NOPE + MLA_ROPE].astype(F32)
        s_past = _dot_nt(qlat, lat_buf[slot, r]) + _dot(qpe, kpe_buf[slot, r])
        nlat = nlat_ref[r].astype(BF16)
        nkpe = nkpe_ref[r][:, MLA_NOPE:MLA_NOPE + MLA_ROPE].astype(BF16)
        s_new = _dot_nt(qlat_ref[r], nlat) + _dot_nt(q_ref[r][:, MLA_NOPE:MLA_NOPE + MLA_ROPE], nkpe)
        t_of_row = lax.broadcasted_iota(jnp.int32, (rows, n_new), 0) // MLA_HEADS
        s_new = jnp.where(lax.broadcasted_iota(jnp.int32, (rows, n_new), 1) <= t_of_row, s_new, NEG_INF)
        m = jnp.maximum(jnp.max(s_past, axis=-1, keepdims=True), jnp.max(s_new, axis=-1, keepdims=True))
        p_past = jnp.exp2(s_past - m)
        p_new = jnp.exp2(s_new - m)
        denom = jnp.sum(p_past, axis=-1, keepdims=True) + jnp.sum(p_new, axis=-1, keepdims=True)
        o = _dot(p_past.astype(BF16).astype(F32), lat_buf[slot, r]) + _dot(p_new.astype(BF16), nlat)
        o_ref[r] = o / denom


def _paged(page_table, q_cat, q_lat, lat_new, kpe_new, cache_latent, cache_k_rope, layer, batch, n_new):
    rows = n_new * MLA_HEADS
    n_pages = page_table.shape[1]
    past = n_pages * PAGE_SIZE
    rps = PAGED_ROWS_PER_STEP
    q3 = q_cat.reshape(batch, rows, HEAD_SLAB)
    ql3 = q_lat.reshape(batch, rows, MLA_KV_LORA)
    nl3 = lat_new.reshape(batch, n_new, MLA_KV_LORA)
    nk3 = kpe_new.reshape(batch, n_new, HEAD_SLAB)
    blk = lambda r, w: pl.BlockSpec((rps, r, w), lambda b, pt: (b, 0, 0))
    out = pl.pallas_call(
        functools.partial(_paged_body, layer=layer, n_pages=n_pages, n_new=n_new, rows_per_step=rps),
        grid_spec=pltpu.PrefetchScalarGridSpec(
            num_scalar_prefetch=1, grid=(batch // rps,),
            in_specs=[blk(rows, HEAD_SLAB), blk(rows, MLA_KV_LORA), blk(n_new, MLA_KV_LORA), blk(n_new, HEAD_SLAB),
                      pl.BlockSpec(memory_space=pl.ANY), pl.BlockSpec(memory_space=pl.ANY)],
            out_specs=blk(rows, MLA_KV_LORA),
            scratch_shapes=[pltpu.VMEM((2, rps, past, MLA_KV_LORA), F32), pltpu.VMEM((2, rps, MLA_ROPE, past), F32),
                            pltpu.SemaphoreType.DMA((2, 2))]),
        out_shape=jax.ShapeDtypeStruct((batch, rows, MLA_KV_LORA), F32),
        compiler_params=_cparams(("arbitrary",)),
        name="paged",
    )(page_table, q3, ql3, nl3, nk3, cache_latent, jnp.swapaxes(cache_k_rope, 2, 3))
    return out.reshape(batch * n_new, MLA_HEADS * MLA_KV_LORA)


def _head_proj_body(o_ref, w_ref, y_ref):
    for h in range(MLA_HEADS):
        y_ref[:, h * MLA_V:(h + 1) * MLA_V] = _dot(
            o_ref[:, h * MLA_KV_LORA:(h + 1) * MLA_KV_LORA].astype(BF16), w_ref[h])


def _head_proj(o_lat, w_uv):
    n = o_lat.shape[0]
    w = w_uv.reshape(MLA_KV_LORA, MLA_HEADS, MLA_V).transpose(1, 0, 2).astype(BF16)
    return pl.pallas_call(
        _head_proj_body, grid=(1,),
        in_specs=[pl.BlockSpec(o_lat.shape, lambda i: (0, 0)), pl.BlockSpec(w.shape, lambda i: (0, 0, 0))],
        out_specs=pl.BlockSpec((n, MLA_HEADS * MLA_V), lambda i: (0, 0)),
        out_shape=jax.ShapeDtypeStruct((n, MLA_HEADS * MLA_V), F32),
        compiler_params=_cparams(("arbitrary",)), name="head_proj",
    )(o_lat, w)


N_REC_HEADS = 4
REC_WIDTH = N_REC_HEADS * HEAD_SLAB


def _split3(x):
    a = x.astype(BF16)
    r = x - a.astype(F32)
    b = r.astype(BF16)
    c = (r - b.astype(F32)).astype(BF16)
    return a, b, c


def _scan_body(*refs, gla, chunk, sub, valid, has_s0, k_dim, bb):
    if gla:
        q_ref, k_ref, v_ref, glow_ref, wg_ref, bg_ref, tri_ref = refs[:7]
        rest = refs[7:]
    else:
        q_ref, k_ref, v_ref, la_ref, l1_ref, tri_ref = refs[:6]
        rest = refs[6:]
    if has_s0:
        s0_ref, _, o_ref, sfin_ref, st_scr = rest
    else:
        _, o_ref, sfin_ref, st_scr = rest
    ci = pl.program_id(1)
    nh = N_REC_HEADS

    @pl.when(ci == 0)
    def _():
        if has_s0:
            for i in range(bb):
                for h in range(nh):
                    s0 = s0_ref[i, h]
                    if k_dim < HEAD_SLAB:
                        s0 = jnp.concatenate([s0, jnp.zeros((HEAD_SLAB - k_dim, s0.shape[1]), F32)], axis=0)
                    st_scr[i * nh + h] = s0.T
        else:
            st_scr[...] = jnp.zeros(st_scr.shape, F32)

    for i in range(bb):
        _scan_chunk(i, q_ref, k_ref, v_ref, refs, o_ref, st_scr, tri_ref,
                    gla=gla, chunk=chunk, sub=sub, valid=valid)

    @pl.when(ci == pl.num_programs(1) - 1)
    def _():
        for i in range(bb):
            for h in range(nh):
                sfin_ref[0, i, h] = st_scr[i * nh + h].T[:k_dim]


def _scan_chunk(row, q_ref, k_ref, v_ref, refs, o_ref, st_scr, tri_ref, *, gla, chunk, sub, valid):
    nh = N_REC_HEADS
    v = v_ref[row]
    if gla:
        glow_ref, wg_ref, bg_ref = refs[3:6]
        q = q_ref[row] * (GLA_DK ** -0.5)
        k = k_ref[row]
        g = _log_sigmoid(_dot(glow_ref[row].astype(BF16), wg_ref[...]) + bg_ref[...]) * (1.0 / GLA_GATE_NORMALIZER)
    else:
        la_ref, l1_ref = refs[3:5]
        xq = q_ref[row]
        q = xq * jax.nn.sigmoid(xq) * (HG_DK ** -0.5)
        a = la_ref[...]
        bb = l1_ref[...] + _log_sigmoid(k_ref[row])
        g = jnp.maximum(a, bb) + jnp.log1p(jnp.exp(-jnp.abs(a - bb)))
        k = 1.0 - jnp.exp(g)
    if valid < chunk:
        live = lax.broadcasted_iota(jnp.int32, (chunk, 1), 0) < valid
        g = jnp.where(live, g, 0.0)
        k = jnp.where(live, k, 0.0)

    tri = tri_ref[...]
    g1, g2, g3 = _split3(g)
    b = _dot(tri, g1) + _dot(tri, g2) + _dot(tri, g3)
    b_end = b[chunk - 1:chunk]
    hs = lambda h: slice(h * HEAD_SLAB, (h + 1) * HEAD_SLAB)
    b2 = b * LOG2_E
    qe = (q * jnp.exp(b)).astype(BF16)
    kd_f = k * jnp.exp(b_end - b)
    kd = kd_f.astype(BF16)
    kd_lo = (kd_f - kd.astype(F32)).astype(BF16)
    vb = v.astype(BF16)
    v_lo = (v - vb.astype(F32)).astype(BF16)

    o_inter = jnp.concatenate(
        [_dot_nt(qe[:, hs(h)], st_scr[row * nh + h].astype(BF16)) for h in range(nh)], axis=1)

    row_in_tile = lax.broadcasted_iota(jnp.int32, (SUBLANES, 1), 0)
    blocks = []
    for i in range(chunk // sub):
        lo = i * sub
        if lo >= valid:
            blocks.append(o_inter[lo:lo + sub])
            continue
        bi, qi_ = b[lo:lo + sub], q[lo:lo + sub]
        blk = o_inter[lo:lo + sub]
        if i > 0:
            r = b[lo - 1:lo]
            qt = (qi_ * jnp.exp(bi - r)).astype(BF16)
            kt = (k[:lo] * jnp.exp(r - b[:lo])).astype(BF16)
            off = []
            for h in range(nh):
                att = _dot_nt(qt[:, hs(h)], kt[:, hs(h)])
                off.append(_dot(att.astype(BF16), vb[:lo, hs(h)]))
            blk = blk + jnp.concatenate(off, axis=1)
        live_rows = min(sub, valid - lo)
        tiles = []
        for r0 in range(0, sub, SUBLANES):
            acc = jnp.zeros((SUBLANES, REC_WIDTH), F32)
            if r0 < live_rows:
                bt = b2[lo + r0:lo + r0 + SUBLANES]
                qt8 = qi_[r0:r0 + SUBLANES]
                rows8 = row_in_tile + r0
                for s in range(min(live_rows, r0 + SUBLANES)):
                    d = bt - b2[lo + s:lo + s + 1]
                    if s > r0:
                        d = jnp.minimum(d, 0.0)
                    w = qt8 * (k[lo + s:lo + s + 1] * jnp.exp2(d))
                    v_s = v[lo + s:lo + s + 1]
                    parts = []
                    for h in range(nh):
                        a_ts = jnp.sum(w[:, hs(h)], axis=-1, keepdims=True)
                        if s > r0:
                            a_ts = jnp.where(rows8 >= s, a_ts, 0.0)
                        parts.append(a_ts * v_s[:, hs(h)])
                    acc = acc + jnp.concatenate(parts, axis=1)
            tiles.append(acc)
        blocks.append(blk + jnp.concatenate(tiles, axis=0))
    o_ref[row] = jnp.concatenate(blocks, axis=0)

    decay = jnp.exp(b_end)
    for h in range(nh):
        upd = (_dot_tn(vb[:, hs(h)], kd[:, hs(h)]) + _dot_tn(vb[:, hs(h)], kd_lo[:, hs(h)])
               + _dot_tn(v_lo[:, hs(h)], kd[:, hs(h)]))
        st_scr[row * nh + h] = st_scr[row * nh + h] * decay[:, hs(h)] + upd


def _scan(z, batch, time, chunk, sub, valid, gla, extra, s0, bb, layer, stacked):
    k_dim = GLA_DK if gla else HG_DK
    nck = time // chunk
    nh = N_REC_HEADS
    z3 = z.reshape(batch, time, Z_WIDTH)
    cb = lambda col: pl.BlockSpec((bb, chunk, REC_WIDTH), lambda b, c: (b, c, col // REC_WIDTH))
    full = lambda a: pl.BlockSpec(a.shape, lambda b, c: (0,) * a.ndim)
    state = pl.BlockSpec((bb, nh, k_dim, HEAD_SLAB), lambda b, c: (b, 0, 0, 0))
    tri = jnp.tril(jnp.ones((chunk, chunk), F32)).astype(BF16)
    if gla:
        wg, bg = extra
        ins = [z3, z3, z3, z3, wg, bg, tri]
        specs = [cb(Z_GQ), cb(Z_GK), cb(Z_GV),
                 pl.BlockSpec((bb, chunk, LANE), lambda b, c: (b, c, Z_GLOW // LANE)), full(wg), full(bg), full(tri)]
    else:
        la, l1 = extra
        ins = [z3, z3, z3, la, l1, tri]
        specs = [cb(Z_HQ), cb(Z_HF), cb(Z_HI), full(la), full(l1), full(tri)]
    if s0 is not None:
        ins.append(s0)
        specs.append(state)
    ins.append(stacked)
    specs.append(pl.BlockSpec(memory_space=pl.ANY))
    o, stacked = pl.pallas_call(
        functools.partial(_scan_body, gla=gla, chunk=chunk, sub=sub, valid=valid, has_s0=s0 is not None,
                          k_dim=k_dim, bb=bb),
        grid=(batch // bb, nck),
        in_specs=specs,
        out_specs=[pl.BlockSpec((bb, chunk, REC_WIDTH), lambda b, c: (b, c, 0)),
                   pl.BlockSpec((1, bb, nh, k_dim, HEAD_SLAB), lambda b, c: (layer, b, 0, 0, 0))],
        out_shape=[jax.ShapeDtypeStruct((batch, time, REC_WIDTH), F32),
                   jax.ShapeDtypeStruct(stacked.shape, F32)],
        input_output_aliases={len(ins) - 1: 1},
        scratch_shapes=[pltpu.VMEM((bb * nh, HEAD_SLAB, HEAD_SLAB), F32)],
        compiler_params=_cparams(("parallel", "arbitrary")),
        name="scan_gla" if gla else "scan_hgrn",
    )(*ins)
    return o.reshape(batch * time, REC_WIDTH), stacked


def _merge_body(x_ref, ym_ref, oh_ref, og_ref, hgate_ref, ggate_ref, br0_ref, br1_ref, br2_ref,
                g1_ref, sc2_ref, sh2_ref, hgw_ref, glw_ref, nfw_ref, wbm_ref, wbh_ref, wbg_ref, wout_ref,
                rwh_ref, rwl_ref, rb_ref, tri_ref, cnt0_ref, x1_ref, h2_ref, te_ref, tw_ref, rk_ref, cnt_ref,
                carry_scr):
    silu = lambda t: t * jax.nn.sigmoid(t)
    yh = _rms(oh_ref[...]) * hgw_ref[...] * silu(hgate_ref[...])
    og = og_ref[...]
    glw = glw_ref[...]
    yg = jnp.concatenate([_rms(og[:, h * GLA_DV:(h + 1) * GLA_DV]) * glw for h in range(GLA_HEADS)], axis=1)
    yg = yg * silu(ggate_ref[...])
    m = (jax.nn.sigmoid(br0_ref[...]) * _dot(ym_ref[...].astype(BF16), wbm_ref[...])
         + jax.nn.sigmoid(br1_ref[...]) * _dot(yh.astype(BF16), wbh_ref[...])
         + jax.nn.sigmoid(br2_ref[...]) * _dot(yg.astype(BF16), wbg_ref[...]))
    x1 = x_ref[...] + g1_ref[0, 0] * _dot(m.astype(BF16), wout_ref[...])
    x1_ref[...] = x1
    h2 = _rms(x1) * nfw_ref[...] * (1.0 + sc2_ref[0, 0]) + sh2_ref[0, 0]
    hh = h2.astype(BF16)
    half = D_MODEL // 2
    bits = lambda t: lax.bitcast_convert_type(t.astype(F32), jnp.uint32)
    h2_ref[...] = bits(hh[:, :half]) | (bits(hh[:, half:]) >> 16)
    hl = (h2 - hh.astype(F32)).astype(BF16)
    rwh = rwh_ref[...]
    logits = _dot(hh, rwh) + _dot(hh, rwl_ref[...]) + _dot(hl, rwh) + rb_ref[...]

    lane = lax.broadcasted_iota(jnp.int32, logits.shape, 1).astype(F32)
    work = jnp.where(lane < N_EXPERTS, logits, NEG_INF)
    vals, idxs, hots = [], [], []
    for _ in range(TOP_K):
        mx = jnp.max(work, axis=1, keepdims=True)
        idx = jnp.min(jnp.where(work == mx, lane, float(LANE)), axis=1, keepdims=True)
        hot = lane == idx
        work = jnp.where(hot, NEG_INF, work)
        vals.append(mx)
        idxs.append(idx)
        hots.append(hot)
    ex = [jnp.exp(v - vals[0]) for v in vals]
    den = ex[0] + ex[1] + ex[2] + ex[3]

    @pl.when(pl.program_id(0) == 0)
    def _():
        carry_scr[...] = cnt0_ref[...]

    cnt = sum(h.astype(F32) for h in hots)
    before = _dot(tri_ref[...], cnt.astype(BF16)) + carry_scr[...]
    te = jnp.zeros(logits.shape, F32)
    tw = jnp.zeros(logits.shape, F32)
    rk = jnp.zeros(logits.shape, F32)
    for k in range(TOP_K):
        sel = lane == float(k)
        te = jnp.where(sel, idxs[k], te)
        tw = jnp.where(sel, ex[k] / den, tw)
        rk = jnp.where(sel, jnp.sum(jnp.where(hots[k], before, 0.0), axis=1, keepdims=True), rk)
    te_ref[...] = te.astype(jnp.int32)
    tw_ref[...] = tw
    rk_ref[...] = rk.astype(jnp.int32)
    carry_scr[...] = carry_scr[...] + jnp.sum(cnt, axis=0, keepdims=True)
    cnt_ref[...] = carry_scr[...]


def _merge(grp, layer, x, y_mla, o_hg, o_gla, z, lw, cnt0):
    tm = grp.tile(256)
    tri = jnp.tril(jnp.ones((tm, tm), F32), k=-1).astype(BF16)
    row = lambda w: pl.BlockSpec((tm, w), lambda i: (i, 0))
    zc = lambda col, w: pl.BlockSpec((tm, w), lambda i: (i, col // w))
    full = lambda a: pl.BlockSpec(a.shape, lambda i: (0,) * a.ndim)
    ws = [lw["hg_norm_w"], lw["gla_norm_w"], lw["norm_ffn_w"], lw["w_br_mla"], lw["w_br_hg"], lw["w_br_gla"],
          lw["w_out"], lw["rw_hi"], lw["rw_lo"], lw["rb"], tri, cnt0]
    n = grp.n_tok
    one = pl.BlockSpec((1, LANE), lambda i: (0, 0))
    return pl.pallas_call(
        _merge_body, grid=(grp.n_tok // tm,),
        in_specs=[row(D_MODEL), row(512), row(512), row(512), zc(Z_HGATE, 512), zc(Z_GGATE, 512),
                  zc(Z_BR, 1024), zc(Z_BR + 1024, 1024), zc(Z_BR + 2048, 1024),
                  grp.mod_spec(layer, MOD_GATE1, tm), grp.mod_spec(layer, MOD_SCALE2, tm),
                  grp.mod_spec(layer, MOD_SHIFT2, tm)] + [full(w) for w in ws],
        out_specs=[row(D_MODEL), row(D_MODEL // 2), row(LANE), row(LANE), row(LANE), one],
        out_shape=[jax.ShapeDtypeStruct((n, D_MODEL), F32), jax.ShapeDtypeStruct((n, D_MODEL // 2), jnp.uint32),
                   jax.ShapeDtypeStruct((n, LANE), jnp.int32), jax.ShapeDtypeStruct((n, LANE), F32),
                   jax.ShapeDtypeStruct((n, LANE), jnp.int32), jax.ShapeDtypeStruct((1, LANE), F32)],
        scratch_shapes=[pltpu.VMEM((1, LANE), F32)],
        compiler_params=_cparams(("arbitrary",)), name="merge",
    )(x, y_mla, o_hg, o_gla, z, z, z, z, z, grp.mod, grp.mod, grp.mod, *ws)


def _experts_body(be_ref, grp_ref, nxt_ref, nu_ref, x_ref, bgu_ref, bd_ref, wgu_hbm, wd_hbm, y_ref,
                  wgu_buf, wd_buf, wgu_bf, wd_bf, sem, *, layer):
    i = pl.program_id(0)
    e = be_ref[i]
    slot = grp_ref[i] % 2
    first = jnp.logical_or(i == 0, e != be_ref[jnp.maximum(i - 1, 0)])

    def fetch(expert, s):
        w = layer * N_EXPERTS + expert
        return (pltpu.make_async_copy(wgu_hbm.at[w], wgu_buf.at[s], sem.at[0, s]),
                pltpu.make_async_copy(wd_hbm.at[w], wd_buf.at[s], sem.at[1, s]))

    @pl.when(i == 0)
    def _():
        for c in fetch(e, 0):
            c.start()

    @pl.when(first)
    def _():
        for c in fetch(e, slot):
            c.wait()

        @pl.when(nxt_ref[i] >= 0)
        def _():
            for c in fetch(nxt_ref[i], 1 - slot):
                c.start()

        wgu_bf[...] = wgu_buf[slot].astype(BF16)
        wd_bf[...] = wd_buf[slot].astype(BF16)

    @pl.when(i < nu_ref[0])
    def _():
        half = D_MODEL // 2
        xw = x_ref[...]
        x_lo = lax.bitcast_convert_type(xw & jnp.uint32(0xFFFF0000), F32).astype(BF16)
        x_hi = lax.bitcast_convert_type(xw << 16, F32).astype(BF16)
        gu = _dot(x_lo, wgu_bf[:half, :]) + _dot(x_hi, wgu_bf[half:, :]) + bgu_ref[0]
        gate = jnp.minimum(gu[:, :D_EXPERT], SWIGLU_LIMIT)
        up = jnp.clip(gu[:, D_EXPERT:], -SWIGLU_LIMIT, SWIGLU_LIMIT)
        act = (up + 1.0) * gate * jax.nn.sigmoid(SWIGLU_ALPHA * gate)
        y_ref[...] = _dot(act.astype(BF16), wd_bf[...]) + bd_ref[0]

    @pl.when(i >= nu_ref[0])
    def _():
        y_ref[...] = jnp.zeros(y_ref.shape, F32)


def _experts(block_e, block_grp, block_nxt, n_used, xb, w_gate_up, b_gate_up, w_down, b_down, layer):
    n_slots = xb.shape[0]
    tm = MOE_TM
    ne = N_EXPERTS
    bias = lambda w: pl.BlockSpec((1, 1, w), lambda i, be, grp, nxt, nu: (layer * ne + be[i], 0, 0))
    return pl.pallas_call(
        functools.partial(_experts_body, layer=layer),
        grid_spec=pltpu.PrefetchScalarGridSpec(
            num_scalar_prefetch=4, grid=(n_slots // tm,),
            in_specs=[pl.BlockSpec((tm, D_MODEL // 2), lambda i, be, grp, nxt, nu: (i, 0)),
                      bias(2 * D_EXPERT), bias(D_MODEL),
                      pl.BlockSpec(memory_space=pl.ANY), pl.BlockSpec(memory_space=pl.ANY)],
            out_specs=pl.BlockSpec((tm, D_MODEL), lambda i, be, grp, nxt, nu: (i, 0)),
            scratch_shapes=[pltpu.VMEM((2, D_MODEL, 2 * D_EXPERT), F32), pltpu.VMEM((2, D_EXPERT, D_MODEL), F32),
                            pltpu.VMEM((D_MODEL, 2 * D_EXPERT), BF16), pltpu.VMEM((D_EXPERT, D_MODEL), BF16),
                            pltpu.SemaphoreType.DMA((2, 2))]),
        out_shape=jax.ShapeDtypeStruct((n_slots, D_MODEL), F32),
        compiler_params=_cparams(("arbitrary",)),
        name="experts",
    )(block_e, block_grp, block_nxt, n_used, xb,
      b_gate_up.reshape(DEPTH * ne, 1, 2 * D_EXPERT), b_down.reshape(DEPTH * ne, 1, D_MODEL),
      w_gate_up.reshape(DEPTH * ne, D_MODEL, 2 * D_EXPERT), w_down.reshape(DEPTH * ne, D_EXPERT, D_MODEL))


def _route(top_e, rank, counts):
    n = top_e.shape[0]
    nk = n * TOP_K
    tm = MOE_TM
    experts = jnp.arange(N_EXPERTS, dtype=jnp.int32)
    padded = (counts + tm - 1) // tm * tm
    pad_end = jnp.cumsum(padded)
    pad_start = pad_end - padded
    start = jnp.cumsum(counts) - counts
    dest = jnp.sum(jnp.where(top_e[..., None] == experts, pad_start, 0), axis=-1) + rank
    flat_tok = jnp.arange(nk, dtype=jnp.int32) // TOP_K
    _, stok = lax.sort((dest.reshape(-1), flat_tok), num_keys=1)
    n_blocks = (nk + N_EXPERTS * (tm - 1) + tm - 1) // tm
    block_lo = jnp.arange(n_blocks, dtype=jnp.int32) * tm
    block_e = jnp.minimum(jnp.sum((pad_end[None, :] <= block_lo[:, None]).astype(jnp.int32), axis=1),
                          N_EXPERTS - 1)
    off = (block_lo - pad_start[block_e])[:, None] + jnp.arange(tm, dtype=jnp.int32)[None, :]
    live = off < counts[block_e][:, None]
    src = jnp.clip(start[block_e][:, None] + off, 0, nk - 1)
    slot_tok = jnp.where(live, stok[src.reshape(-1)].reshape(n_blocks, tm), 0).reshape(-1)
    n_used = (pad_end[-1] // tm).astype(jnp.int32).reshape(1)
    block_e = block_e.astype(jnp.int32)
    first = jnp.concatenate([jnp.ones((1,), jnp.bool_), block_e[1:] != block_e[:-1]])
    block_grp = jnp.cumsum(first.astype(jnp.int32)) - 1
    runs = jnp.arange(n_blocks, dtype=jnp.int32)
    run_e = jnp.sum(jnp.where(first[:, None] & (block_grp[:, None] == runs[None, :]), block_e[:, None], 0), axis=0)
    following = jnp.minimum(block_grp + 1, n_blocks - 1)
    block_nxt = jnp.where(block_grp + 1 <= block_grp[-1], run_e[following], -1).astype(jnp.int32)
    return slot_tok, (block_e, block_grp, block_nxt), n_used, dest


def _final_norm_body(x_ref, w_ref, o_ref):
    o_ref[...] = _rms(x_ref[...]) * w_ref[...]


def _final_norm(x, w, tm):
    n = x.shape[0]
    return pl.pallas_call(
        _final_norm_body, grid=(n // tm,),
        in_specs=[pl.BlockSpec((tm, D_MODEL), lambda i: (i, 0)), pl.BlockSpec((1, D_MODEL), lambda i: (0, 0))],
        out_specs=pl.BlockSpec((tm, D_MODEL), lambda i: (i, 0)),
        out_shape=jax.ShapeDtypeStruct((n, D_MODEL), F32),
        compiler_params=_cparams(("parallel",)), name="final_norm",
    )(x, w.reshape(1, D_MODEL))


def kernel(x_prompt, x_sample, cache_latent, cache_k_rope, state_hgrn, state_gla, page_table, c_prompt, c_sample, norm_mix_w, norm_ffn_w, final_norm_w, w_ada, b_ada, w_in, mla_q_norm_w, mla_w_uq, mla_kv_norm_w, mla_w_uk, mla_w_uv, hgrn_lower_bounds, hgrn_norm_w, gla_w_gate, gla_b_gate, gla_norm_w, w_branch_mla, w_branch_hgrn, w_branch_gla, w_out, router_w, router_b, w_gate_up, b_gate_up, w_down, b_down):
    bp, seq, _ = x_prompt.shape
    bs, tnew, _ = x_sample.shape
    n_p, n_s = bp * seq, bs * tnew
    past_len = page_table.shape[1] * cache_latent.shape[2]
    s_pad = SCAN_SUB

    rows = bp + bs
    rows_pad = -(-rows // 8) * 8
    c_all = jnp.pad(jnp.concatenate([c_prompt, c_sample], axis=0), ((0, rows_pad - rows), (0, 0)))
    mod = _adaln(c_all, w_ada, b_ada)
    gp = _Group(bp, seq, per_token=False, mod=mod[:, :bp])
    gs = _Group(bs, tnew, per_token=True, mod=mod[:, bp:rows])

    lbs = jax.nn.softmax(hgrn_lower_bounds.astype(F32), axis=0)
    lbs = jnp.cumsum(lbs, axis=0) - lbs[0]
    log_lb = jnp.log(jnp.maximum(lbs, TINY))
    log_1m_lb = jnp.log1p(-lbs)

    ct_p, st_p = _rope_tables(jnp.arange(seq, dtype=jnp.int32))
    pos_s = past_len + jnp.arange(tnew, dtype=jnp.int32)
    ct_s, st_s = _rope_tables(jnp.tile(pos_s, bs))

    xp = x_prompt.reshape(n_p, D_MODEL)
    xs = x_sample.reshape(n_s, D_MODEL)
    lat_p, kpe_p, lat_s, kpe_s = [], [], [], []
    state_buf = lambda b, k: jnp.zeros((DEPTH, b, N_REC_HEADS, k, HEAD_SLAB), F32)
    hg_p, gla_p, hg_s, gla_s = state_buf(bp, HG_DK), state_buf(bp, GLA_DK), state_buf(bs, HG_DK), state_buf(bs, GLA_DK)

    for l in range(DEPTH):
        w_pad = _pad_w_in(w_in[l])
        mw = _mla_weights(mla_w_uq[l], mla_w_uk[l], mla_w_uv[l])
        rw = jnp.pad(router_w[l], ((0, 0), (0, LANE - N_EXPERTS)))
        rw_hi = rw.astype(BF16)
        lw = dict(hg_norm_w=hgrn_norm_w[l].reshape(1, -1), gla_norm_w=gla_norm_w[l].reshape(1, -1),
                  norm_ffn_w=norm_ffn_w[l].reshape(1, -1), w_br_mla=w_branch_mla[l].astype(BF16),
                  w_br_hg=w_branch_hgrn[l].astype(BF16), w_br_gla=w_branch_gla[l].astype(BF16),
                  w_out=w_out[l].astype(BF16), rw_hi=rw_hi, rw_lo=(rw - rw_hi.astype(F32)).astype(BF16),
                  rb=jnp.pad(router_b[l], (0, LANE - N_EXPERTS)).reshape(1, LANE))
        wg = jnp.pad(gla_w_gate[l].reshape(GLA_GATE_RANK, GLA_HEADS, GLA_DK),
                     ((0, LANE - GLA_GATE_RANK), (0, 0), (0, HEAD_SLAB - GLA_DK))).reshape(LANE, REC_WIDTH).astype(BF16)
        bg = jnp.pad(gla_b_gate[l].reshape(GLA_HEADS, GLA_DK), ((0, 0), (0, HEAD_SLAB - GLA_DK))).reshape(1, REC_WIDTH)
        hg_extra = (log_lb[l].reshape(1, -1), log_1m_lb[l].reshape(1, -1))
        g2p = mod[l, :bp, MOD_GATE2 * D_MODEL:]
        g2s = mod[l, bp:rows, MOD_GATE2 * D_MODEL:]

        zp = _in_proj(gp, l, xp, norm_mix_w[l], w_pad)
        q, k, v, latp, kpep = _mla_prep(gp, zp, mla_q_norm_w[l], mla_kv_norm_w[l], mw, ct_p, st_p, sample=False)
        y_mla_p = _flash(q, k, v, bp, seq, tq=512)
        o_hg_p, hg_p = _scan(zp, bp, seq, SCAN_CHUNK, SCAN_SUB, SCAN_CHUNK, False, hg_extra, None, bp, l, hg_p)
        o_gl_p, gla_p = _scan(zp, bp, seq, SCAN_CHUNK, SCAN_SUB, SCAN_CHUNK, True, (wg, bg), None, bp, l, gla_p)
        x1p, h2p, te_p, tw_p, rk_p, cnt_p = _merge(gp, l, xp, y_mla_p, o_hg_p, o_gl_p, zp, lw,
                                                   jnp.zeros((1, LANE), F32))

        zs = _in_proj(gs, l, xs, norm_mix_w[l], w_pad)
        qs, qlat, lats, kpes = _mla_prep(gs, zs, mla_q_norm_w[l], mla_kv_norm_w[l], mw, ct_s, st_s, sample=True)
        o_lat = _paged(page_table, qs, qlat, lats, kpes, cache_latent, cache_k_rope, l, bs, tnew)
        y_mla_s = _head_proj(o_lat, mla_w_uv[l])
        zs_pad = jnp.pad(zs.reshape(bs, tnew, Z_WIDTH), ((0, 0), (0, s_pad - tnew), (0, 0))).reshape(bs * s_pad, Z_WIDTH)
        unpad = lambda o: o.reshape(bs, s_pad, -1)[:, :tnew].reshape(n_s, -1)
        o_hg_s, hg_s = _scan(zs_pad, bs, s_pad, s_pad, SCAN_SUB, tnew, False, hg_extra, state_hgrn[l],
                             SAMPLE_ROWS_PER_STEP, l, hg_s)
        o_gl_s, gla_s = _scan(zs_pad, bs, s_pad, s_pad, SCAN_SUB, tnew, True, (wg, bg), state_gla[l],
                              SAMPLE_ROWS_PER_STEP, l, gla_s)
        x1s, h2s, te_s, tw_s, rk_s, cnt_all = _merge(gs, l, xs, y_mla_s, unpad(o_hg_s), unpad(o_gl_s), zs, lw, cnt_p)

        h2 = jnp.concatenate([h2p, h2s], axis=0)
        both = lambda a, b: jnp.concatenate([a, b], axis=0)[:, :TOP_K]
        top_w = both(tw_p, tw_s)
        slot_tok, block_e, n_used, dest = _route(both(te_p, te_s), both(rk_p, rk_s),
                                                 cnt_all[0, :N_EXPERTS].astype(jnp.int32))
        yb = _experts(*block_e, n_used, h2[slot_tok], w_gate_up, b_gate_up, w_down, b_down, l)
        combine = lambda w, d: sum(w[:, k:k + 1] * yb[d[:, k]] for k in range(TOP_K))
        xp = x1p + jnp.repeat(g2p, seq, axis=0) * combine(top_w[:n_p], dest[:n_p])
        xs = x1s + jnp.repeat(g2s, tnew, axis=0) * combine(top_w[n_p:], dest[n_p:])

        sl = slice(MLA_NOPE, MLA_NOPE + MLA_ROPE)
        lat_p.append(latp.reshape(bp, seq, -1)); kpe_p.append(kpep[:, sl].reshape(bp, seq, -1))
        lat_s.append(lats.reshape(bs, tnew, -1)); kpe_s.append(kpes[:, sl].reshape(bs, tnew, -1))

    y_prompt = _final_norm(xp, final_norm_w, 1024).reshape(bp, seq, D_MODEL)
    y_sample = _final_norm(xs, final_norm_w, n_s).reshape(bs, tnew, D_MODEL)
    st = jnp.stack
    return (y_prompt, y_sample, st(lat_p), st(kpe_p), hg_p, gla_p, st(lat_s), st(kpe_s), hg_s, gla_s)
```

```python
import functools

import numpy as np
import jax
import jax.numpy as jnp
from jax import lax
from jax.experimental import pallas as pl
from jax.experimental.pallas import tpu as pltpu

F32 = jnp.float32
BF16 = jnp.bfloat16

D_MODEL = 1024
DEPTH = 2
PAGE_SIZE = 128
MLA_HEADS = 8
MLA_NOPE = 64
MLA_ROPE = 32
MLA_V = 64
MLA_Q_LORA = 384
MLA_KV_LORA = 256
MLA_SCALE = (MLA_NOPE + MLA_ROPE) ** -0.5
ROPE_BASE = 10000.0
HG_HEADS = 4
HG_DK = 128
HG_DV = 128
GLA_HEADS = 4
GLA_DK = 64
GLA_DV = 128
GLA_GATE_RANK = 16
GLA_GATE_NORMALIZER = 16.0
N_EXPERTS = 32
TOP_K = 4
D_EXPERT = 1024
SWIGLU_LIMIT = 7.0
SWIGLU_ALPHA = 1.702
EPS = 1e-6
NEG_INF = -1e30
TINY = 1e-30
LOG2_E = 1.4426950408889634

IN_SPLITS = (MLA_Q_LORA, MLA_KV_LORA, MLA_ROPE, 512, 512, 512, 512, 256, 256, 512, 512, GLA_GATE_RANK,
             3 * D_MODEL)

LANE = 128
SUBLANES = 8
HEAD_SLAB = 128
VMEM_LIMIT = 56 * 1024 * 1024

Z_WIDTH = 8192
Z_CQ, Z_KR, Z_CKV, Z_KRR, Z_GLOW = 0, 384, 512, 768, 896
Z_HQ, Z_HF, Z_HI, Z_HGATE = 1024, 1536, 2048, 2560
Z_GQ, Z_GK, Z_GV, Z_GGATE, Z_BR = 3072, 3584, 4096, 4608, 5120

SCAN_CHUNK = 64
SCAN_SUB = 16
MOE_TM = 256
SAMPLE_ROWS_PER_STEP = 4
PAGED_ROWS_PER_STEP = 2


def _cparams(sem, vmem=VMEM_LIMIT):
    return pltpu.CompilerParams(dimension_semantics=sem, vmem_limit_bytes=vmem)


def _dot(a, b):
    return jnp.dot(a, b, preferred_element_type=F32)


def _dot_nt(a, b):
    return lax.dot_general(a, b, (((1,), (1,)), ((), ())), preferred_element_type=F32)


def _dot_tn(a, b):
    return lax.dot_general(a, b, (((0,), (0,)), ((), ())), preferred_element_type=F32)


def _rms(x):
    return x * lax.rsqrt(jnp.mean(x * x, axis=-1, keepdims=True) + EPS)


def _log_sigmoid(x):
    return jnp.minimum(x, 0.0) - jnp.log1p(jnp.exp(-jnp.abs(x)))


def _adaln_body(c_ref, w_ref, b_ref, o_ref):
    c = c_ref[...]
    a = (c * jax.nn.sigmoid(c)).astype(BF16)
    o_ref[0] = _dot(a, w_ref[0].astype(BF16)) + b_ref[0]


def _adaln(c_all, w_ada, b_ada):
    rows = c_all.shape[0]
    tn = 1536
    n_out = w_ada.shape[-1]
    return pl.pallas_call(
        _adaln_body,
        grid=(DEPTH, n_out // tn),
        in_specs=[pl.BlockSpec((rows, D_MODEL), lambda l, j: (0, 0)),
                  pl.BlockSpec((1, D_MODEL, tn), lambda l, j: (l, 0, j)),
                  pl.BlockSpec((1, 1, tn), lambda l, j: (l, 0, j))],
        out_specs=pl.BlockSpec((1, rows, tn), lambda l, j: (l, 0, j)),
        out_shape=jax.ShapeDtypeStruct((DEPTH, rows, n_out), F32),
        compiler_params=_cparams(("parallel", "parallel")),
        name="adaln",
    )(c_all, w_ada, b_ada.reshape(DEPTH, 1, n_out))


MOD_SHIFT1, MOD_SCALE1, MOD_GATE1, MOD_SHIFT2, MOD_SCALE2, MOD_GATE2 = range(6)


class _Group:
    def __init__(self, batch, time, per_token, mod):
        self.batch, self.time, self.per_token = batch, time, per_token
        self.n_tok = batch * time
        if per_token:
            self.mod = jnp.repeat(mod, time, axis=1)[:, None]
        else:
            self.mod = mod[:, :, None, :]

    def tile(self, tm):
        tm = min(tm, self.n_tok)
        assert (self.n_tok if self.per_token else self.time) % tm == 0
        return tm

    def mod_spec(self, layer, col, tm):
        if self.per_token:
            return pl.BlockSpec((1, 1, tm, D_MODEL), lambda *g: (layer, 0, g[0], col))
        per = self.time // tm
        return pl.BlockSpec((1, 1, 1, D_MODEL), lambda *g: (layer, g[0] // per, 0, col))


def _in_proj_body(x_ref, nw_ref, sc_ref, sh_ref, w_ref, z_ref, h_scr):
    @pl.when(pl.program_id(1) == 0)
    def _():
        h = _rms(x_ref[...]) * nw_ref[...]
        h_scr[...] = (h * (1.0 + sc_ref[0, 0]) + sh_ref[0, 0]).astype(BF16)

    z_ref[...] = _dot(h_scr[...], w_ref[...])


def _in_proj(grp, layer, x, norm_w, w_pad):
    tm, tn = grp.tile(1024), 1024
    return pl.pallas_call(
        _in_proj_body,
        grid=(grp.n_tok // tm, Z_WIDTH // tn),
        in_specs=[pl.BlockSpec((tm, D_MODEL), lambda i, j: (i, 0)),
                  pl.BlockSpec((1, D_MODEL), lambda i, j: (0, 0)),
                  grp.mod_spec(layer, MOD_SCALE1, tm), grp.mod_spec(layer, MOD_SHIFT1, tm),
                  pl.BlockSpec((D_MODEL, tn), lambda i, j: (0, j))],
        out_specs=pl.BlockSpec((tm, tn), lambda i, j: (i, j)),
        out_shape=jax.ShapeDtypeStruct((grp.n_tok, Z_WIDTH), F32),
        scratch_shapes=[pltpu.VMEM((tm, D_MODEL), BF16)],
        compiler_params=_cparams(("parallel", "arbitrary")),
        name="in_proj",
    )(x, norm_w.reshape(1, D_MODEL), grp.mod, grp.mod, w_pad)


def _pad_w_in(w):
    idx = np.cumsum(IN_SPLITS)[:-1].tolist()
    cq, ckv, kr, hq, hf, hi, hgate, gq, gk, gv, ggate, glow, br = jnp.split(w, idx, axis=1)
    zeros = lambda n: jnp.zeros((w.shape[0], n), w.dtype)
    half = MLA_ROPE // 2
    kr_rot = jnp.concatenate([-kr[:, half:], kr[:, :half]], axis=1)
    slab = lambda a: jnp.concatenate([zeros(MLA_NOPE), a, zeros(HEAD_SLAB - MLA_NOPE - MLA_ROPE)], axis=1)
    pad_heads = lambda a: jnp.pad(a.reshape(-1, GLA_HEADS, GLA_DK),
                                  ((0, 0), (0, 0), (0, HEAD_SLAB - GLA_DK))).reshape(-1, GLA_HEADS * HEAD_SLAB)
    glow_slab = jnp.concatenate([glow, zeros(LANE - GLA_GATE_RANK)], axis=1)
    out = jnp.concatenate([cq, slab(kr), ckv, slab(kr_rot), glow_slab, hq, hf, hi, hgate,
                           pad_heads(gq), pad_heads(gk), gv, ggate, br], axis=1)
    assert out.shape[1] == Z_WIDTH
    return out.astype(BF16)


def _mla_weights(w_uq, w_uk, w_uv):
    hd = MLA_NOPE + MLA_ROPE
    half = MLA_ROPE // 2
    q = w_uq.reshape(MLA_Q_LORA, MLA_HEADS, hd)
    nope, pe = q[..., :MLA_NOPE], q[..., MLA_NOPE:]
    pe_rot = jnp.concatenate([-pe[..., half:], pe[..., :half]], axis=-1)
    z = lambda n: jnp.zeros((MLA_Q_LORA, MLA_HEADS, n), w_uq.dtype)
    wa = jnp.concatenate([nope, pe, z(HEAD_SLAB - hd)], axis=-1).reshape(MLA_Q_LORA, -1)
    wb = jnp.concatenate([z(MLA_NOPE), pe_rot, z(HEAD_SLAB - hd)], axis=-1).reshape(MLA_Q_LORA, -1)
    k = w_uk.reshape(MLA_KV_LORA, MLA_HEADS, MLA_NOPE)
    wka = jnp.pad(k, ((0, 0), (0, 0), (0, HEAD_SLAB - MLA_NOPE))).reshape(MLA_KV_LORA, -1)
    wukt = jnp.pad(k.transpose(1, 2, 0), ((0, 0), (0, HEAD_SLAB - MLA_NOPE), (0, 0)))
    return wa.astype(BF16), wb.astype(BF16), wka.astype(BF16), w_uv.astype(BF16), wukt.astype(BF16)


def _rope_tables(pos):
    half = MLA_ROPE // 2
    inv = ROPE_BASE ** (-jnp.arange(half, dtype=F32) / half)
    ang = pos.astype(F32)[:, None] * inv[None, :]
    cos, sin = jnp.cos(ang), jnp.sin(ang)
    n = pos.shape[0]
    tail = jnp.zeros((n, HEAD_SLAB - MLA_NOPE - MLA_ROPE), F32)
    ct = jnp.concatenate([jnp.ones((n, MLA_NOPE), F32), cos, cos, tail], axis=1)
    st = jnp.concatenate([jnp.zeros((n, MLA_NOPE), F32), sin, sin, tail], axis=1)
    return ct, st


def _mla_common(z_ref, qnw_ref, kvnw_ref, wa_ref, wb_ref, ct_ref, st_ref):
    z = z_ref[...]
    qn = (_rms(z[:, Z_CQ:Z_CQ + MLA_Q_LORA]) * qnw_ref[...]).astype(BF16)
    lat = _rms(z[:, Z_CKV:Z_CKV + MLA_KV_LORA]) * kvnw_ref[...]
    ct, st = ct_ref[...], st_ref[...]
    ct8 = jnp.concatenate([ct] * MLA_HEADS, axis=1)
    st8 = jnp.concatenate([st] * MLA_HEADS, axis=1)
    q_cat = (_dot(qn, wa_ref[...]) * ct8 + _dot(qn, wb_ref[...]) * st8) * (MLA_SCALE * LOG2_E)
    kpe = z[:, Z_KR:Z_KR + HEAD_SLAB] * ct + z[:, Z_KRR:Z_KRR + HEAD_SLAB] * st
    return q_cat, lat, kpe


def _mla_prep_prompt_body(z_ref, qnw_ref, kvnw_ref, wa_ref, wb_ref, ct_ref, st_ref, wka_ref, wv_ref,
                          q_ref, k_ref, v_ref, lat_ref, kpe_ref):
    q_cat, lat, kpe = _mla_common(z_ref, qnw_ref, kvnw_ref, wa_ref, wb_ref, ct_ref, st_ref)
    q_ref[...] = q_cat.astype(BF16)
    lat_ref[...] = lat
    kpe_ref[...] = kpe
    lb = lat.astype(BF16)
    k_ref[...] = (_dot(lb, wka_ref[...]) + jnp.concatenate([kpe] * MLA_HEADS, axis=1)).astype(BF16)
    v_ref[...] = _dot(lb, wv_ref[...]).astype(BF16)


def _mla_prep_sample_body(z_ref, qnw_ref, kvnw_ref, wa_ref, wb_ref, ct_ref, st_ref, wukt_ref,
                          q_ref, qlat_ref, lat_ref, kpe_ref):
    q_cat, lat, kpe = _mla_common(z_ref, qnw_ref, kvnw_ref, wa_ref, wb_ref, ct_ref, st_ref)
    qb = q_cat.astype(BF16)
    q_ref[...] = qb
    lat_ref[...] = lat
    kpe_ref[...] = kpe
    for h in range(MLA_HEADS):
        qlat_ref[:, h * MLA_KV_LORA:(h + 1) * MLA_KV_LORA] = _dot(
            qb[:, h * HEAD_SLAB:(h + 1) * HEAD_SLAB], wukt_ref[h]).astype(BF16)


def _mla_prep(grp, z, q_norm_w, kv_norm_w, mw, ct, st, sample):
    wa, wb, wka, wv, wukt = mw
    tm = grp.tile(512)
    n_tiles = grp.n_tok // tm
    hw = MLA_HEADS * HEAD_SLAB
    full = lambda a: pl.BlockSpec(a.shape, lambda i: (0,) * a.ndim)
    row = lambda w: pl.BlockSpec((tm, w), lambda i: (i, 0))
    if grp.per_token:
        tab = pl.BlockSpec((tm, HEAD_SLAB), lambda i: (i, 0))
    else:
        per = grp.time // tm
        tab = pl.BlockSpec((tm, HEAD_SLAB), lambda i: (i % per, 0))
    qnw = q_norm_w.reshape(1, -1)
    kvnw = kv_norm_w.reshape(1, -1)
    common_in = [pl.BlockSpec((tm, 1024), lambda i: (i, 0)), full(qnw), full(kvnw), full(wa), full(wb), tab, tab]
    n = grp.n_tok
    if sample:
        return pl.pallas_call(
            _mla_prep_sample_body, grid=(n_tiles,),
            in_specs=common_in + [full(wukt)],
            out_specs=[row(hw), row(MLA_HEADS * MLA_KV_LORA), row(MLA_KV_LORA), row(HEAD_SLAB)],
            out_shape=[jax.ShapeDtypeStruct((n, hw), BF16),
                       jax.ShapeDtypeStruct((n, MLA_HEADS * MLA_KV_LORA), BF16),
                       jax.ShapeDtypeStruct((n, MLA_KV_LORA), F32),
                       jax.ShapeDtypeStruct((n, HEAD_SLAB), F32)],
            compiler_params=_cparams(("parallel",)), name="mla_prep_sample",
        )(z, qnw, kvnw, wa, wb, ct, st, wukt)
    return pl.pallas_call(
        _mla_prep_prompt_body, grid=(n_tiles,),
        in_specs=common_in + [full(wka), full(wv)],
        out_specs=[row(hw), row(hw), row(MLA_HEADS * MLA_V), row(MLA_KV_LORA), row(HEAD_SLAB)],
        out_shape=[jax.ShapeDtypeStruct((n, hw), BF16),
                   jax.ShapeDtypeStruct((n, hw), BF16),
                   jax.ShapeDtypeStruct((n, MLA_HEADS * MLA_V), BF16),
                   jax.ShapeDtypeStruct((n, MLA_KV_LORA), F32),
                   jax.ShapeDtypeStruct((n, HEAD_SLAB), F32)],
        compiler_params=_cparams(("parallel",)), name="mla_prep_prompt",
    )(z, qnw, kvnw, wa, wb, ct, st, wka, wv)


def _flash_body(q_ref, k_ref, v_ref, o_ref, m_scr, l_scr, acc_scr, *, tq):
    qi, ki = pl.program_id(1), pl.program_id(2)

    @pl.when(ki == 0)
    def _():
        m_scr[...] = jnp.full(m_scr.shape, NEG_INF, F32)
        l_scr[...] = jnp.zeros(l_scr.shape, F32)
        acc_scr[...] = jnp.zeros(acc_scr.shape, F32)

    low = lax.broadcasted_iota(jnp.int32, (tq, LANE), 1) < MLA_V

    def step(masked):
        if masked:
            keep = (lax.broadcasted_iota(jnp.int32, (tq, tq), 1)
                    <= lax.broadcasted_iota(jnp.int32, (tq, tq), 0))
        for hp in range(MLA_HEADS // 2):
            pv, al = [], []
            for e in range(2):
                h = 2 * hp + e
                s = _dot_nt(q_ref[0, :, h * HEAD_SLAB:(h + 1) * HEAD_SLAB],
                            k_ref[0, :, h * HEAD_SLAB:(h + 1) * HEAD_SLAB])
                if masked:
                    s = jnp.where(keep, s, NEG_INF)
                m_prev = m_scr[h]
                m_new = jnp.maximum(m_prev, jnp.max(s, axis=-1, keepdims=True))
                alpha = jnp.exp2(m_prev - m_new)
                p = jnp.exp2(s - jnp.concatenate([m_new] * (tq // LANE), axis=1))
                l_scr[h] = alpha * l_scr[h] + jnp.sum(p, axis=-1, keepdims=True)
                m_scr[h] = m_new
                pv.append(_dot(p.astype(BF16), v_ref[0, :, hp * LANE:(hp + 1) * LANE]))
                al.append(alpha)
            sl = slice(hp * LANE, (hp + 1) * LANE)
            acc_scr[:, sl] = jnp.where(low, al[0], al[1]) * acc_scr[:, sl] + jnp.where(low, pv[0], pv[1])

    @pl.when(ki < qi)
    def _():
        step(False)

    @pl.when(ki == qi)
    def _():
        step(True)

    @pl.when(ki == pl.num_programs(2) - 1)
    def _():
        for hp in range(MLA_HEADS // 2):
            sl = slice(hp * LANE, (hp + 1) * LANE)
            o_ref[0, :, sl] = acc_scr[:, sl] / jnp.where(low, l_scr[2 * hp], l_scr[2 * hp + 1])


def _flash(q, k, v, batch, seq, tq):
    hw = MLA_HEADS * HEAD_SLAB
    vw = MLA_HEADS * MLA_V
    nq = seq // tq
    q3, k3, v3 = q.reshape(batch, seq, hw), k.reshape(batch, seq, hw), v.reshape(batch, seq, vw)
    out = pl.pallas_call(
        functools.partial(_flash_body, tq=tq),
        grid=(batch, nq, nq),
        in_specs=[pl.BlockSpec((1, tq, hw), lambda b, i, j: (b, i, 0)),
                  pl.BlockSpec((1, tq, hw), lambda b, i, j: (b, jnp.minimum(i, j), 0)),
                  pl.BlockSpec((1, tq, vw), lambda b, i, j: (b, jnp.minimum(i, j), 0))],
        out_specs=pl.BlockSpec((1, tq, vw), lambda b, i, j: (b, i, 0)),
        out_shape=jax.ShapeDtypeStruct((batch, seq, vw), F32),
        scratch_shapes=[pltpu.VMEM((MLA_HEADS, tq, LANE), F32), pltpu.VMEM((MLA_HEADS, tq, LANE), F32),
                        pltpu.VMEM((tq, vw), F32)],
        compiler_params=_cparams(("parallel", "parallel", "arbitrary")),
        name="flash",
    )(q3, k3, v3)
    return out.reshape(batch * seq, vw)


def _paged_body(pt_ref, q_ref, qlat_ref, nlat_ref, nkpe_ref, lat_hbm, kpe_hbm, o_ref,
                lat_buf, kpe_buf, sem, *, layer, n_pages, n_new, rows_per_step):
    g = pl.program_id(0)
    ng = pl.num_programs(0)
    rows = q_ref.shape[1]
    rps = rows_per_step

    def copies(step, slot):
        out = []
        for r in range(rps):
            for j in range(n_pages):
                pg = pt_ref[step * rps + r, j]
                dst = pl.ds(j * PAGE_SIZE, PAGE_SIZE)
                out.append(pltpu.make_async_copy(lat_hbm.at[layer, pg], lat_buf.at[slot, r, dst], sem.at[0, slot]))
                out.append(pltpu.make_async_copy(kpe_hbm.at[layer, pg], kpe_buf.at[slot, r, :, dst], sem.at[1, slot]))
        return out

    @pl.when(g == 0)
    def _():
        for c in copies(0, 0):
            c.start()

    slot = g % 2

    @pl.when(g + 1 < ng)
    def _():
        for c in copies(g + 1, 1 - slot):
            c.start()

    for c in copies(g, slot):
        c.wait()

    for r in range(rps):
        qlat = qlat_ref[r].astype(F32)
        qpe = q_ref[r][:, MLA_NOPE:MLA_NOPE + MLA_ROPE].astype(F32)
        s_past = _dot_nt(qlat, lat_buf[slot, r]) + _dot(qpe, kpe_buf[slot, r])
        nlat = nlat_ref[r].astype(BF16)
        nkpe = nkpe_ref[r][:, MLA_NOPE:MLA_NOPE + MLA_ROPE].astype(BF16)
        s_new = _dot_nt(qlat_ref[r], nlat) + _dot_nt(q_ref[r][:, MLA_NOPE:MLA_NOPE + MLA_ROPE], nkpe)
        t_of_row = lax.broadcasted_iota(jnp.int32, (rows, n_new), 0) // MLA_HEADS
        s_new = jnp.where(lax.broadcasted_iota(jnp.int32, (rows, n_new), 1) <= t_of_row, s_new, NEG_INF)
        m = jnp.maximum(jnp.max(s_past, axis=-1, keepdims=True), jnp.max(s_new, axis=-1, keepdims=True))
        p_past = jnp.exp2(s_past - m)
        p_new = jnp.exp2(s_new - m)
        denom = jnp.sum(p_past, axis=-1, keepdims=True) + jnp.sum(p_new, axis=-1, keepdims=True)
        o = _dot(p_past.astype(BF16).astype(F32), lat_buf[slot, r]) + _dot(p_new.astype(BF16), nlat)
        o_ref[r] = o / denom


def _paged(page_table, q_cat, q_lat, lat_new, kpe_new, cache_latent, cache_k_rope, layer, batch, n_new):
    rows = n_new * MLA_HEADS
    n_pages = page_table.shape[1]
    past = n_pages * PAGE_SIZE
    rps = PAGED_ROWS_PER_STEP
    q3 = q_cat.reshape(batch, rows, HEAD_SLAB)
    ql3 = q_lat.reshape(batch, rows, MLA_KV_LORA)
    nl3 = lat_new.reshape(batch, n_new, MLA_KV_LORA)
    nk3 = kpe_new.reshape(batch, n_new, HEAD_SLAB)
    blk = lambda r, w: pl.BlockSpec((rps, r, w), lambda b, pt: (b, 0, 0))
    out = pl.pallas_call(
        functools.partial(_paged_body, layer=layer, n_pages=n_pages, n_new=n_new, rows_per_step=rps),
        grid_spec=pltpu.PrefetchScalarGridSpec(
            num_scalar_prefetch=1, grid=(batch // rps,),
            in_specs=[blk(rows, HEAD_SLAB), blk(rows, MLA_KV_LORA), blk(n_new, MLA_KV_LORA), blk(n_new, HEAD_SLAB),
                      pl.BlockSpec(memory_space=pl.ANY), pl.BlockSpec(memory_space=pl.ANY)],
            out_specs=blk(rows, MLA_KV_LORA),
            scratch_shapes=[pltpu.VMEM((2, rps, past, MLA_KV_LORA), F32), pltpu.VMEM((2, rps, MLA_ROPE, past), F32),
                            pltpu.SemaphoreType.DMA((2, 2))]),
        out_shape=jax.ShapeDtypeStruct((batch, rows, MLA_KV_LORA), F32),
        compiler_params=_cparams(("arbitrary",)),
        name="paged",
    )(page_table, q3, ql3, nl3, nk3, cache_latent, jnp.swapaxes(cache_k_rope, 2, 3))
    return out.reshape(batch * n_new, MLA_HEADS * MLA_KV_LORA)


def _head_proj_body(o_ref, w_ref, y_ref):
    for h in range(MLA_HEADS):
        y_ref[:, h * MLA_V:(h + 1) * MLA_V] = _dot(
            o_ref[:, h * MLA_KV_LORA:(h + 1) * MLA_KV_LORA].astype(BF16), w_ref[h])


def _head_proj(o_lat, w_uv):
    n = o_lat.shape[0]
    w = w_uv.reshape(MLA_KV_LORA, MLA_HEADS, MLA_V).transpose(1, 0, 2).astype(BF16)
    return pl.pallas_call(
        _head_proj_body, grid=(1,),
        in_specs=[pl.BlockSpec(o_lat.shape, lambda i: (0, 0)), pl.BlockSpec(w.shape, lambda i: (0, 0, 0))],
        out_specs=pl.BlockSpec((n, MLA_HEADS * MLA_V), lambda i: (0, 0)),
        out_shape=jax.ShapeDtypeStruct((n, MLA_HEADS * MLA_V), F32),
        compiler_params=_cparams(("arbitrary",)), name="head_proj",
    )(o_lat, w)


N_REC_HEADS = 4
REC_WIDTH = N_REC_HEADS * HEAD_SLAB


def _split3(x):
    a = x.astype(BF16)
    r = x - a.astype(F32)
    b = r.astype(BF16)
    c = (r - b.astype(F32)).astype(BF16)
    return a, b, c


def _scan_body(*refs, gla, chunk, sub, valid, has_s0, k_dim, bb):
    if gla:
        q_ref, k_ref, v_ref, glow_ref, wg_ref, bg_ref, tri_ref = refs[:7]
        rest = refs[7:]
    else:
        q_ref, k_ref, v_ref, la_ref, l1_ref, tri_ref = refs[:6]
        rest = refs[6:]
    if has_s0:
        s0_ref, _, o_ref, sfin_ref, st_scr = rest
    else:
        _, o_ref, sfin_ref, st_scr = rest
    ci = pl.program_id(1)
    nh = N_REC_HEADS

    @pl.when(ci == 0)
    def _():
        if has_s0:
            for i in range(bb):
                for h in range(nh):
                    s0 = s0_ref[i, h]
                    if k_dim < HEAD_SLAB:
                        s0 = jnp.concatenate([s0, jnp.zeros((HEAD_SLAB - k_dim, s0.shape[1]), F32)], axis=0)
                    st_scr[i * nh + h] = s0.T
        else:
            st_scr[...] = jnp.zeros(st_scr.shape, F32)

    for i in range(bb):
        _scan_chunk(i, q_ref, k_ref, v_ref, refs, o_ref, st_scr, tri_ref,
                    gla=gla, chunk=chunk, sub=sub, valid=valid)

    @pl.when(ci == pl.num_programs(1) - 1)
    def _():
        for i in range(bb):
            for h in range(nh):
                sfin_ref[0, i, h] = st_scr[i * nh + h].T[:k_dim]


def _scan_chunk(row, q_ref, k_ref, v_ref, refs, o_ref, st_scr, tri_ref, *, gla, chunk, sub, valid):
    nh = N_REC_HEADS
    v = v_ref[row]
    if gla:
        glow_ref, wg_ref, bg_ref = refs[3:6]
        q = q_ref[row] * (GLA_DK ** -0.5)
        k = k_ref[row]
        g = _log_sigmoid(_dot(glow_ref[row].astype(BF16), wg_ref[...]) + bg_ref[...]) * (1.0 / GLA_GATE_NORMALIZER)
    else:
        la_ref, l1_ref = refs[3:5]
        xq = q_ref[row]
        q = xq * jax.nn.sigmoid(xq) * (HG_DK ** -0.5)
        a = la_ref[...]
        bb = l1_ref[...] + _log_sigmoid(k_ref[row])
        g = jnp.maximum(a, bb) + jnp.log1p(jnp.exp(-jnp.abs(a - bb)))
        k = 1.0 - jnp.exp(g)
    if valid < chunk:
        live = lax.broadcasted_iota(jnp.int32, (chunk, 1), 0) < valid
        g = jnp.where(live, g, 0.0)
        k = jnp.where(live, k, 0.0)

    tri = tri_ref[...]
    g1, g2, g3 = _split3(g)
    b = _dot(tri, g1) + _dot(tri, g2) + _dot(tri, g3)
    b_end = b[chunk - 1:chunk]
    hs = lambda h: slice(h * HEAD_SLAB, (h + 1) * HEAD_SLAB)
    b2 = b * LOG2_E
    qe = (q * jnp.exp(b)).astype(BF16)
    kd_f = k * jnp.exp(b_end - b)
    kd = kd_f.astype(BF16)
    kd_lo = (kd_f - kd.astype(F32)).astype(BF16)
    vb = v.astype(BF16)
    v_lo = (v - vb.astype(F32)).astype(BF16)

    o_inter = jnp.concatenate(
        [_dot_nt(qe[:, hs(h)], st_scr[row * nh + h].astype(BF16)) for h in range(nh)], axis=1)

    row_in_tile = lax.broadcasted_iota(jnp.int32, (SUBLANES, 1), 0)
    blocks = []
    for i in range(chunk // sub):
        lo = i * sub
        if lo >= valid:
            blocks.append(o_inter[lo:lo + sub])
            continue
        bi, qi_ = b[lo:lo + sub], q[lo:lo + sub]
        blk = o_inter[lo:lo + sub]
        if i > 0:
            r = b[lo - 1:lo]
            qt = (qi_ * jnp.exp(bi - r)).astype(BF16)
            kt = (k[:lo] * jnp.exp(r - b[:lo])).astype(BF16)
            off = []
            for h in range(nh):
                att = _dot_nt(qt[:, hs(h)], kt[:, hs(h)])
                off.append(_dot(att.astype(BF16), vb[:lo, hs(h)]))
            blk = blk + jnp.concatenate(off, axis=1)
        live_rows = min(sub, valid - lo)
        tiles = []
        for r0 in range(0, sub, SUBLANES):
            acc = jnp.zeros((SUBLANES, REC_WIDTH), F32)
            if r0 < live_rows:
                bt = b2[lo + r0:lo + r0 + SUBLANES]
                qt8 = qi_[r0:r0 + SUBLANES]
                rows8 = row_in_tile + r0
                for s in range(min(live_rows, r0 + SUBLANES)):
                    d = bt - b2[lo + s:lo + s + 1]
                    if s > r0:
                        d = jnp.minimum(d, 0.0)
                    w = qt8 * (k[lo + s:lo + s + 1] * jnp.exp2(d))
                    v_s = v[lo + s:lo + s + 1]
                    parts = []
                    for h in range(nh):
                        a_ts = jnp.sum(w[:, hs(h)], axis=-1, keepdims=True)
                        if s > r0:
                            a_ts = jnp.where(rows8 >= s, a_ts, 0.0)
                        parts.append(a_ts * v_s[:, hs(h)])
                    acc = acc + jnp.concatenate(parts, axis=1)
            tiles.append(acc)
        blocks.append(blk + jnp.concatenate(tiles, axis=0))
    o_ref[row] = jnp.concatenate(blocks, axis=0)

    decay = jnp.exp(b_end)
    for h in range(nh):
        upd = (_dot_tn(vb[:, hs(h)], kd[:, hs(h)]) + _dot_tn(vb[:, hs(h)], kd_lo[:, hs(h)])
               + _dot_tn(v_lo[:, hs(h)], kd[:, hs(h)]))
        st_scr[row * nh + h] = st_scr[row * nh + h] * decay[:, hs(h)] + upd


def _scan(z, batch, time, chunk, sub, valid, gla, extra, s0, bb, layer, stacked):
    k_dim = GLA_DK if gla else HG_DK
    nck = time // chunk
    nh = N_REC_HEADS
    z3 = z.reshape(batch, time, Z_WIDTH)
    cb = lambda col: pl.BlockSpec((bb, chunk, REC_WIDTH), lambda b, c: (b, c, col // REC_WIDTH))
    full = lambda a: pl.BlockSpec(a.shape, lambda b, c: (0,) * a.ndim)
    state = pl.BlockSpec((bb, nh, k_dim, HEAD_SLAB), lambda b, c: (b, 0, 0, 0))
    tri = jnp.tril(jnp.ones((chunk, chunk), F32)).astype(BF16)
    if gla:
        wg, bg = extra
        ins = [z3, z3, z3, z3, wg, bg, tri]
        specs = [cb(Z_GQ), cb(Z_GK), cb(Z_GV),
                 pl.BlockSpec((bb, chunk, LANE), lambda b, c: (b, c, Z_GLOW // LANE)), full(wg), full(bg), full(tri)]
    else:
        la, l1 = extra
        ins = [z3, z3, z3, la, l1, tri]
        specs = [cb(Z_HQ), cb(Z_HF), cb(Z_HI), full(la), full(l1), full(tri)]
    if s0 is not None:
        ins.append(s0)
        specs.append(state)
    ins.append(stacked)
    specs.append(pl.BlockSpec(memory_space=pl.ANY))
    o, stacked = pl.pallas_call(
        functools.partial(_scan_body, gla=gla, chunk=chunk, sub=sub, valid=valid, has_s0=s0 is not None,
                          k_dim=k_dim, bb=bb),
        grid=(batch // bb, nck),
        in_specs=specs,
        out_specs=[pl.BlockSpec((bb, chunk, REC_WIDTH), lambda b, c: (b, c, 0)),
                   pl.BlockSpec((1, bb, nh, k_dim, HEAD_SLAB), lambda b, c: (layer, b, 0, 0, 0))],
        out_shape=[jax.ShapeDtypeStruct((batch, time, REC_WIDTH), F32),
                   jax.ShapeDtypeStruct(stacked.shape, F32)],
        input_output_aliases={len(ins) - 1: 1},
        scratch_shapes=[pltpu.VMEM((bb * nh, HEAD_SLAB, HEAD_SLAB), F32)],
        compiler_params=_cparams(("parallel", "arbitrary")),
        name="scan_gla" if gla else "scan_hgrn",
    )(*ins)
    return o.reshape(batch * time, REC_WIDTH), stacked


def _merge_body(x_ref, ym_ref, oh_ref, og_ref, hgate_ref, ggate_ref, br0_ref, br1_ref, br2_ref,
                g1_ref, sc2_ref, sh2_ref, hgw_ref, glw_ref, nfw_ref, wbm_ref, wbh_ref, wbg_ref, wout_ref,
                rwh_ref, rwl_ref, rb_ref, tri_ref, cnt0_ref, x1_ref, h2_ref, te_ref, tw_ref, rk_ref, cnt_ref,
                carry_scr):
    silu = lambda t: t * jax.nn.sigmoid(t)
    yh = _rms(oh_ref[...]) * hgw_ref[...] * silu(hgate_ref[...])
    og = og_ref[...]
    glw = glw_ref[...]
    yg = jnp.concatenate([_rms(og[:, h * GLA_DV:(h + 1) * GLA_DV]) * glw for h in range(GLA_HEADS)], axis=1)
    yg = yg * silu(ggate_ref[...])
    m = (jax.nn.sigmoid(br0_ref[...]) * _dot(ym_ref[...].astype(BF16), wbm_ref[...])
         + jax.nn.sigmoid(br1_ref[...]) * _dot(yh.astype(BF16), wbh_ref[...])
         + jax.nn.sigmoid(br2_ref[...]) * _dot(yg.astype(BF16), wbg_ref[...]))
    x1 = x_ref[...] + g1_ref[0, 0] * _dot(m.astype(BF16), wout_ref[...])
    x1_ref[...] = x1
    h2 = _rms(x1) * nfw_ref[...] * (1.0 + sc2_ref[0, 0]) + sh2_ref[0, 0]
    h2_ref[...] = h2
    hh = h2.astype(BF16)
    hl = (h2 - hh.astype(F32)).astype(BF16)
    rwh = rwh_ref[...]
    logits = _dot(hh, rwh) + _dot(hh, rwl_ref[...]) + _dot(hl, rwh) + rb_ref[...]

    lane = lax.broadcasted_iota(jnp.int32, logits.shape, 1).astype(F32)
    work = jnp.where(lane < N_EXPERTS, logits, NEG_INF)
    vals, idxs, hots = [], [], []
    for _ in range(TOP_K):
        mx = jnp.max(work, axis=1, keepdims=True)
        idx = jnp.min(jnp.where(work == mx, lane, float(LANE)), axis=1, keepdims=True)
        hot = lane == idx
        work = jnp.where(hot, NEG_INF, work)
        vals.append(mx)
        idxs.append(idx)
        hots.append(hot)
    ex = [jnp.exp(v - vals[0]) for v in vals]
    den = ex[0] + ex[1] + ex[2] + ex[3]

    @pl.when(pl.program_id(0) == 0)
    def _():
        carry_scr[...] = cnt0_ref[...]

    cnt = sum(h.astype(F32) for h in hots)
    before = _dot(tri_ref[...], cnt.astype(BF16)) + carry_scr[...]
    te = jnp.zeros(logits.shape, F32)
    tw = jnp.zeros(logits.shape, F32)
    rk = jnp.zeros(logits.shape, F32)
    for k in range(TOP_K):
        sel = lane == float(k)
        te = jnp.where(sel, idxs[k], te)
        tw = jnp.where(sel, ex[k] / den, tw)
        rk = jnp.where(sel, jnp.sum(jnp.where(hots[k], before, 0.0), axis=1, keepdims=True), rk)
    te_ref[...] = te.astype(jnp.int32)
    tw_ref[...] = tw
    rk_ref[...] = rk.astype(jnp.int32)
    carry_scr[...] = carry_scr[...] + jnp.sum(cnt, axis=0, keepdims=True)
    cnt_ref[...] = carry_scr[...]


def _merge(grp, layer, x, y_mla, o_hg, o_gla, z, lw, cnt0):
    tm = grp.tile(256)
    tri = jnp.tril(jnp.ones((tm, tm), F32), k=-1).astype(BF16)
    row = lambda w: pl.BlockSpec((tm, w), lambda i: (i, 0))
    zc = lambda col, w: pl.BlockSpec((tm, w), lambda i: (i, col // w))
    full = lambda a: pl.BlockSpec(a.shape, lambda i: (0,) * a.ndim)
    ws = [lw["hg_norm_w"], lw["gla_norm_w"], lw["norm_ffn_w"], lw["w_br_mla"], lw["w_br_hg"], lw["w_br_gla"],
          lw["w_out"], lw["rw_hi"], lw["rw_lo"], lw["rb"], tri, cnt0]
    n = grp.n_tok
    one = pl.BlockSpec((1, LANE), lambda i: (0, 0))
    return pl.pallas_call(
        _merge_body, grid=(grp.n_tok // tm,),
        in_specs=[row(D_MODEL), row(512), row(512), row(512), zc(Z_HGATE, 512), zc(Z_GGATE, 512),
                  zc(Z_BR, 1024), zc(Z_BR + 1024, 1024), zc(Z_BR + 2048, 1024),
                  grp.mod_spec(layer, MOD_GATE1, tm), grp.mod_spec(layer, MOD_SCALE2, tm),
                  grp.mod_spec(layer, MOD_SHIFT2, tm)] + [full(w) for w in ws],
        out_specs=[row(D_MODEL), row(D_MODEL), row(LANE), row(LANE), row(LANE), one],
        out_shape=[jax.ShapeDtypeStruct((n, D_MODEL), F32), jax.ShapeDtypeStruct((n, D_MODEL), F32),
                   jax.ShapeDtypeStruct((n, LANE), jnp.int32), jax.ShapeDtypeStruct((n, LANE), F32),
                   jax.ShapeDtypeStruct((n, LANE), jnp.int32), jax.ShapeDtypeStruct((1, LANE), F32)],
        scratch_shapes=[pltpu.VMEM((1, LANE), F32)],
        compiler_params=_cparams(("arbitrary",)), name="merge",
    )(x, y_mla, o_hg, o_gla, z, z, z, z, z, grp.mod, grp.mod, grp.mod, *ws)


def _experts_body(be_ref, grp_ref, nxt_ref, nu_ref, x_ref, bgu_ref, bd_ref, wgu_hbm, wd_hbm, y_ref,
                  wgu_buf, wd_buf, wgu_bf, wd_bf, sem, *, layer):
    i = pl.program_id(0)
    e = be_ref[i]
    slot = grp_ref[i] % 2
    first = jnp.logical_or(i == 0, e != be_ref[jnp.maximum(i - 1, 0)])

    def fetch(expert, s):
        w = layer * N_EXPERTS + expert
        return (pltpu.make_async_copy(wgu_hbm.at[w], wgu_buf.at[s], sem.at[0, s]),
                pltpu.make_async_copy(wd_hbm.at[w], wd_buf.at[s], sem.at[1, s]))

    @pl.when(i == 0)
    def _():
        for c in fetch(e, 0):
            c.start()

    @pl.when(first)
    def _():
        for c in fetch(e, slot):
            c.wait()

        @pl.when(nxt_ref[i] >= 0)
        def _():
            for c in fetch(nxt_ref[i], 1 - slot):
                c.start()

        wgu_bf[...] = wgu_buf[slot].astype(BF16)
        wd_bf[...] = wd_buf[slot].astype(BF16)

    @pl.when(i < nu_ref[0])
    def _():
        gu = _dot(x_ref[...].astype(BF16), wgu_bf[...]) + bgu_ref[0]
        gate = jnp.minimum(gu[:, :D_EXPERT], SWIGLU_LIMIT)
        up = jnp.clip(gu[:, D_EXPERT:], -SWIGLU_LIMIT, SWIGLU_LIMIT)
        act = (up + 1.0) * gate * jax.nn.sigmoid(SWIGLU_ALPHA * gate)
        y_ref[...] = _dot(act.astype(BF16), wd_bf[...]) + bd_ref[0]

    @pl.when(i >= nu_ref[0])
    def _():
        y_ref[...] = jnp.zeros(y_ref.shape, F32)


def _experts(block_e, block_grp, block_nxt, n_used, xb, w_gate_up, b_gate_up, w_down, b_down, layer):
    n_slots = xb.shape[0]
    tm = MOE_TM
    ne = N_EXPERTS
    bias = lambda w: pl.BlockSpec((1, 1, w), lambda i, be, grp, nxt, nu: (layer * ne + be[i], 0, 0))
    return pl.pallas_call(
        functools.partial(_experts_body, layer=layer),
        grid_spec=pltpu.PrefetchScalarGridSpec(
            num_scalar_prefetch=4, grid=(n_slots // tm,),
            in_specs=[pl.BlockSpec((tm, D_MODEL), lambda i, be, grp, nxt, nu: (i, 0)),
                      bias(2 * D_EXPERT), bias(D_MODEL),
                      pl.BlockSpec(memory_space=pl.ANY), pl.BlockSpec(memory_space=pl.ANY)],
            out_specs=pl.BlockSpec((tm, D_MODEL), lambda i, be, grp, nxt, nu: (i, 0)),
            scratch_shapes=[pltpu.VMEM((2, D_MODEL, 2 * D_EXPERT), F32), pltpu.VMEM((2, D_EXPERT, D_MODEL), F32),
                            pltpu.VMEM((D_MODEL, 2 * D_EXPERT), BF16), pltpu.VMEM((D_EXPERT, D_MODEL), BF16),
                            pltpu.SemaphoreType.DMA((2, 2))]),
        out_shape=jax.ShapeDtypeStruct((n_slots, D_MODEL), F32),
        compiler_params=_cparams(("arbitrary",)),
        name="experts",
    )(block_e, block_grp, block_nxt, n_used, xb,
      b_gate_up.reshape(DEPTH * ne, 1, 2 * D_EXPERT), b_down.reshape(DEPTH * ne, 1, D_MODEL),
      w_gate_up.reshape(DEPTH * ne, D_MODEL, 2 * D_EXPERT), w_down.reshape(DEPTH * ne, D_EXPERT, D_MODEL))


def _route(top_e, rank, counts):
    n = top_e.shape[0]
    nk = n * TOP_K
    tm = MOE_TM
    experts = jnp.arange(N_EXPERTS, dtype=jnp.int32)
    padded = (counts + tm - 1) // tm * tm
    pad_end = jnp.cumsum(padded)
    pad_start = pad_end - padded
    start = jnp.cumsum(counts) - counts
    dest = jnp.sum(jnp.where(top_e[..., None] == experts, pad_start, 0), axis=-1) + rank
    flat_tok = jnp.arange(nk, dtype=jnp.int32) // TOP_K
    _, stok = lax.sort((dest.reshape(-1), flat_tok), num_keys=1)
    n_blocks = (nk + N_EXPERTS * (tm - 1) + tm - 1) // tm
    block_lo = jnp.arange(n_blocks, dtype=jnp.int32) * tm
    block_e = jnp.minimum(jnp.sum((pad_end[None, :] <= block_lo[:, None]).astype(jnp.int32), axis=1),
                          N_EXPERTS - 1)
    off = (block_lo - pad_start[block_e])[:, None] + jnp.arange(tm, dtype=jnp.int32)[None, :]
    live = off < counts[block_e][:, None]
    src = jnp.clip(start[block_e][:, None] + off, 0, nk - 1)
    slot_tok = jnp.where(live, stok[src.reshape(-1)].reshape(n_blocks, tm), 0).reshape(-1)
    n_used = (pad_end[-1] // tm).astype(jnp.int32).reshape(1)
    block_e = block_e.astype(jnp.int32)
    first = jnp.concatenate([jnp.ones((1,), jnp.bool_), block_e[1:] != block_e[:-1]])
    block_grp = jnp.cumsum(first.astype(jnp.int32)) - 1
    runs = jnp.arange(n_blocks, dtype=jnp.int32)
    run_e = jnp.sum(jnp.where(first[:, None] & (block_grp[:, None] == runs[None, :]), block_e[:, None], 0), axis=0)
    following = jnp.minimum(block_grp + 1, n_blocks - 1)
    block_nxt = jnp.where(block_grp + 1 <= block_grp[-1], run_e[following], -1).astype(jnp.int32)
    return slot_tok, (block_e, block_grp, block_nxt), n_used, dest


def _final_norm_body(x_ref, w_ref, o_ref):
    o_ref[...] = _rms(x_ref[...]) * w_ref[...]


def _final_norm(x, w, tm):
    n = x.shape[0]
    return pl.pallas_call(
        _final_norm_body, grid=(n // tm,),
        in_specs=[pl.BlockSpec((tm, D_MODEL), lambda i: (i, 0)), pl.BlockSpec((1, D_MODEL), lambda i: (0, 0))],
        out_specs=pl.BlockSpec((tm, D_MODEL), lambda i: (i, 0)),
        out_shape=jax.ShapeDtypeStruct((n, D_MODEL), F32),
        compiler_params=_cparams(("parallel",)), name="final_norm",
    )(x, w.reshape(1, D_MODEL))


def kernel(x_prompt, x_sample, cache_latent, cache_k_rope, state_hgrn, state_gla, page_table, c_prompt, c_sample, norm_mix_w, norm_ffn_w, final_norm_w, w_ada, b_ada, w_in, mla_q_norm_w, mla_w_uq, mla_kv_norm_w, mla_w_uk, mla_w_uv, hgrn_lower_bounds, hgrn_norm_w, gla_w_gate, gla_b_gate, gla_norm_w, w_branch_mla, w_branch_hgrn, w_branch_gla, w_out, router_w, router_b, w_gate_up, b_gate_up, w_down, b_down):
    bp, seq, _ = x_prompt.shape
    bs, tnew, _ = x_sample.shape
    n_p, n_s = bp * seq, bs * tnew
    past_len = page_table.shape[1] * cache_latent.shape[2]
    s_pad = SCAN_SUB

    rows = bp + bs
    rows_pad = -(-rows // 8) * 8
    c_all = jnp.pad(jnp.concatenate([c_prompt, c_sample], axis=0), ((0, rows_pad - rows), (0, 0)))
    mod = _adaln(c_all, w_ada, b_ada)
    gp = _Group(bp, seq, per_token=False, mod=mod[:, :bp])
    gs = _Group(bs, tnew, per_token=True, mod=mod[:, bp:rows])

    lbs = jax.nn.softmax(hgrn_lower_bounds.astype(F32), axis=0)
    lbs = jnp.cumsum(lbs, axis=0) - lbs[0]
    log_lb = jnp.log(jnp.maximum(lbs, TINY))
    log_1m_lb = jnp.log1p(-lbs)

    ct_p, st_p = _rope_tables(jnp.arange(seq, dtype=jnp.int32))
    pos_s = past_len + jnp.arange(tnew, dtype=jnp.int32)
    ct_s, st_s = _rope_tables(jnp.tile(pos_s, bs))

    xp = x_prompt.reshape(n_p, D_MODEL)
    xs = x_sample.reshape(n_s, D_MODEL)
    lat_p, kpe_p, lat_s, kpe_s = [], [], [], []
    state_buf = lambda b, k: jnp.zeros((DEPTH, b, N_REC_HEADS, k, HEAD_SLAB), F32)
    hg_p, gla_p, hg_s, gla_s = state_buf(bp, HG_DK), state_buf(bp, GLA_DK), state_buf(bs, HG_DK), state_buf(bs, GLA_DK)

    for l in range(DEPTH):
        w_pad = _pad_w_in(w_in[l])
        mw = _mla_weights(mla_w_uq[l], mla_w_uk[l], mla_w_uv[l])
        rw = jnp.pad(router_w[l], ((0, 0), (0, LANE - N_EXPERTS)))
        rw_hi = rw.astype(BF16)
        lw = dict(hg_norm_w=hgrn_norm_w[l].reshape(1, -1), gla_norm_w=gla_norm_w[l].reshape(1, -1),
                  norm_ffn_w=norm_ffn_w[l].reshape(1, -1), w_br_mla=w_branch_mla[l].astype(BF16),
                  w_br_hg=w_branch_hgrn[l].astype(BF16), w_br_gla=w_branch_gla[l].astype(BF16),
                  w_out=w_out[l].astype(BF16), rw_hi=rw_hi, rw_lo=(rw - rw_hi.astype(F32)).astype(BF16),
                  rb=jnp.pad(router_b[l], (0, LANE - N_EXPERTS)).reshape(1, LANE))
        wg = jnp.pad(gla_w_gate[l].reshape(GLA_GATE_RANK, GLA_HEADS, GLA_DK),
                     ((0, LANE - GLA_GATE_RANK), (0, 0), (0, HEAD_SLAB - GLA_DK))).reshape(LANE, REC_WIDTH).astype(BF16)
        bg = jnp.pad(gla_b_gate[l].reshape(GLA_HEADS, GLA_DK), ((0, 0), (0, HEAD_SLAB - GLA_DK))).reshape(1, REC_WIDTH)
        hg_extra = (log_lb[l].reshape(1, -1), log_1m_lb[l].reshape(1, -1))
        g2p = mod[l, :bp, MOD_GATE2 * D_MODEL:]
        g2s = mod[l, bp:rows, MOD_GATE2 * D_MODEL:]

        zp = _in_proj(gp, l, xp, norm_mix_w[l], w_pad)
        q, k, v, latp, kpep = _mla_prep(gp, zp, mla_q_norm_w[l], mla_kv_norm_w[l], mw, ct_p, st_p, sample=False)
        y_mla_p = _flash(q, k, v, bp, seq, tq=512)
        o_hg_p, hg_p = _scan(zp, bp, seq, SCAN_CHUNK, SCAN_SUB, SCAN_CHUNK, False, hg_extra, None, bp, l, hg_p)
        o_gl_p, gla_p = _scan(zp, bp, seq, SCAN_CHUNK, SCAN_SUB, SCAN_CHUNK, True, (wg, bg), None, bp, l, gla_p)
        x1p, h2p, te_p, tw_p, rk_p, cnt_p = _merge(gp, l, xp, y_mla_p, o_hg_p, o_gl_p, zp, lw,
                                                   jnp.zeros((1, LANE), F32))

        zs = _in_proj(gs, l, xs, norm_mix_w[l], w_pad)
        qs, qlat, lats, kpes = _mla_prep(gs, zs, mla_q_norm_w[l], mla_kv_norm_w[l], mw, ct_s, st_s, sample=True)
        o_lat = _paged(page_table, qs, qlat, lats, kpes, cache_latent, cache_k_rope, l, bs, tnew)
        y_mla_s = _head_proj(o_lat, mla_w_uv[l])
        zs_pad = jnp.pad(zs.reshape(bs, tnew, Z_WIDTH), ((0, 0), (0, s_pad - tnew), (0, 0))).reshape(bs * s_pad, Z_WIDTH)
        unpad = lambda o: o.reshape(bs, s_pad, -1)[:, :tnew].reshape(n_s, -1)
        o_hg_s, hg_s = _scan(zs_pad, bs, s_pad, s_pad, SCAN_SUB, tnew, False, hg_extra, state_hgrn[l],
                             SAMPLE_ROWS_PER_STEP, l, hg_s)
        o_gl_s, gla_s = _scan(zs_pad, bs, s_pad, s_pad, SCAN_SUB, tnew, True, (wg, bg), state_gla[l],
                              SAMPLE_ROWS_PER_STEP, l, gla_s)
        x1s, h2s, te_s, tw_s, rk_s, cnt_all = _merge(gs, l, xs, y_mla_s, unpad(o_hg_s), unpad(o_gl_s), zs, lw, cnt_p)

        h2 = jnp.concatenate([h2p, h2s], axis=0)
        both = lambda a, b: jnp.concatenate([a, b], axis=0)[:, :TOP_K]
        top_w = both(tw_p, tw_s)
        slot_tok, block_e, n_used, dest = _route(both(te_p, te_s), both(rk_p, rk_s),
                                                 cnt_all[0, :N_EXPERTS].astype(jnp.int32))
        yb = _experts(*block_e, n_used, h2[slot_tok], w_gate_up, b_gate_up, w_down, b_down, l)
        combine = lambda w, d: sum(w[:, k:k + 1] * yb[d[:, k]] for k in range(TOP_K))
        xp = x1p + jnp.repeat(g2p, seq, axis=0) * combine(top_w[:n_p], dest[:n_p])
        xs = x1s + jnp.repeat(g2s, tnew, axis=0) * combine(top_w[n_p:], dest[n_p:])

        sl = slice(MLA_NOPE, MLA_NOPE + MLA_ROPE)
        lat_p.append(latp.reshape(bp, seq, -1)); kpe_p.append(kpep[:, sl].reshape(bp, seq, -1))
        lat_s.append(lats.reshape(bs, tnew, -1)); kpe_s.append(kpes[:, sl].reshape(bs, tnew, -1))

    y_prompt = _final_norm(xp, final_norm_w, 1024).reshape(bp, seq, D_MODEL)
    y_sample = _final_norm(xs, final_norm_w, n_s).reshape(bs, tnew, D_MODEL)
    st = jnp.stack
    return (y_prompt, y_sample, st(lat_p), st(kpe_p), hg_p, gla_p, st(lat_s), st(kpe_s), hg_s, gla_s)
```

```python
import functools

import numpy as np
import jax
import jax.numpy as jnp
from jax import lax
from jax.experimental import pallas as pl
from jax.experimental.pallas import tpu as pltpu

F32 = jnp.float32
BF16 = jnp.bfloat16

D_MODEL = 1024
DEPTH = 2
PAGE_SIZE = 128
MLA_HEADS = 8
MLA_NOPE = 64
MLA_ROPE = 32
MLA_V = 64
MLA_Q_LORA = 384
MLA_KV_LORA = 256
MLA_SCALE = (MLA_NOPE + MLA_ROPE) ** -0.5
ROPE_BASE = 10000.0
HG_HEADS = 4
HG_DK = 128
HG_DV = 128
GLA_HEADS = 4
GLA_DK = 64
GLA_DV = 128
GLA_GATE_RANK = 16
GLA_GATE_NORMALIZER = 16.0
N_EXPERTS = 32
TOP_K = 4
D_EXPERT = 1024
SWIGLU_LIMIT = 7.0
SWIGLU_ALPHA = 1.702
EPS = 1e-6
NEG_INF = -1e30
TINY = 1e-30
LOG2_E = 1.4426950408889634

IN_SPLITS = (MLA_Q_LORA, MLA_KV_LORA, MLA_ROPE, 512, 512, 512, 512, 256, 256, 512, 512, GLA_GATE_RANK,
             3 * D_MODEL)

LANE = 128
SUBLANES = 8
HEAD_SLAB = 128
VMEM_LIMIT = 56 * 1024 * 1024

Z_WIDTH = 8192
Z_CQ, Z_KR, Z_CKV, Z_KRR, Z_GLOW = 0, 384, 512, 768, 896
Z_HQ, Z_HF, Z_HI, Z_HGATE = 1024, 1536, 2048, 2560
Z_GQ, Z_GK, Z_GV, Z_GGATE, Z_BR = 3072, 3584, 4096, 4608, 5120

SCAN_CHUNK = 32
SCAN_SUB = 16
MOE_TM = 256
SAMPLE_ROWS_PER_STEP = 8
PAGED_ROWS_PER_STEP = 2


def _cparams(sem, vmem=VMEM_LIMIT):
    return pltpu.CompilerParams(dimension_semantics=sem, vmem_limit_bytes=vmem)


def _dot(a, b):
    return jnp.dot(a, b, preferred_element_type=F32)


def _dot_nt(a, b):
    return lax.dot_general(a, b, (((1,), (1,)), ((), ())), preferred_element_type=F32)


def _dot_tn(a, b):
    return lax.dot_general(a, b, (((0,), (0,)), ((), ())), preferred_element_type=F32)


def _rms(x):
    return x * lax.rsqrt(jnp.mean(x * x, axis=-1, keepdims=True) + EPS)


def _log_sigmoid(x):
    return jnp.minimum(x, 0.0) - jnp.log1p(jnp.exp(-jnp.abs(x)))


def _adaln_body(c_ref, w_ref, b_ref, o_ref):
    c = c_ref[...]
    a = (c * jax.nn.sigmoid(c)).astype(BF16)
    o_ref[0] = _dot(a, w_ref[0].astype(BF16)) + b_ref[0]


def _adaln(c_all, w_ada, b_ada):
    rows = c_all.shape[0]
    tn = 1536
    n_out = w_ada.shape[-1]
    return pl.pallas_call(
        _adaln_body,
        grid=(DEPTH, n_out // tn),
        in_specs=[pl.BlockSpec((rows, D_MODEL), lambda l, j: (0, 0)),
                  pl.BlockSpec((1, D_MODEL, tn), lambda l, j: (l, 0, j)),
                  pl.BlockSpec((1, 1, tn), lambda l, j: (l, 0, j))],
        out_specs=pl.BlockSpec((1, rows, tn), lambda l, j: (l, 0, j)),
        out_shape=jax.ShapeDtypeStruct((DEPTH, rows, n_out), F32),
        compiler_params=_cparams(("parallel", "parallel")),
        name="adaln",
    )(c_all, w_ada, b_ada.reshape(DEPTH, 1, n_out))


MOD_SHIFT1, MOD_SCALE1, MOD_GATE1, MOD_SHIFT2, MOD_SCALE2, MOD_GATE2 = range(6)


class _Group:
    def __init__(self, batch, time, per_token, mod):
        self.batch, self.time, self.per_token = batch, time, per_token
        self.n_tok = batch * time
        if per_token:
            self.mod = jnp.repeat(mod, time, axis=1)[:, None]
        else:
            self.mod = mod[:, :, None, :]

    def tile(self, tm):
        tm = min(tm, self.n_tok)
        assert (self.n_tok if self.per_token else self.time) % tm == 0
        return tm

    def mod_spec(self, layer, col, tm):
        if self.per_token:
            return pl.BlockSpec((1, 1, tm, D_MODEL), lambda *g: (layer, 0, g[0], col))
        per = self.time // tm
        return pl.BlockSpec((1, 1, 1, D_MODEL), lambda *g: (layer, g[0] // per, 0, col))


def _in_proj_body(x_ref, nw_ref, sc_ref, sh_ref, w_ref, z_ref, h_scr):
    @pl.when(pl.program_id(1) == 0)
    def _():
        h = _rms(x_ref[...]) * nw_ref[...]
        h_scr[...] = (h * (1.0 + sc_ref[0, 0]) + sh_ref[0, 0]).astype(BF16)

    z_ref[...] = _dot(h_scr[...], w_ref[...])


def _in_proj(grp, layer, x, norm_w, w_pad):
    tm, tn = grp.tile(1024), 2048
    return pl.pallas_call(
        _in_proj_body,
        grid=(grp.n_tok // tm, Z_WIDTH // tn),
        in_specs=[pl.BlockSpec((tm, D_MODEL), lambda i, j: (i, 0)),
                  pl.BlockSpec((1, D_MODEL), lambda i, j: (0, 0)),
                  grp.mod_spec(layer, MOD_SCALE1, tm), grp.mod_spec(layer, MOD_SHIFT1, tm),
                  pl.BlockSpec((D_MODEL, tn), lambda i, j: (0, j))],
        out_specs=pl.BlockSpec((tm, tn), lambda i, j: (i, j)),
        out_shape=jax.ShapeDtypeStruct((grp.n_tok, Z_WIDTH), F32),
        scratch_shapes=[pltpu.VMEM((tm, D_MODEL), BF16)],
        compiler_params=_cparams(("parallel", "arbitrary")),
        name="in_proj",
    )(x, norm_w.reshape(1, D_MODEL), grp.mod, grp.mod, w_pad)


def _pad_w_in(w):
    idx = np.cumsum(IN_SPLITS)[:-1].tolist()
    cq, ckv, kr, hq, hf, hi, hgate, gq, gk, gv, ggate, glow, br = jnp.split(w, idx, axis=1)
    zeros = lambda n: jnp.zeros((w.shape[0], n), w.dtype)
    half = MLA_ROPE // 2
    kr_rot = jnp.concatenate([-kr[:, half:], kr[:, :half]], axis=1)
    slab = lambda a: jnp.concatenate([zeros(MLA_NOPE), a, zeros(HEAD_SLAB - MLA_NOPE - MLA_ROPE)], axis=1)
    pad_heads = lambda a: jnp.pad(a.reshape(-1, GLA_HEADS, GLA_DK),
                                  ((0, 0), (0, 0), (0, HEAD_SLAB - GLA_DK))).reshape(-1, GLA_HEADS * HEAD_SLAB)
    glow_slab = jnp.concatenate([glow, zeros(LANE - GLA_GATE_RANK)], axis=1)
    out = jnp.concatenate([cq, slab(kr), ckv, slab(kr_rot), glow_slab, hq, hf, hi, hgate,
                           pad_heads(gq), pad_heads(gk), gv, ggate, br], axis=1)
    assert out.shape[1] == Z_WIDTH
    return out.astype(BF16)


def _mla_weights(w_uq, w_uk, w_uv):
    hd = MLA_NOPE + MLA_ROPE
    half = MLA_ROPE // 2
    q = w_uq.reshape(MLA_Q_LORA, MLA_HEADS, hd)
    nope, pe = q[..., :MLA_NOPE], q[..., MLA_NOPE:]
    pe_rot = jnp.concatenate([-pe[..., half:], pe[..., :half]], axis=-1)
    z = lambda n: jnp.zeros((MLA_Q_LORA, MLA_HEADS, n), w_uq.dtype)
    wa = jnp.concatenate([nope, pe, z(HEAD_SLAB - hd)], axis=-1).reshape(MLA_Q_LORA, -1)
    wb = jnp.concatenate([z(MLA_NOPE), pe_rot, z(HEAD_SLAB - hd)], axis=-1).reshape(MLA_Q_LORA, -1)
    k = w_uk.reshape(MLA_KV_LORA, MLA_HEADS, MLA_NOPE)
    wka = jnp.pad(k, ((0, 0), (0, 0), (0, HEAD_SLAB - MLA_NOPE))).reshape(MLA_KV_LORA, -1)
    wukt = jnp.pad(k.transpose(1, 2, 0), ((0, 0), (0, HEAD_SLAB - MLA_NOPE), (0, 0)))
    return wa.astype(BF16), wb.astype(BF16), wka.astype(BF16), w_uv.astype(BF16), wukt.astype(BF16)


def _rope_tables(pos):
    half = MLA_ROPE // 2
    inv = ROPE_BASE ** (-jnp.arange(half, dtype=F32) / half)
    ang = pos.astype(F32)[:, None] * inv[None, :]
    cos, sin = jnp.cos(ang), jnp.sin(ang)
    n = pos.shape[0]
    tail = jnp.zeros((n, HEAD_SLAB - MLA_NOPE - MLA_ROPE), F32)
    ct = jnp.concatenate([jnp.ones((n, MLA_NOPE), F32), cos, cos, tail], axis=1)
    st = jnp.concatenate([jnp.zeros((n, MLA_NOPE), F32), sin, sin, tail], axis=1)
    return ct, st


def _mla_common(z_ref, qnw_ref, kvnw_ref, wa_ref, wb_ref, ct_ref, st_ref):
    z = z_ref[...]
    qn = (_rms(z[:, Z_CQ:Z_CQ + MLA_Q_LORA]) * qnw_ref[...]).astype(BF16)
    lat = _rms(z[:, Z_CKV:Z_CKV + MLA_KV_LORA]) * kvnw_ref[...]
    ct, st = ct_ref[...], st_ref[...]
    ct8 = jnp.concatenate([ct] * MLA_HEADS, axis=1)
    st8 = jnp.concatenate([st] * MLA_HEADS, axis=1)
    q_cat = (_dot(qn, wa_ref[...]) * ct8 + _dot(qn, wb_ref[...]) * st8) * (MLA_SCALE * LOG2_E)
    kpe = z[:, Z_KR:Z_KR + HEAD_SLAB] * ct + z[:, Z_KRR:Z_KRR + HEAD_SLAB] * st
    return q_cat, lat, kpe


def _mla_prep_prompt_body(z_ref, qnw_ref, kvnw_ref, wa_ref, wb_ref, ct_ref, st_ref, wka_ref, wv_ref,
                          q_ref, k_ref, v_ref, lat_ref, kpe_ref):
    q_cat, lat, kpe = _mla_common(z_ref, qnw_ref, kvnw_ref, wa_ref, wb_ref, ct_ref, st_ref)
    q_ref[...] = q_cat.astype(BF16)
    lat_ref[...] = lat
    kpe_ref[...] = kpe
    lb = lat.astype(BF16)
    k_ref[...] = (_dot(lb, wka_ref[...]) + jnp.concatenate([kpe] * MLA_HEADS, axis=1)).astype(BF16)
    v_ref[...] = _dot(lb, wv_ref[...]).astype(BF16)


def _mla_prep_sample_body(z_ref, qnw_ref, kvnw_ref, wa_ref, wb_ref, ct_ref, st_ref, wukt_ref,
                          q_ref, qlat_ref, lat_ref, kpe_ref):
    q_cat, lat, kpe = _mla_common(z_ref, qnw_ref, kvnw_ref, wa_ref, wb_ref, ct_ref, st_ref)
    qb = q_cat.astype(BF16)
    q_ref[...] = qb
    lat_ref[...] = lat
    kpe_ref[...] = kpe
    for h in range(MLA_HEADS):
        qlat_ref[:, h * MLA_KV_LORA:(h + 1) * MLA_KV_LORA] = _dot(
            qb[:, h * HEAD_SLAB:(h + 1) * HEAD_SLAB], wukt_ref[h]).astype(BF16)


def _mla_prep(grp, z, q_norm_w, kv_norm_w, mw, ct, st, sample):
    wa, wb, wka, wv, wukt = mw
    tm = grp.tile(512)
    n_tiles = grp.n_tok // tm
    hw = MLA_HEADS * HEAD_SLAB
    full = lambda a: pl.BlockSpec(a.shape, lambda i: (0,) * a.ndim)
    row = lambda w: pl.BlockSpec((tm, w), lambda i: (i, 0))
    if grp.per_token:
        tab = pl.BlockSpec((tm, HEAD_SLAB), lambda i: (i, 0))
    else:
        per = grp.time // tm
        tab = pl.BlockSpec((tm, HEAD_SLAB), lambda i: (i % per, 0))
    qnw = q_norm_w.reshape(1, -1)
    kvnw = kv_norm_w.reshape(1, -1)
    common_in = [pl.BlockSpec((tm, 1024), lambda i: (i, 0)), full(qnw), full(kvnw), full(wa), full(wb), tab, tab]
    n = grp.n_tok
    if sample:
        return pl.pallas_call(
            _mla_prep_sample_body, grid=(n_tiles,),
            in_specs=common_in + [full(wukt)],
            out_specs=[row(hw), row(MLA_HEADS * MLA_KV_LORA), row(MLA_KV_LORA), row(HEAD_SLAB)],
            out_shape=[jax.ShapeDtypeStruct((n, hw), BF16),
                       jax.ShapeDtypeStruct((n, MLA_HEADS * MLA_KV_LORA), BF16),
                       jax.ShapeDtypeStruct((n, MLA_KV_LORA), F32),
                       jax.ShapeDtypeStruct((n, HEAD_SLAB), F32)],
            compiler_params=_cparams(("parallel",)), name="mla_prep_sample",
        )(z, qnw, kvnw, wa, wb, ct, st, wukt)
    return pl.pallas_call(
        _mla_prep_prompt_body, grid=(n_tiles,),
        in_specs=common_in + [full(wka), full(wv)],
        out_specs=[row(hw), row(hw), row(MLA_HEADS * MLA_V), row(MLA_KV_LORA), row(HEAD_SLAB)],
        out_shape=[jax.ShapeDtypeStruct((n, hw), BF16),
                   jax.ShapeDtypeStruct((n, hw), BF16),
                   jax.ShapeDtypeStruct((n, MLA_HEADS * MLA_V), BF16),
                   jax.ShapeDtypeStruct((n, MLA_KV_LORA), F32),
                   jax.ShapeDtypeStruct((n, HEAD_SLAB), F32)],
        compiler_params=_cparams(("parallel",)), name="mla_prep_prompt",
    )(z, qnw, kvnw, wa, wb, ct, st, wka, wv)


def _flash_body(q_ref, k_ref, v_ref, o_ref, m_scr, l_scr, acc_scr, *, tq):
    qi, ki = pl.program_id(1), pl.program_id(2)

    @pl.when(ki == 0)
    def _():
        m_scr[...] = jnp.full(m_scr.shape, NEG_INF, F32)
        l_scr[...] = jnp.zeros(l_scr.shape, F32)
        acc_scr[...] = jnp.zeros(acc_scr.shape, F32)

    low = lax.broadcasted_iota(jnp.int32, (tq, LANE), 1) < MLA_V

    def step(masked):
        if masked:
            keep = (lax.broadcasted_iota(jnp.int32, (tq, tq), 1)
                    <= lax.broadcasted_iota(jnp.int32, (tq, tq), 0))
        for hp in range(MLA_HEADS // 2):
            pv, al = [], []
            for e in range(2):
                h = 2 * hp + e
                s = _dot_nt(q_ref[0, :, h * HEAD_SLAB:(h + 1) * HEAD_SLAB],
                            k_ref[0, :, h * HEAD_SLAB:(h + 1) * HEAD_SLAB])
                if masked:
                    s = jnp.where(keep, s, NEG_INF)
                m_prev = m_scr[h]
                m_new = jnp.maximum(m_prev, jnp.max(s, axis=-1, keepdims=True))
                alpha = jnp.exp2(m_prev - m_new)
                p = jnp.exp2(s - jnp.concatenate([m_new] * (tq // LANE), axis=1))
                l_scr[h] = alpha * l_scr[h] + jnp.sum(p, axis=-1, keepdims=True)
                m_scr[h] = m_new
                pv.append(_dot(p.astype(BF16), v_ref[0, :, hp * LANE:(hp + 1) * LANE]))
                al.append(alpha)
            sl = slice(hp * LANE, (hp + 1) * LANE)
            acc_scr[:, sl] = jnp.where(low, al[0], al[1]) * acc_scr[:, sl] + jnp.where(low, pv[0], pv[1])

    @pl.when(ki < qi)
    def _():
        step(False)

    @pl.when(ki == qi)
    def _():
        step(True)

    @pl.when(ki == pl.num_programs(2) - 1)
    def _():
        for hp in range(MLA_HEADS // 2):
            sl = slice(hp * LANE, (hp + 1) * LANE)
            o_ref[0, :, sl] = acc_scr[:, sl] / jnp.where(low, l_scr[2 * hp], l_scr[2 * hp + 1])


def _flash(q, k, v, batch, seq, tq):
    hw = MLA_HEADS * HEAD_SLAB
    vw = MLA_HEADS * MLA_V
    nq = seq // tq
    q3, k3, v3 = q.reshape(batch, seq, hw), k.reshape(batch, seq, hw), v.reshape(batch, seq, vw)
    out = pl.pallas_call(
        functools.partial(_flash_body, tq=tq),
        grid=(batch, nq, nq),
        in_specs=[pl.BlockSpec((1, tq, hw), lambda b, i, j: (b, i, 0)),
                  pl.BlockSpec((1, tq, hw), lambda b, i, j: (b, jnp.minimum(i, j), 0)),
                  pl.BlockSpec((1, tq, vw), lambda b, i, j: (b, jnp.minimum(i, j), 0))],
        out_specs=pl.BlockSpec((1, tq, vw), lambda b, i, j: (b, i, 0)),
        out_shape=jax.ShapeDtypeStruct((batch, seq, vw), F32),
        scratch_shapes=[pltpu.VMEM((MLA_HEADS, tq, LANE), F32), pltpu.VMEM((MLA_HEADS, tq, LANE), F32),
                        pltpu.VMEM((tq, vw), F32)],
        compiler_params=_cparams(("parallel", "parallel", "arbitrary")),
        name="flash",
    )(q3, k3, v3)
    return out.reshape(batch * seq, vw)


def _paged_body(pt_ref, q_ref, qlat_ref, nlat_ref, nkpe_ref, lat_hbm, kpe_hbm, o_ref,
                lat_buf, kpe_buf, sem, *, layer, n_pages, n_new, rows_per_step):
    g = pl.program_id(0)
    ng = pl.num_programs(0)
    rows = q_ref.shape[1]
    rps = rows_per_step

    def copies(step, slot):
        out = []
        for r in range(rps):
            for j in range(n_pages):
                pg = pt_ref[step * rps + r, j]
                dst = pl.ds(j * PAGE_SIZE, PAGE_SIZE)
                out.append(pltpu.make_async_copy(lat_hbm.at[layer, pg], lat_buf.at[slot, r, dst], sem.at[0, slot]))
                out.append(pltpu.make_async_copy(kpe_hbm.at[layer, pg], kpe_buf.at[slot, r, :, dst], sem.at[1, slot]))
        return out

    @pl.when(g == 0)
    def _():
        for c in copies(0, 0):
            c.start()

    slot = g % 2

    @pl.when(g + 1 < ng)
    def _():
        for c in copies(g + 1, 1 - slot):
            c.start()

    for c in copies(g, slot):
        c.wait()

    for r in range(rps):
        qlat = qlat_ref[r].astype(F32)
        qpe = q_ref[r][:, MLA_NOPE:MLA_NOPE + MLA_ROPE].astype(F32)
        s_past = _dot_nt(qlat, lat_buf[slot, r]) + _dot(qpe, kpe_buf[slot, r])
        nlat = nlat_ref[r].astype(BF16)
        nkpe = nkpe_ref[r][:, MLA_NOPE:MLA_NOPE + MLA_ROPE].astype(BF16)
        s_new = _dot_nt(qlat_ref[r], nlat) + _dot_nt(q_ref[r][:, MLA_NOPE:MLA_NOPE + MLA_ROPE], nkpe)
        t_of_row = lax.broadcasted_iota(jnp.int32, (rows, n_new), 0) // MLA_HEADS
        s_new = jnp.where(lax.broadcasted_iota(jnp.int32, (rows, n_new), 1) <= t_of_row, s_new, NEG_INF)
        m = jnp.maximum(jnp.max(s_past, axis=-1, keepdims=True), jnp.max(s_new, axis=-1, keepdims=True))
        p_past = jnp.exp2(s_past - m)
        p_new = jnp.exp2(s_new - m)
        denom = jnp.sum(p_past, axis=-1, keepdims=True) + jnp.sum(p_new, axis=-1, keepdims=True)
        o = _dot(p_past.astype(BF16).astype(F32), lat_buf[slot, r]) + _dot(p_new.astype(BF16), nlat)
        o_ref[r] = o / denom


def _paged(page_table, q_cat, q_lat, lat_new, kpe_new, cache_latent, cache_k_rope, layer, batch, n_new):
    rows = n_new * MLA_HEADS
    n_pages = page_table.shape[1]
    past = n_pages * PAGE_SIZE
    rps = PAGED_ROWS_PER_STEP
    q3 = q_cat.reshape(batch, rows, HEAD_SLAB)
    ql3 = q_lat.reshape(batch, rows, MLA_KV_LORA)
    nl3 = lat_new.reshape(batch, n_new, MLA_KV_LORA)
    nk3 = kpe_new.reshape(batch, n_new, HEAD_SLAB)
    blk = lambda r, w: pl.BlockSpec((rps, r, w), lambda b, pt: (b, 0, 0))
    out = pl.pallas_call(
        functools.partial(_paged_body, layer=layer, n_pages=n_pages, n_new=n_new, rows_per_step=rps),
        grid_spec=pltpu.PrefetchScalarGridSpec(
            num_scalar_prefetch=1, grid=(batch // rps,),
            in_specs=[blk(rows, HEAD_SLAB), blk(rows, MLA_KV_LORA), blk(n_new, MLA_KV_LORA), blk(n_new, HEAD_SLAB),
                      pl.BlockSpec(memory_space=pl.ANY), pl.BlockSpec(memory_space=pl.ANY)],
            out_specs=blk(rows, MLA_KV_LORA),
            scratch_shapes=[pltpu.VMEM((2, rps, past, MLA_KV_LORA), F32), pltpu.VMEM((2, rps, MLA_ROPE, past), F32),
                            pltpu.SemaphoreType.DMA((2, 2))]),
        out_shape=jax.ShapeDtypeStruct((batch, rows, MLA_KV_LORA), F32),
        compiler_params=_cparams(("arbitrary",)),
        name="paged",
    )(page_table, q3, ql3, nl3, nk3, cache_latent, jnp.swapaxes(cache_k_rope, 2, 3))
    return out.reshape(batch * n_new, MLA_HEADS * MLA_KV_LORA)


def _head_proj_body(o_ref, w_ref, y_ref):
    for h in range(MLA_HEADS):
        y_ref[:, h * MLA_V:(h + 1) * MLA_V] = _dot(
            o_ref[:, h * MLA_KV_LORA:(h + 1) * MLA_KV_LORA].astype(BF16), w_ref[h])


def _head_proj(o_lat, w_uv):
    n = o_lat.shape[0]
    w = w_uv.reshape(MLA_KV_LORA, MLA_HEADS, MLA_V).transpose(1, 0, 2).astype(BF16)
    return pl.pallas_call(
        _head_proj_body, grid=(1,),
        in_specs=[pl.BlockSpec(o_lat.shape, lambda i: (0, 0)), pl.BlockSpec(w.shape, lambda i: (0, 0, 0))],
        out_specs=pl.BlockSpec((n, MLA_HEADS * MLA_V), lambda i: (0, 0)),
        out_shape=jax.ShapeDtypeStruct((n, MLA_HEADS * MLA_V), F32),
        compiler_params=_cparams(("arbitrary",)), name="head_proj",
    )(o_lat, w)


N_REC_HEADS = 4
REC_WIDTH = N_REC_HEADS * HEAD_SLAB


def _split3(x):
    a = x.astype(BF16)
    r = x - a.astype(F32)
    b = r.astype(BF16)
    c = (r - b.astype(F32)).astype(BF16)
    return a, b, c


def _scan_body(*refs, gla, chunk, sub, valid, has_s0, k_dim, bb):
    if gla:
        q_ref, k_ref, v_ref, glow_ref, wg_ref, bg_ref, tri_ref = refs[:7]
        rest = refs[7:]
    else:
        q_ref, k_ref, v_ref, la_ref, l1_ref, tri_ref = refs[:6]
        rest = refs[6:]
    if has_s0:
        s0_ref, _, o_ref, sfin_ref, st_scr = rest
    else:
        _, o_ref, sfin_ref, st_scr = rest
    ci = pl.program_id(1)
    nh = N_REC_HEADS

    @pl.when(ci == 0)
    def _():
        if has_s0:
            for i in range(bb):
                for h in range(nh):
                    s0 = s0_ref[i, h]
                    if k_dim < HEAD_SLAB:
                        s0 = jnp.concatenate([s0, jnp.zeros((HEAD_SLAB - k_dim, s0.shape[1]), F32)], axis=0)
                    st_scr[i * nh + h] = s0.T
        else:
            st_scr[...] = jnp.zeros(st_scr.shape, F32)

    for i in range(bb):
        _scan_chunk(i, q_ref, k_ref, v_ref, refs, o_ref, st_scr, tri_ref,
                    gla=gla, chunk=chunk, sub=sub, valid=valid)

    @pl.when(ci == pl.num_programs(1) - 1)
    def _():
        for i in range(bb):
            for h in range(nh):
                sfin_ref[0, i, h] = st_scr[i * nh + h].T[:k_dim]


def _scan_chunk(row, q_ref, k_ref, v_ref, refs, o_ref, st_scr, tri_ref, *, gla, chunk, sub, valid):
    nh = N_REC_HEADS
    v = v_ref[row]
    if gla:
        glow_ref, wg_ref, bg_ref = refs[3:6]
        q = q_ref[row] * (GLA_DK ** -0.5)
        k = k_ref[row]
        g = _log_sigmoid(_dot(glow_ref[row].astype(BF16), wg_ref[...]) + bg_ref[...]) * (1.0 / GLA_GATE_NORMALIZER)
    else:
        la_ref, l1_ref = refs[3:5]
        xq = q_ref[row]
        q = xq * jax.nn.sigmoid(xq) * (HG_DK ** -0.5)
        a = la_ref[...]
        bb = l1_ref[...] + _log_sigmoid(k_ref[row])
        g = jnp.maximum(a, bb) + jnp.log1p(jnp.exp(-jnp.abs(a - bb)))
        k = 1.0 - jnp.exp(g)
    if valid < chunk:
        live = lax.broadcasted_iota(jnp.int32, (chunk, 1), 0) < valid
        g = jnp.where(live, g, 0.0)
        k = jnp.where(live, k, 0.0)

    tri = tri_ref[...]
    g1, g2, g3 = _split3(g)
    b = _dot(tri, g1) + _dot(tri, g2) + _dot(tri, g3)
    b_end = b[chunk - 1:chunk]
    hs = lambda h: slice(h * HEAD_SLAB, (h + 1) * HEAD_SLAB)
    b2 = b * LOG2_E
    qe = (q * jnp.exp(b)).astype(BF16)
    kd_f = k * jnp.exp(b_end - b)
    kd = kd_f.astype(BF16)
    kd_lo = (kd_f - kd.astype(F32)).astype(BF16)
    vb = v.astype(BF16)
    v_lo = (v - vb.astype(F32)).astype(BF16)

    o_inter = jnp.concatenate(
        [_dot_nt(qe[:, hs(h)], st_scr[row * nh + h].astype(BF16)) for h in range(nh)], axis=1)

    row_in_tile = lax.broadcasted_iota(jnp.int32, (SUBLANES, 1), 0)
    blocks = []
    for i in range(chunk // sub):
        lo = i * sub
        if lo >= valid:
            blocks.append(o_inter[lo:lo + sub])
            continue
        bi, qi_ = b[lo:lo + sub], q[lo:lo + sub]
        blk = o_inter[lo:lo + sub]
        if i > 0:
            r = b[lo - 1:lo]
            qt = (qi_ * jnp.exp(bi - r)).astype(BF16)
            kt = (k[:lo] * jnp.exp(r - b[:lo])).astype(BF16)
            off = []
            for h in range(nh):
                att = _dot_nt(qt[:, hs(h)], kt[:, hs(h)])
                off.append(_dot(att.astype(BF16), vb[:lo, hs(h)]))
            blk = blk + jnp.concatenate(off, axis=1)
        live_rows = min(sub, valid - lo)
        tiles = []
        for r0 in range(0, sub, SUBLANES):
            acc = jnp.zeros((SUBLANES, REC_WIDTH), F32)
            if r0 < live_rows:
                bt = b2[lo + r0:lo + r0 + SUBLANES]
                qt8 = qi_[r0:r0 + SUBLANES]
                rows8 = row_in_tile + r0
                for s in range(min(live_rows, r0 + SUBLANES)):
                    d = bt - b2[lo + s:lo + s + 1]
                    if s > r0:
                        d = jnp.minimum(d, 0.0)
                    w = qt8 * (k[lo + s:lo + s + 1] * jnp.exp2(d))
                    v_s = v[lo + s:lo + s + 1]
                    parts = []
                    for h in range(nh):
                        a_ts = jnp.sum(w[:, hs(h)], axis=-1, keepdims=True)
                        if s > r0:
                            a_ts = jnp.where(rows8 >= s, a_ts, 0.0)
                        parts.append(a_ts * v_s[:, hs(h)])
                    acc = acc + jnp.concatenate(parts, axis=1)
            tiles.append(acc)
        blocks.append(blk + jnp.concatenate(tiles, axis=0))
    o_ref[row] = jnp.concatenate(blocks, axis=0)

    decay = jnp.exp(b_end)
    for h in range(nh):
        upd = (_dot_tn(vb[:, hs(h)], kd[:, hs(h)]) + _dot_tn(vb[:, hs(h)], kd_lo[:, hs(h)])
               + _dot_tn(v_lo[:, hs(h)], kd[:, hs(h)]))
        st_scr[row * nh + h] = st_scr[row * nh + h] * decay[:, hs(h)] + upd


def _scan(z, batch, time, chunk, sub, valid, gla, extra, s0, bb, layer, stacked):
    k_dim = GLA_DK if gla else HG_DK
    nck = time // chunk
    nh = N_REC_HEADS
    z3 = z.reshape(batch, time, Z_WIDTH)
    cb = lambda col: pl.BlockSpec((bb, chunk, REC_WIDTH), lambda b, c: (b, c, col // REC_WIDTH))
    full = lambda a: pl.BlockSpec(a.shape, lambda b, c: (0,) * a.ndim)
    state = pl.BlockSpec((bb, nh, k_dim, HEAD_SLAB), lambda b, c: (b, 0, 0, 0))
    tri = jnp.tril(jnp.ones((chunk, chunk), F32)).astype(BF16)
    if gla:
        wg, bg = extra
        ins = [z3, z3, z3, z3, wg, bg, tri]
        specs = [cb(Z_GQ), cb(Z_GK), cb(Z_GV),
                 pl.BlockSpec((bb, chunk, LANE), lambda b, c: (b, c, Z_GLOW // LANE)), full(wg), full(bg), full(tri)]
    else:
        la, l1 = extra
        ins = [z3, z3, z3, la, l1, tri]
        specs = [cb(Z_HQ), cb(Z_HF), cb(Z_HI), full(la), full(l1), full(tri)]
    if s0 is not None:
        ins.append(s0)
        specs.append(state)
    ins.append(stacked)
    specs.append(pl.BlockSpec(memory_space=pl.ANY))
    o, stacked = pl.pallas_call(
        functools.partial(_scan_body, gla=gla, chunk=chunk, sub=sub, valid=valid, has_s0=s0 is not None,
                          k_dim=k_dim, bb=bb),
        grid=(batch // bb, nck),
        in_specs=specs,
        out_specs=[pl.BlockSpec((bb, chunk, REC_WIDTH), lambda b, c: (b, c, 0)),
                   pl.BlockSpec((1, bb, nh, k_dim, HEAD_SLAB), lambda b, c: (layer, b, 0, 0, 0))],
        out_shape=[jax.ShapeDtypeStruct((batch, time, REC_WIDTH), F32),
                   jax.ShapeDtypeStruct(stacked.shape, F32)],
        input_output_aliases={len(ins) - 1: 1},
        scratch_shapes=[pltpu.VMEM((bb * nh, HEAD_SLAB, HEAD_SLAB), F32)],
        compiler_params=_cparams(("parallel", "arbitrary")),
        name="scan_gla" if gla else "scan_hgrn",
    )(*ins)
    return o.reshape(batch * time, REC_WIDTH), stacked


def _merge_body(x_ref, ym_ref, oh_ref, og_ref, hgate_ref, ggate_ref, br0_ref, br1_ref, br2_ref,
                g1_ref, sc2_ref, sh2_ref, hgw_ref, glw_ref, nfw_ref, wbm_ref, wbh_ref, wbg_ref, wout_ref,
                rwh_ref, rwl_ref, rb_ref, tri_ref, cnt0_ref, x1_ref, h2_ref, te_ref, tw_ref, rk_ref, cnt_ref,
                carry_scr):
    silu = lambda t: t * jax.nn.sigmoid(t)
    yh = _rms(oh_ref[...]) * hgw_ref[...] * silu(hgate_ref[...])
    og = og_ref[...]
    glw = glw_ref[...]
    yg = jnp.concatenate([_rms(og[:, h * GLA_DV:(h + 1) * GLA_DV]) * glw for h in range(GLA_HEADS)], axis=1)
    yg = yg * silu(ggate_ref[...])
    m = (jax.nn.sigmoid(br0_ref[...]) * _dot(ym_ref[...].astype(BF16), wbm_ref[...])
         + jax.nn.sigmoid(br1_ref[...]) * _dot(yh.astype(BF16), wbh_ref[...])
         + jax.nn.sigmoid(br2_ref[...]) * _dot(yg.astype(BF16), wbg_ref[...]))
    x1 = x_ref[...] + g1_ref[0, 0] * _dot(m.astype(BF16), wout_ref[...])
    x1_ref[...] = x1
    h2 = _rms(x1) * nfw_ref[...] * (1.0 + sc2_ref[0, 0]) + sh2_ref[0, 0]
    h2_ref[...] = h2
    hh = h2.astype(BF16)
    hl = (h2 - hh.astype(F32)).astype(BF16)
    rwh = rwh_ref[...]
    logits = _dot(hh, rwh) + _dot(hh, rwl_ref[...]) + _dot(hl, rwh) + rb_ref[...]

    lane = lax.broadcasted_iota(jnp.int32, logits.shape, 1).astype(F32)
    work = jnp.where(lane < N_EXPERTS, logits, NEG_INF)
    vals, idxs, hots = [], [], []
    for _ in range(TOP_K):
        mx = jnp.max(work, axis=1, keepdims=True)
        idx = jnp.min(jnp.where(work == mx, lane, float(LANE)), axis=1, keepdims=True)
        hot = lane == idx
        work = jnp.where(hot, NEG_INF, work)
        vals.append(mx)
        idxs.append(idx)
        hots.append(hot)
    ex = [jnp.exp(v - vals[0]) for v in vals]
    den = ex[0] + ex[1] + ex[2] + ex[3]

    @pl.when(pl.program_id(0) == 0)
    def _():
        carry_scr[...] = cnt0_ref[...]

    cnt = sum(h.astype(F32) for h in hots)
    before = _dot(tri_ref[...], cnt.astype(BF16)) + carry_scr[...]
    te = jnp.zeros(logits.shape, F32)
    tw = jnp.zeros(logits.shape, F32)
    rk = jnp.zeros(logits.shape, F32)
    for k in range(TOP_K):
        sel = lane == float(k)
        te = jnp.where(sel, idxs[k], te)
        tw = jnp.where(sel, ex[k] / den, tw)
        rk = jnp.where(sel, jnp.sum(jnp.where(hots[k], before, 0.0), axis=1, keepdims=True), rk)
    te_ref[...] = te.astype(jnp.int32)
    tw_ref[...] = tw
    rk_ref[...] = rk.astype(jnp.int32)
    carry_scr[...] = carry_scr[...] + jnp.sum(cnt, axis=0, keepdims=True)
    cnt_ref[...] = carry_scr[...]


def _merge(grp, layer, x, y_mla, o_hg, o_gla, z, lw, cnt0):
    tm = grp.tile(256)
    tri = jnp.tril(jnp.ones((tm, tm), F32), k=-1).astype(BF16)
    row = lambda w: pl.BlockSpec((tm, w), lambda i: (i, 0))
    zc = lambda col, w: pl.BlockSpec((tm, w), lambda i: (i, col // w))
    full = lambda a: pl.BlockSpec(a.shape, lambda i: (0,) * a.ndim)
    ws = [lw["hg_norm_w"], lw["gla_norm_w"], lw["norm_ffn_w"], lw["w_br_mla"], lw["w_br_hg"], lw["w_br_gla"],
          lw["w_out"], lw["rw_hi"], lw["rw_lo"], lw["rb"], tri, cnt0]
    n = grp.n_tok
    one = pl.BlockSpec((1, LANE), lambda i: (0, 0))
    return pl.pallas_call(
        _merge_body, grid=(grp.n_tok // tm,),
        in_specs=[row(D_MODEL), row(512), row(512), row(512), zc(Z_HGATE, 512), zc(Z_GGATE, 512),
                  zc(Z_BR, 1024), zc(Z_BR + 1024, 1024), zc(Z_BR + 2048, 1024),
                  grp.mod_spec(layer, MOD_GATE1, tm), grp.mod_spec(layer, MOD_SCALE2, tm),
                  grp.mod_spec(layer, MOD_SHIFT2, tm)] + [full(w) for w in ws],
        out_specs=[row(D_MODEL), row(D_MODEL), row(LANE), row(LANE), row(LANE), one],
        out_shape=[jax.ShapeDtypeStruct((n, D_MODEL), F32), jax.ShapeDtypeStruct((n, D_MODEL), F32),
                   jax.ShapeDtypeStruct((n, LANE), jnp.int32), jax.ShapeDtypeStruct((n, LANE), F32),
                   jax.ShapeDtypeStruct((n, LANE), jnp.int32), jax.ShapeDtypeStruct((1, LANE), F32)],
        scratch_shapes=[pltpu.VMEM((1, LANE), F32)],
        compiler_params=_cparams(("arbitrary",)), name="merge",
    )(x, y_mla, o_hg, o_gla, z, z, z, z, z, grp.mod, grp.mod, grp.mod, *ws)


def _experts_body(be_ref, grp_ref, nxt_ref, nu_ref, x_ref, bgu_ref, bd_ref, wgu_hbm, wd_hbm, y_ref,
                  wgu_buf, wd_buf, wgu_bf, wd_bf, sem, *, layer):
    i = pl.program_id(0)
    e = be_ref[i]
    slot = grp_ref[i] % 2
    first = jnp.logical_or(i == 0, e != be_ref[jnp.maximum(i - 1, 0)])

    def fetch(expert, s):
        w = layer * N_EXPERTS + expert
        return (pltpu.make_async_copy(wgu_hbm.at[w], wgu_buf.at[s], sem.at[0, s]),
                pltpu.make_async_copy(wd_hbm.at[w], wd_buf.at[s], sem.at[1, s]))

    @pl.when(i == 0)
    def _():
        for c in fetch(e, 0):
            c.start()

    @pl.when(first)
    def _():
        for c in fetch(e, slot):
            c.wait()

        @pl.when(nxt_ref[i] >= 0)
        def _():
            for c in fetch(nxt_ref[i], 1 - slot):
                c.start()

        wgu_bf[...] = wgu_buf[slot].astype(BF16)
        wd_bf[...] = wd_buf[slot].astype(BF16)

    @pl.when(i < nu_ref[0])
    def _():
        gu = _dot(x_ref[...].astype(BF16), wgu_bf[...]) + bgu_ref[0]
        gate = jnp.minimum(gu[:, :D_EXPERT], SWIGLU_LIMIT)
        up = jnp.clip(gu[:, D_EXPERT:], -SWIGLU_LIMIT, SWIGLU_LIMIT)
        act = (up + 1.0) * gate * jax.nn.sigmoid(SWIGLU_ALPHA * gate)
        y_ref[...] = _dot(act.astype(BF16), wd_bf[...]) + bd_ref[0]

    @pl.when(i >= nu_ref[0])
    def _():
        y_ref[...] = jnp.zeros(y_ref.shape, F32)


def _experts(block_e, block_grp, block_nxt, n_used, xb, w_gate_up, b_gate_up, w_down, b_down, layer):
    n_slots = xb.shape[0]
    tm = MOE_TM
    ne = N_EXPERTS
    bias = lambda w: pl.BlockSpec((1, 1, w), lambda i, be, grp, nxt, nu: (layer * ne + be[i], 0, 0))
    return pl.pallas_call(
        functools.partial(_experts_body, layer=layer),
        grid_spec=pltpu.PrefetchScalarGridSpec(
            num_scalar_prefetch=4, grid=(n_slots // tm,),
            in_specs=[pl.BlockSpec((tm, D_MODEL), lambda i, be, grp, nxt, nu: (i, 0)),
                      bias(2 * D_EXPERT), bias(D_MODEL),
                      pl.BlockSpec(memory_space=pl.ANY), pl.BlockSpec(memory_space=pl.ANY)],
            out_specs=pl.BlockSpec((tm, D_MODEL), lambda i, be, grp, nxt, nu: (i, 0)),
            scratch_shapes=[pltpu.VMEM((2, D_MODEL, 2 * D_EXPERT), F32), pltpu.VMEM((2, D_EXPERT, D_MODEL), F32),
                            pltpu.VMEM((D_MODEL, 2 * D_EXPERT), BF16), pltpu.VMEM((D_EXPERT, D_MODEL), BF16),
                            pltpu.SemaphoreType.DMA((2, 2))]),
        out_shape=jax.ShapeDtypeStruct((n_slots, D_MODEL), F32),
        compiler_params=_cparams(("arbitrary",)),
        name="experts",
    )(block_e, block_grp, block_nxt, n_used, xb,
      b_gate_up.reshape(DEPTH * ne, 1, 2 * D_EXPERT), b_down.reshape(DEPTH * ne, 1, D_MODEL),
      w_gate_up.reshape(DEPTH * ne, D_MODEL, 2 * D_EXPERT), w_down.reshape(DEPTH * ne, D_EXPERT, D_MODEL))


def _route(top_e, rank, counts):
    n = top_e.shape[0]
    nk = n * TOP_K
    tm = MOE_TM
    experts = jnp.arange(N_EXPERTS, dtype=jnp.int32)
    padded = (counts + tm - 1) // tm * tm
    pad_end = jnp.cumsum(padded)
    pad_start = pad_end - padded
    start = jnp.cumsum(counts) - counts
    dest = jnp.sum(jnp.where(top_e[..., None] == experts, pad_start, 0), axis=-1) + rank
    flat_tok = jnp.arange(nk, dtype=jnp.int32) // TOP_K
    _, stok = lax.sort((dest.reshape(-1), flat_tok), num_keys=1)
    n_blocks = (nk + N_EXPERTS * (tm - 1) + tm - 1) // tm
    block_lo = jnp.arange(n_blocks, dtype=jnp.int32) * tm
    block_e = jnp.minimum(jnp.sum((pad_end[None, :] <= block_lo[:, None]).astype(jnp.int32), axis=1),
                          N_EXPERTS - 1)
    off = (block_lo - pad_start[block_e])[:, None] + jnp.arange(tm, dtype=jnp.int32)[None, :]
    live = off < counts[block_e][:, None]
    src = jnp.clip(start[block_e][:, None] + off, 0, nk - 1)
    slot_tok = jnp.where(live, stok[src.reshape(-1)].reshape(n_blocks, tm), 0).reshape(-1)
    n_used = (pad_end[-1] // tm).astype(jnp.int32).reshape(1)
    block_e = block_e.astype(jnp.int32)
    first = jnp.concatenate([jnp.ones((1,), jnp.bool_), block_e[1:] != block_e[:-1]])
    block_grp = jnp.cumsum(first.astype(jnp.int32)) - 1
    runs = jnp.arange(n_blocks, dtype=jnp.int32)
    run_e = jnp.sum(jnp.where(first[:, None] & (block_grp[:, None] == runs[None, :]), block_e[:, None], 0), axis=0)
    following = jnp.minimum(block_grp + 1, n_blocks - 1)
    block_nxt = jnp.where(block_grp + 1 <= block_grp[-1], run_e[following], -1).astype(jnp.int32)
    return slot_tok, (block_e, block_grp, block_nxt), n_used, dest


def _final_norm_body(x_ref, w_ref, o_ref):
    o_ref[...] = _rms(x_ref[...]) * w_ref[...]


def _final_norm(x, w, tm):
    n = x.shape[0]
    return pl.pallas_call(
        _final_norm_body, grid=(n // tm,),
        in_specs=[pl.BlockSpec((tm, D_MODEL), lambda i: (i, 0)), pl.BlockSpec((1, D_MODEL), lambda i: (0, 0))],
        out_specs=pl.BlockSpec((tm, D_MODEL), lambda i: (i, 0)),
        out_shape=jax.ShapeDtypeStruct((n, D_MODEL), F32),
        compiler_params=_cparams(("parallel",)), name="final_norm",
    )(x, w.reshape(1, D_MODEL))


def kernel(x_prompt, x_sample, cache_latent, cache_k_rope, state_hgrn, state_gla, page_table, c_prompt, c_sample, norm_mix_w, norm_ffn_w, final_norm_w, w_ada, b_ada, w_in, mla_q_norm_w, mla_w_uq, mla_kv_norm_w, mla_w_uk, mla_w_uv, hgrn_lower_bounds, hgrn_norm_w, gla_w_gate, gla_b_gate, gla_norm_w, w_branch_mla, w_branch_hgrn, w_branch_gla, w_out, router_w, router_b, w_gate_up, b_gate_up, w_down, b_down):
    bp, seq, _ = x_prompt.shape
    bs, tnew, _ = x_sample.shape
    n_p, n_s = bp * seq, bs * tnew
    past_len = page_table.shape[1] * cache_latent.shape[2]
    s_pad = SCAN_SUB

    rows = bp + bs
    rows_pad = -(-rows // 8) * 8
    c_all = jnp.pad(jnp.concatenate([c_prompt, c_sample], axis=0), ((0, rows_pad - rows), (0, 0)))
    mod = _adaln(c_all, w_ada, b_ada)
    gp = _Group(bp, seq, per_token=False, mod=mod[:, :bp])
    gs = _Group(bs, tnew, per_token=True, mod=mod[:, bp:rows])

    lbs = jax.nn.softmax(hgrn_lower_bounds.astype(F32), axis=0)
    lbs = jnp.cumsum(lbs, axis=0) - lbs[0]
    log_lb = jnp.log(jnp.maximum(lbs, TINY))
    log_1m_lb = jnp.log1p(-lbs)

    ct_p, st_p = _rope_tables(jnp.arange(seq, dtype=jnp.int32))
    pos_s = past_len + jnp.arange(tnew, dtype=jnp.int32)
    ct_s, st_s = _rope_tables(jnp.tile(pos_s, bs))

    xp = x_prompt.reshape(n_p, D_MODEL)
    xs = x_sample.reshape(n_s, D_MODEL)
    lat_p, kpe_p, lat_s, kpe_s = [], [], [], []
    state_buf = lambda b, k: jnp.zeros((DEPTH, b, N_REC_HEADS, k, HEAD_SLAB), F32)
    hg_p, gla_p, hg_s, gla_s = state_buf(bp, HG_DK), state_buf(bp, GLA_DK), state_buf(bs, HG_DK), state_buf(bs, GLA_DK)

    for l in range(DEPTH):
        w_pad = _pad_w_in(w_in[l])
        mw = _mla_weights(mla_w_uq[l], mla_w_uk[l], mla_w_uv[l])
        rw = jnp.pad(router_w[l], ((0, 0), (0, LANE - N_EXPERTS)))
        rw_hi = rw.astype(BF16)
        lw = dict(hg_norm_w=hgrn_norm_w[l].reshape(1, -1), gla_norm_w=gla_norm_w[l].reshape(1, -1),
                  norm_ffn_w=norm_ffn_w[l].reshape(1, -1), w_br_mla=w_branch_mla[l].astype(BF16),
                  w_br_hg=w_branch_hgrn[l].astype(BF16), w_br_gla=w_branch_gla[l].astype(BF16),
                  w_out=w_out[l].astype(BF16), rw_hi=rw_hi, rw_lo=(rw - rw_hi.astype(F32)).astype(BF16),
                  rb=jnp.pad(router_b[l], (0, LANE - N_EXPERTS)).reshape(1, LANE))
        wg = jnp.pad(gla_w_gate[l].reshape(GLA_GATE_RANK, GLA_HEADS, GLA_DK),
                     ((0, LANE - GLA_GATE_RANK), (0, 0), (0, HEAD_SLAB - GLA_DK))).reshape(LANE, REC_WIDTH).astype(BF16)
        bg = jnp.pad(gla_b_gate[l].reshape(GLA_HEADS, GLA_DK), ((0, 0), (0, HEAD_SLAB - GLA_DK))).reshape(1, REC_WIDTH)
        hg_extra = (log_lb[l].reshape(1, -1), log_1m_lb[l].reshape(1, -1))
        g2p = mod[l, :bp, MOD_GATE2 * D_MODEL:]
        g2s = mod[l, bp:rows, MOD_GATE2 * D_MODEL:]

        zp = _in_proj(gp, l, xp, norm_mix_w[l], w_pad)
        q, k, v, latp, kpep = _mla_prep(gp, zp, mla_q_norm_w[l], mla_kv_norm_w[l], mw, ct_p, st_p, sample=False)
        y_mla_p = _flash(q, k, v, bp, seq, tq=512)
        o_hg_p, hg_p = _scan(zp, bp, seq, SCAN_CHUNK, SCAN_SUB, SCAN_CHUNK, False, hg_extra, None, bp, l, hg_p)
        o_gl_p, gla_p = _scan(zp, bp, seq, SCAN_CHUNK, SCAN_SUB, SCAN_CHUNK, True, (wg, bg), None, bp, l, gla_p)
        x1p, h2p, te_p, tw_p, rk_p, cnt_p = _merge(gp, l, xp, y_mla_p, o_hg_p, o_gl_p, zp, lw,
                                                   jnp.zeros((1, LANE), F32))

        zs = _in_proj(gs, l, xs, norm_mix_w[l], w_pad)
        qs, qlat, lats, kpes = _mla_prep(gs, zs, mla_q_norm_w[l], mla_kv_norm_w[l], mw, ct_s, st_s, sample=True)
        o_lat = _paged(page_table, qs, qlat, lats, kpes, cache_latent, cache_k_rope, l, bs, tnew)
        y_mla_s = _head_proj(o_lat, mla_w_uv[l])
        zs_pad = jnp.pad(zs.reshape(bs, tnew, Z_WIDTH), ((0, 0), (0, s_pad - tnew), (0, 0))).reshape(bs * s_pad, Z_WIDTH)
        unpad = lambda o: o.reshape(bs, s_pad, -1)[:, :tnew].reshape(n_s, -1)
        o_hg_s, hg_s = _scan(zs_pad, bs, s_pad, s_pad, SCAN_SUB, tnew, False, hg_extra, state_hgrn[l],
                             SAMPLE_ROWS_PER_STEP, l, hg_s)
        o_gl_s, gla_s = _scan(zs_pad, bs, s_pad, s_pad, SCAN_SUB, tnew, True, (wg, bg), state_gla[l],
                              SAMPLE_ROWS_PER_STEP, l, gla_s)
        x1s, h2s, te_s, tw_s, rk_s, cnt_all = _merge(gs, l, xs, y_mla_s, unpad(o_hg_s), unpad(o_gl_s), zs, lw, cnt_p)

        h2 = jnp.concatenate([h2p, h2s], axis=0)
        both = lambda a, b: jnp.concatenate([a, b], axis=0)[:, :TOP_K]
        top_w = both(tw_p, tw_s)
        slot_tok, block_e, n_used, dest = _route(both(te_p, te_s), both(rk_p, rk_s),
                                                 cnt_all[0, :N_EXPERTS].astype(jnp.int32))
        yb = _experts(*block_e, n_used, h2[slot_tok], w_gate_up, b_gate_up, w_down, b_down, l)
        combine = lambda w, d: sum(w[:, k:k + 1] * yb[d[:, k]] for k in range(TOP_K))
        xp = x1p + jnp.repeat(g2p, seq, axis=0) * combine(top_w[:n_p], dest[:n_p])
        xs = x1s + jnp.repeat(g2s, tnew, axis=0) * combine(top_w[n_p:], dest[n_p:])

        sl = slice(MLA_NOPE, MLA_NOPE + MLA_ROPE)
        lat_p.append(latp.reshape(bp, seq, -1)); kpe_p.append(kpep[:, sl].reshape(bp, seq, -1))
        lat_s.append(lats.reshape(bs, tnew, -1)); kpe_s.append(kpes[:, sl].reshape(bs, tnew, -1))

    y_prompt = _final_norm(xp, final_norm_w, 1024).reshape(bp, seq, D_MODEL)
    y_sample = _final_norm(xs, final_norm_w, n_s).reshape(bs, tnew, D_MODEL)
    st = jnp.stack
    return (y_prompt, y_sample, st(lat_p), st(kpe_p), hg_p, gla_p, st(lat_s), st(kpe_s), hg_s, gla_s)
```

```python
import functools

import numpy as np
import jax
import jax.numpy as jnp
from jax import lax
from jax.experimental import pallas as pl
from jax.experimental.pallas import tpu as pltpu

F32 = jnp.float32
BF16 = jnp.bfloat16

D_MODEL = 1024
DEPTH = 2
PAGE_SIZE = 128
MLA_HEADS = 8
MLA_NOPE = 64
MLA_ROPE = 32
MLA_V = 64
MLA_Q_LORA = 384
MLA_KV_LORA = 256
MLA_SCALE = (MLA_NOPE + MLA_ROPE) ** -0.5
ROPE_BASE = 10000.0
HG_HEADS = 4
HG_DK = 128
HG_DV = 128
GLA_HEADS = 4
GLA_DK = 64
GLA_DV = 128
GLA_GATE_RANK = 16
GLA_GATE_NORMALIZER = 16.0
N_EXPERTS = 32
TOP_K = 4
D_EXPERT = 1024
SWIGLU_LIMIT = 7.0
SWIGLU_ALPHA = 1.702
EPS = 1e-6
NEG_INF = -1e30
TINY = 1e-30
LOG2_E = 1.4426950408889634

IN_SPLITS = (MLA_Q_LORA, MLA_KV_LORA, MLA_ROPE, 512, 512, 512, 512, 256, 256, 512, 512, GLA_GATE_RANK,
             3 * D_MODEL)

LANE = 128
SUBLANES = 8
HEAD_SLAB = 128
VMEM_LIMIT = 56 * 1024 * 1024

Z_WIDTH = 8192
Z_CQ, Z_KR, Z_CKV, Z_KRR, Z_GLOW = 0, 384, 512, 768, 896
Z_HQ, Z_HF, Z_HI, Z_HGATE = 1024, 1536, 2048, 2560
Z_GQ, Z_GK, Z_GV, Z_GGATE, Z_BR = 3072, 3584, 4096, 4608, 5120

SCAN_CHUNK = 32
SCAN_SUB = 16
MOE_TM = 256
SAMPLE_ROWS_PER_STEP = 8
PAGED_ROWS_PER_STEP = 2


def _cparams(sem, vmem=VMEM_LIMIT):
    return pltpu.CompilerParams(dimension_semantics=sem, vmem_limit_bytes=vmem)


def _dot(a, b):
    return jnp.dot(a, b, preferred_element_type=F32)


def _dot_nt(a, b):
    return lax.dot_general(a, b, (((1,), (1,)), ((), ())), preferred_element_type=F32)


def _dot_tn(a, b):
    return lax.dot_general(a, b, (((0,), (0,)), ((), ())), preferred_element_type=F32)


def _rms(x):
    return x * lax.rsqrt(jnp.mean(x * x, axis=-1, keepdims=True) + EPS)


def _log_sigmoid(x):
    return jnp.minimum(x, 0.0) - jnp.log1p(jnp.exp(-jnp.abs(x)))


def _adaln_body(c_ref, w_ref, b_ref, o_ref):
    c = c_ref[...]
    a = (c * jax.nn.sigmoid(c)).astype(BF16)
    o_ref[0] = _dot(a, w_ref[0].astype(BF16)) + b_ref[0]


def _adaln(c_all, w_ada, b_ada):
    rows = c_all.shape[0]
    tn = 1536
    n_out = w_ada.shape[-1]
    return pl.pallas_call(
        _adaln_body,
        grid=(DEPTH, n_out // tn),
        in_specs=[pl.BlockSpec((rows, D_MODEL), lambda l, j: (0, 0)),
                  pl.BlockSpec((1, D_MODEL, tn), lambda l, j: (l, 0, j)),
                  pl.BlockSpec((1, 1, tn), lambda l, j: (l, 0, j))],
        out_specs=pl.BlockSpec((1, rows, tn), lambda l, j: (l, 0, j)),
        out_shape=jax.ShapeDtypeStruct((DEPTH, rows, n_out), F32),
        compiler_params=_cparams(("parallel", "parallel")),
        name="adaln",
    )(c_all, w_ada, b_ada.reshape(DEPTH, 1, n_out))


MOD_SHIFT1, MOD_SCALE1, MOD_GATE1, MOD_SHIFT2, MOD_SCALE2, MOD_GATE2 = range(6)


class _Group:
    def __init__(self, batch, time, per_token, mod):
        self.batch, self.time, self.per_token = batch, time, per_token
        self.n_tok = batch * time
        if per_token:
            self.mod = jnp.repeat(mod, time, axis=1)[:, None]
        else:
            self.mod = mod[:, :, None, :]

    def tile(self, tm):
        tm = min(tm, self.n_tok)
        assert (self.n_tok if self.per_token else self.time) % tm == 0
        return tm

    def mod_spec(self, layer, col, tm):
        if self.per_token:
            return pl.BlockSpec((1, 1, tm, D_MODEL), lambda *g: (layer, 0, g[0], col))
        per = self.time // tm
        return pl.BlockSpec((1, 1, 1, D_MODEL), lambda *g: (layer, g[0] // per, 0, col))


def _in_proj_body(x_ref, nw_ref, sc_ref, sh_ref, w_ref, z_ref, h_scr):
    @pl.when(pl.program_id(1) == 0)
    def _():
        h = _rms(x_ref[...]) * nw_ref[...]
        h_scr[...] = (h * (1.0 + sc_ref[0, 0]) + sh_ref[0, 0]).astype(BF16)

    z_ref[...] = _dot(h_scr[...], w_ref[...])


def _in_proj(grp, layer, x, norm_w, w_pad):
    tm, tn = grp.tile(1024), 2048
    return pl.pallas_call(
        _in_proj_body,
        grid=(grp.n_tok // tm, Z_WIDTH // tn),
        in_specs=[pl.BlockSpec((tm, D_MODEL), lambda i, j: (i, 0)),
                  pl.BlockSpec((1, D_MODEL), lambda i, j: (0, 0)),
                  grp.mod_spec(layer, MOD_SCALE1, tm), grp.mod_spec(layer, MOD_SHIFT1, tm),
                  pl.BlockSpec((D_MODEL, tn), lambda i, j: (0, j))],
        out_specs=pl.BlockSpec((tm, tn), lambda i, j: (i, j)),
        out_shape=jax.ShapeDtypeStruct((grp.n_tok, Z_WIDTH), F32),
        scratch_shapes=[pltpu.VMEM((tm, D_MODEL), BF16)],
        compiler_params=_cparams(("parallel", "arbitrary")),
        name="in_proj",
    )(x, norm_w.reshape(1, D_MODEL), grp.mod, grp.mod, w_pad)


def _pad_w_in(w):
    idx = np.cumsum(IN_SPLITS)[:-1].tolist()
    cq, ckv, kr, hq, hf, hi, hgate, gq, gk, gv, ggate, glow, br = jnp.split(w, idx, axis=1)
    zeros = lambda n: jnp.zeros((w.shape[0], n), w.dtype)
    half = MLA_ROPE // 2
    kr_rot = jnp.concatenate([-kr[:, half:], kr[:, :half]], axis=1)
    slab = lambda a: jnp.concatenate([zeros(MLA_NOPE), a, zeros(HEAD_SLAB - MLA_NOPE - MLA_ROPE)], axis=1)
    pad_heads = lambda a: jnp.pad(a.reshape(-1, GLA_HEADS, GLA_DK),
                                  ((0, 0), (0, 0), (0, HEAD_SLAB - GLA_DK))).reshape(-1, GLA_HEADS * HEAD_SLAB)
    glow_slab = jnp.concatenate([glow, zeros(LANE - GLA_GATE_RANK)], axis=1)
    out = jnp.concatenate([cq, slab(kr), ckv, slab(kr_rot), glow_slab, hq, hf, hi, hgate,
                           pad_heads(gq), pad_heads(gk), gv, ggate, br], axis=1)
    assert out.shape[1] == Z_WIDTH
    return out.astype(BF16)


def _mla_weights(w_uq, w_uk, w_uv):
    hd = MLA_NOPE + MLA_ROPE
    half = MLA_ROPE // 2
    q = w_uq.reshape(MLA_Q_LORA, MLA_HEADS, hd)
    nope, pe = q[..., :MLA_NOPE], q[..., MLA_NOPE:]
    pe_rot = jnp.concatenate([-pe[..., half:], pe[..., :half]], axis=-1)
    z = lambda n: jnp.zeros((MLA_Q_LORA, MLA_HEADS, n), w_uq.dtype)
    wa = jnp.concatenate([nope, pe, z(HEAD_SLAB - hd)], axis=-1).reshape(MLA_Q_LORA, -1)
    wb = jnp.concatenate([z(MLA_NOPE), pe_rot, z(HEAD_SLAB - hd)], axis=-1).reshape(MLA_Q_LORA, -1)
    k = w_uk.reshape(MLA_KV_LORA, MLA_HEADS, MLA_NOPE)
    wka = jnp.pad(k, ((0, 0), (0, 0), (0, HEAD_SLAB - MLA_NOPE))).reshape(MLA_KV_LORA, -1)
    wukt = jnp.pad(k.transpose(1, 2, 0), ((0, 0), (0, HEAD_SLAB - MLA_NOPE), (0, 0)))
    return wa.astype(BF16), wb.astype(BF16), wka.astype(BF16), w_uv.astype(BF16), wukt.astype(BF16)


def _rope_tables(pos):
    half = MLA_ROPE // 2
    inv = ROPE_BASE ** (-jnp.arange(half, dtype=F32) / half)
    ang = pos.astype(F32)[:, None] * inv[None, :]
    cos, sin = jnp.cos(ang), jnp.sin(ang)
    n = pos.shape[0]
    tail = jnp.zeros((n, HEAD_SLAB - MLA_NOPE - MLA_ROPE), F32)
    ct = jnp.concatenate([jnp.ones((n, MLA_NOPE), F32), cos, cos, tail], axis=1)
    st = jnp.concatenate([jnp.zeros((n, MLA_NOPE), F32), sin, sin, tail], axis=1)
    return ct, st


def _mla_common(z_ref, qnw_ref, kvnw_ref, wa_ref, wb_ref, ct_ref, st_ref):
    z = z_ref[...]
    qn = (_rms(z[:, Z_CQ:Z_CQ + MLA_Q_LORA]) * qnw_ref[...]).astype(BF16)
    lat = _rms(z[:, Z_CKV:Z_CKV + MLA_KV_LORA]) * kvnw_ref[...]
    ct, st = ct_ref[...], st_ref[...]
    ct8 = jnp.concatenate([ct] * MLA_HEADS, axis=1)
    st8 = jnp.concatenate([st] * MLA_HEADS, axis=1)
    q_cat = (_dot(qn, wa_ref[...]) * ct8 + _dot(qn, wb_ref[...]) * st8) * (MLA_SCALE * LOG2_E)
    kpe = z[:, Z_KR:Z_KR + HEAD_SLAB] * ct + z[:, Z_KRR:Z_KRR + HEAD_SLAB] * st
    return q_cat, lat, kpe


def _mla_prep_prompt_body(z_ref, qnw_ref, kvnw_ref, wa_ref, wb_ref, ct_ref, st_ref, wka_ref, wv_ref,
                          q_ref, k_ref, v_ref, lat_ref, kpe_ref):
    q_cat, lat, kpe = _mla_common(z_ref, qnw_ref, kvnw_ref, wa_ref, wb_ref, ct_ref, st_ref)
    q_ref[...] = q_cat.astype(BF16)
    lat_ref[...] = lat
    kpe_ref[...] = kpe
    lb = lat.astype(BF16)
    k_ref[...] = (_dot(lb, wka_ref[...]) + jnp.concatenate([kpe] * MLA_HEADS, axis=1)).astype(BF16)
    v_ref[...] = _dot(lb, wv_ref[...]).astype(BF16)


def _mla_prep_sample_body(z_ref, qnw_ref, kvnw_ref, wa_ref, wb_ref, ct_ref, st_ref, wukt_ref,
                          q_ref, qlat_ref, lat_ref, kpe_ref):
    q_cat, lat, kpe = _mla_common(z_ref, qnw_ref, kvnw_ref, wa_ref, wb_ref, ct_ref, st_ref)
    qb = q_cat.astype(BF16)
    q_ref[...] = qb
    lat_ref[...] = lat
    kpe_ref[...] = kpe
    for h in range(MLA_HEADS):
        qlat_ref[:, h * MLA_KV_LORA:(h + 1) * MLA_KV_LORA] = _dot(
            qb[:, h * HEAD_SLAB:(h + 1) * HEAD_SLAB], wukt_ref[h]).astype(BF16)


def _mla_prep(grp, z, q_norm_w, kv_norm_w, mw, ct, st, sample):
    wa, wb, wka, wv, wukt = mw
    tm = grp.tile(512)
    n_tiles = grp.n_tok // tm
    hw = MLA_HEADS * HEAD_SLAB
    full = lambda a: pl.BlockSpec(a.shape, lambda i: (0,) * a.ndim)
    row = lambda w: pl.BlockSpec((tm, w), lambda i: (i, 0))
    if grp.per_token:
        tab = pl.BlockSpec((tm, HEAD_SLAB), lambda i: (i, 0))
    else:
        per = grp.time // tm
        tab = pl.BlockSpec((tm, HEAD_SLAB), lambda i: (i % per, 0))
    qnw = q_norm_w.reshape(1, -1)
    kvnw = kv_norm_w.reshape(1, -1)
    common_in = [pl.BlockSpec((tm, 1024), lambda i: (i, 0)), full(qnw), full(kvnw), full(wa), full(wb), tab, tab]
    n = grp.n_tok
    if sample:
        return pl.pallas_call(
            _mla_prep_sample_body, grid=(n_tiles,),
            in_specs=common_in + [full(wukt)],
            out_specs=[row(hw), row(MLA_HEADS * MLA_KV_LORA), row(MLA_KV_LORA), row(HEAD_SLAB)],
            out_shape=[jax.ShapeDtypeStruct((n, hw), BF16),
                       jax.ShapeDtypeStruct((n, MLA_HEADS * MLA_KV_LORA), BF16),
                       jax.ShapeDtypeStruct((n, MLA_KV_LORA), F32),
                       jax.ShapeDtypeStruct((n, HEAD_SLAB), F32)],
            compiler_params=_cparams(("parallel",)), name="mla_prep_sample",
        )(z, qnw, kvnw, wa, wb, ct, st, wukt)
    return pl.pallas_call(
        _mla_prep_prompt_body, grid=(n_tiles,),
        in_specs=common_in + [full(wka), full(wv)],
        out_specs=[row(hw), row(hw), row(MLA_HEADS * MLA_V), row(MLA_KV_LORA), row(HEAD_SLAB)],
        out_shape=[jax.ShapeDtypeStruct((n, hw), BF16),
                   jax.ShapeDtypeStruct((n, hw), BF16),
                   jax.ShapeDtypeStruct((n, MLA_HEADS * MLA_V), BF16),
                   jax.ShapeDtypeStruct((n, MLA_KV_LORA), F32),
                   jax.ShapeDtypeStruct((n, HEAD_SLAB), F32)],
        compiler_params=_cparams(("parallel",)), name="mla_prep_prompt",
    )(z, qnw, kvnw, wa, wb, ct, st, wka, wv)


def _flash_body(qi_ref, ki_ref, q_ref, k_ref, v_ref, o_ref, m_scr, l_scr, acc_scr, *, tq):
    step_id = pl.program_id(1)
    qi, ki = qi_ref[step_id], ki_ref[step_id]

    @pl.when(ki == 0)
    def _():
        m_scr[...] = jnp.full(m_scr.shape, NEG_INF, F32)
        l_scr[...] = jnp.zeros(l_scr.shape, F32)
        acc_scr[...] = jnp.zeros(acc_scr.shape, F32)

    low = lax.broadcasted_iota(jnp.int32, (tq, LANE), 1) < MLA_V

    def step(masked):
        if masked:
            keep = (lax.broadcasted_iota(jnp.int32, (tq, tq), 1)
                    <= lax.broadcasted_iota(jnp.int32, (tq, tq), 0))
        for hp in range(MLA_HEADS // 2):
            pv, al = [], []
            for e in range(2):
                h = 2 * hp + e
                s = _dot_nt(q_ref[0, :, h * HEAD_SLAB:(h + 1) * HEAD_SLAB],
                            k_ref[0, :, h * HEAD_SLAB:(h + 1) * HEAD_SLAB])
                if masked:
                    s = jnp.where(keep, s, NEG_INF)
                m_prev = m_scr[h]
                m_new = jnp.maximum(m_prev, jnp.max(s, axis=-1, keepdims=True))
                alpha = jnp.exp2(m_prev - m_new)
                p = jnp.exp2(s - jnp.concatenate([m_new] * (tq // LANE), axis=1))
                l_scr[h] = alpha * l_scr[h] + jnp.sum(p, axis=-1, keepdims=True)
                m_scr[h] = m_new
                pv.append(_dot(p.astype(BF16), v_ref[0, :, hp * LANE:(hp + 1) * LANE]))
                al.append(alpha)
            sl = slice(hp * LANE, (hp + 1) * LANE)
            acc_scr[:, sl] = jnp.where(low, al[0], al[1]) * acc_scr[:, sl] + jnp.where(low, pv[0], pv[1])

    @pl.when(ki < qi)
    def _():
        step(False)

    @pl.when(ki == qi)
    def _():
        step(True)

        for hp in range(MLA_HEADS // 2):
            sl = slice(hp * LANE, (hp + 1) * LANE)
            o_ref[0, :, sl] = acc_scr[:, sl] / jnp.where(low, l_scr[2 * hp], l_scr[2 * hp + 1])


def _flash(q, k, v, batch, seq, tq):
    hw = MLA_HEADS * HEAD_SLAB
    vw = MLA_HEADS * MLA_V
    nq = seq // tq
    q3, k3, v3 = q.reshape(batch, seq, hw), k.reshape(batch, seq, hw), v.reshape(batch, seq, vw)
    pairs = [(i, j) for i in range(nq) for j in range(i + 1)]
    qi_tab = jnp.asarray([p[0] for p in pairs], jnp.int32)
    ki_tab = jnp.asarray([p[1] for p in pairs], jnp.int32)
    out = pl.pallas_call(
        functools.partial(_flash_body, tq=tq),
        grid_spec=pltpu.PrefetchScalarGridSpec(
            num_scalar_prefetch=2, grid=(batch, len(pairs)),
            in_specs=[pl.BlockSpec((1, tq, hw), lambda b, s, qt, kt: (b, qt[s], 0)),
                      pl.BlockSpec((1, tq, hw), lambda b, s, qt, kt: (b, kt[s], 0)),
                      pl.BlockSpec((1, tq, vw), lambda b, s, qt, kt: (b, kt[s], 0))],
            out_specs=pl.BlockSpec((1, tq, vw), lambda b, s, qt, kt: (b, qt[s], 0)),
            scratch_shapes=[pltpu.VMEM((MLA_HEADS, tq, LANE), F32), pltpu.VMEM((MLA_HEADS, tq, LANE), F32),
                            pltpu.VMEM((tq, vw), F32)]),
        out_shape=jax.ShapeDtypeStruct((batch, seq, vw), F32),
        compiler_params=_cparams(("parallel", "arbitrary")),
        name="flash",
    )(qi_tab, ki_tab, q3, k3, v3)
    return out.reshape(batch * seq, vw)


def _paged_body(pt_ref, q_ref, qlat_ref, nlat_ref, nkpe_ref, lat_hbm, kpe_hbm, o_ref,
                lat_buf, kpe_buf, sem, *, layer, n_pages, n_new, rows_per_step):
    g = pl.program_id(0)
    ng = pl.num_programs(0)
    rows = q_ref.shape[1]
    rps = rows_per_step

    def copies(step, slot):
        out = []
        for r in range(rps):
            for j in range(n_pages):
                pg = pt_ref[step * rps + r, j]
                dst = pl.ds(j * PAGE_SIZE, PAGE_SIZE)
                out.append(pltpu.make_async_copy(lat_hbm.at[layer, pg], lat_buf.at[slot, r, dst], sem.at[0, slot]))
                out.append(pltpu.make_async_copy(kpe_hbm.at[layer, pg], kpe_buf.at[slot, r, :, dst], sem.at[1, slot]))
        return out

    @pl.when(g == 0)
    def _():
        for c in copies(0, 0):
            c.start()

    slot = g % 2

    @pl.when(g + 1 < ng)
    def _():
        for c in copies(g + 1, 1 - slot):
            c.start()

    for c in copies(g, slot):
        c.wait()

    for r in range(rps):
        qlat = qlat_ref[r].astype(F32)
        qpe = q_ref[r][:, MLA_NOPE:MLA_NOPE + MLA_ROPE].astype(F32)
        s_past = _dot_nt(qlat, lat_buf[slot, r]) + _dot(qpe, kpe_buf[slot, r])
        nlat = nlat_ref[r].astype(BF16)
        nkpe = nkpe_ref[r][:, MLA_NOPE:MLA_NOPE + MLA_ROPE].astype(BF16)
        s_new = _dot_nt(qlat_ref[r], nlat) + _dot_nt(q_ref[r][:, MLA_NOPE:MLA_NOPE + MLA_ROPE], nkpe)
        t_of_row = lax.broadcasted_iota(jnp.int32, (rows, n_new), 0) // MLA_HEADS
        s_new = jnp.where(lax.broadcasted_iota(jnp.int32, (rows, n_new), 1) <= t_of_row, s_new, NEG_INF)
        m = jnp.maximum(jnp.max(s_past, axis=-1, keepdims=True), jnp.max(s_new, axis=-1, keepdims=True))
        p_past = jnp.exp2(s_past - m)
        p_new = jnp.exp2(s_new - m)
        denom = jnp.sum(p_past, axis=-1, keepdims=True) + jnp.sum(p_new, axis=-1, keepdims=True)
        o = _dot(p_past.astype(BF16).astype(F32), lat_buf[slot, r]) + _dot(p_new.astype(BF16), nlat)
        o_ref[r] = o / denom


def _paged(page_table, q_cat, q_lat, lat_new, kpe_new, cache_latent, cache_k_rope, layer, batch, n_new):
    rows = n_new * MLA_HEADS
    n_pages = page_table.shape[1]
    past = n_pages * PAGE_SIZE
    rps = PAGED_ROWS_PER_STEP
    q3 = q_cat.reshape(batch, rows, HEAD_SLAB)
    ql3 = q_lat.reshape(batch, rows, MLA_KV_LORA)
    nl3 = lat_new.reshape(batch, n_new, MLA_KV_LORA)
    nk3 = kpe_new.reshape(batch, n_new, HEAD_SLAB)
    blk = lambda r, w: pl.BlockSpec((rps, r, w), lambda b, pt: (b, 0, 0))
    out = pl.pallas_call(
        functools.partial(_paged_body, layer=layer, n_pages=n_pages, n_new=n_new, rows_per_step=rps),
        grid_spec=pltpu.PrefetchScalarGridSpec(
            num_scalar_prefetch=1, grid=(batch // rps,),
            in_specs=[blk(rows, HEAD_SLAB), blk(rows, MLA_KV_LORA), blk(n_new, MLA_KV_LORA), blk(n_new, HEAD_SLAB),
                      pl.BlockSpec(memory_space=pl.ANY), pl.BlockSpec(memory_space=pl.ANY)],
            out_specs=blk(rows, MLA_KV_LORA),
            scratch_shapes=[pltpu.VMEM((2, rps, past, MLA_KV_LORA), F32), pltpu.VMEM((2, rps, MLA_ROPE, past), F32),
                            pltpu.SemaphoreType.DMA((2, 2))]),
        out_shape=jax.ShapeDtypeStruct((batch, rows, MLA_KV_LORA), F32),
        compiler_params=_cparams(("arbitrary",)),
        name="paged",
    )(page_table, q3, ql3, nl3, nk3, cache_latent, jnp.swapaxes(cache_k_rope, 2, 3))
    return out.reshape(batch * n_new, MLA_HEADS * MLA_KV_LORA)


def _head_proj_body(o_ref, w_ref, y_ref):
    for h in range(MLA_HEADS):
        y_ref[:, h * MLA_V:(h + 1) * MLA_V] = _dot(
            o_ref[:, h * MLA_KV_LORA:(h + 1) * MLA_KV_LORA].astype(BF16), w_ref[h])


def _head_proj(o_lat, w_uv):
    n = o_lat.shape[0]
    w = w_uv.reshape(MLA_KV_LORA, MLA_HEADS, MLA_V).transpose(1, 0, 2).astype(BF16)
    return pl.pallas_call(
        _head_proj_body, grid=(1,),
        in_specs=[pl.BlockSpec(o_lat.shape, lambda i: (0, 0)), pl.BlockSpec(w.shape, lambda i: (0, 0, 0))],
        out_specs=pl.BlockSpec((n, MLA_HEADS * MLA_V), lambda i: (0, 0)),
        out_shape=jax.ShapeDtypeStruct((n, MLA_HEADS * MLA_V), F32),
        compiler_params=_cparams(("arbitrary",)), name="head_proj",
    )(o_lat, w)


N_REC_HEADS = 4
REC_WIDTH = N_REC_HEADS * HEAD_SLAB


def _split3(x):
    a = x.astype(BF16)
    r = x - a.astype(F32)
    b = r.astype(BF16)
    c = (r - b.astype(F32)).astype(BF16)
    return a, b, c


def _scan_body(*refs, gla, chunk, sub, valid, has_s0, k_dim, bb):
    if gla:
        q_ref, k_ref, v_ref, glow_ref, wg_ref, bg_ref, tri_ref = refs[:7]
        rest = refs[7:]
    else:
        q_ref, k_ref, v_ref, la_ref, l1_ref, tri_ref = refs[:6]
        rest = refs[6:]
    if has_s0:
        s0_ref, _, o_ref, sfin_ref, st_scr = rest
    else:
        _, o_ref, sfin_ref, st_scr = rest
    ci = pl.program_id(1)
    nh = N_REC_HEADS

    @pl.when(ci == 0)
    def _():
        if has_s0:
            for i in range(bb):
                for h in range(nh):
                    s0 = s0_ref[i, h]
                    if k_dim < HEAD_SLAB:
                        s0 = jnp.concatenate([s0, jnp.zeros((HEAD_SLAB - k_dim, s0.shape[1]), F32)], axis=0)
                    st_scr[i * nh + h] = s0.T
        else:
            st_scr[...] = jnp.zeros(st_scr.shape, F32)

    for i in range(bb):
        _scan_chunk(i, q_ref, k_ref, v_ref, refs, o_ref, st_scr, tri_ref,
                    gla=gla, chunk=chunk, sub=sub, valid=valid)

    @pl.when(ci == pl.num_programs(1) - 1)
    def _():
        for i in range(bb):
            for h in range(nh):
                sfin_ref[0, i, h] = st_scr[i * nh + h].T[:k_dim]


def _scan_chunk(row, q_ref, k_ref, v_ref, refs, o_ref, st_scr, tri_ref, *, gla, chunk, sub, valid):
    nh = N_REC_HEADS
    v = v_ref[row]
    if gla:
        glow_ref, wg_ref, bg_ref = refs[3:6]
        q = q_ref[row] * (GLA_DK ** -0.5)
        k = k_ref[row]
        g = _log_sigmoid(_dot(glow_ref[row].astype(BF16), wg_ref[...]) + bg_ref[...]) * (1.0 / GLA_GATE_NORMALIZER)
    else:
        la_ref, l1_ref = refs[3:5]
        xq = q_ref[row]
        q = xq * jax.nn.sigmoid(xq) * (HG_DK ** -0.5)
        a = la_ref[...]
        bb = l1_ref[...] + _log_sigmoid(k_ref[row])
        g = jnp.maximum(a, bb) + jnp.log1p(jnp.exp(-jnp.abs(a - bb)))
        k = 1.0 - jnp.exp(g)
    if valid < chunk:
        live = lax.broadcasted_iota(jnp.int32, (chunk, 1), 0) < valid
        g = jnp.where(live, g, 0.0)
        k = jnp.where(live, k, 0.0)

    tri = tri_ref[...]
    g1, g2, g3 = _split3(g)
    b = _dot(tri, g1) + _dot(tri, g2) + _dot(tri, g3)
    b_end = b[chunk - 1:chunk]
    hs = lambda h: slice(h * HEAD_SLAB, (h + 1) * HEAD_SLAB)
    b2 = b * LOG2_E
    qe = (q * jnp.exp(b)).astype(BF16)
    kd_f = k * jnp.exp(b_end - b)
    kd = kd_f.astype(BF16)
    kd_lo = (kd_f - kd.astype(F32)).astype(BF16)
    vb = v.astype(BF16)
    v_lo = (v - vb.astype(F32)).astype(BF16)

    o_inter = jnp.concatenate(
        [_dot_nt(qe[:, hs(h)], st_scr[row * nh + h].astype(BF16)) for h in range(nh)], axis=1)

    row_in_tile = lax.broadcasted_iota(jnp.int32, (SUBLANES, 1), 0)
    blocks = []
    for i in range(chunk // sub):
        lo = i * sub
        if lo >= valid:
            blocks.append(o_inter[lo:lo + sub])
            continue
        bi, qi_ = b[lo:lo + sub], q[lo:lo + sub]
        blk = o_inter[lo:lo + sub]
        if i > 0:
            r = b[lo - 1:lo]
            qt = (qi_ * jnp.exp(bi - r)).astype(BF16)
            kt = (k[:lo] * jnp.exp(r - b[:lo])).astype(BF16)
            off = []
            for h in range(nh):
                att = _dot_nt(qt[:, hs(h)], kt[:, hs(h)])
                off.append(_dot(att.astype(BF16), vb[:lo, hs(h)]))
            blk = blk + jnp.concatenate(off, axis=1)
        live_rows = min(sub, valid - lo)
        tiles = []
        for r0 in range(0, sub, SUBLANES):
            acc = jnp.zeros((SUBLANES, REC_WIDTH), F32)
            if r0 < live_rows:
                bt = b2[lo + r0:lo + r0 + SUBLANES]
                qt8 = qi_[r0:r0 + SUBLANES]
                rows8 = row_in_tile + r0
                for s in range(min(live_rows, r0 + SUBLANES)):
                    d = bt - b2[lo + s:lo + s + 1]
                    if s > r0:
                        d = jnp.minimum(d, 0.0)
                    w = qt8 * (k[lo + s:lo + s + 1] * jnp.exp2(d))
                    v_s = v[lo + s:lo + s + 1]
                    parts = []
                    for h in range(nh):
                        a_ts = jnp.sum(w[:, hs(h)], axis=-1, keepdims=True)
                        if s > r0:
                            a_ts = jnp.where(rows8 >= s, a_ts, 0.0)
                        parts.append(a_ts * v_s[:, hs(h)])
                    acc = acc + jnp.concatenate(parts, axis=1)
            tiles.append(acc)
        blocks.append(blk + jnp.concatenate(tiles, axis=0))
    o_ref[row] = jnp.concatenate(blocks, axis=0)

    decay = jnp.exp(b_end)
    for h in range(nh):
        upd = (_dot_tn(vb[:, hs(h)], kd[:, hs(h)]) + _dot_tn(vb[:, hs(h)], kd_lo[:, hs(h)])
               + _dot_tn(v_lo[:, hs(h)], kd[:, hs(h)]))
        st_scr[row * nh + h] = st_scr[row * nh + h] * decay[:, hs(h)] + upd


def _scan(z, batch, time, chunk, sub, valid, gla, extra, s0, bb, layer, stacked):
    k_dim = GLA_DK if gla else HG_DK
    nck = time // chunk
    nh = N_REC_HEADS
    z3 = z.reshape(batch, time, Z_WIDTH)
    cb = lambda col: pl.BlockSpec((bb, chunk, REC_WIDTH), lambda b, c: (b, c, col // REC_WIDTH))
    full = lambda a: pl.BlockSpec(a.shape, lambda b, c: (0,) * a.ndim)
    state = pl.BlockSpec((bb, nh, k_dim, HEAD_SLAB), lambda b, c: (b, 0, 0, 0))
    tri = jnp.tril(jnp.ones((chunk, chunk), F32)).astype(BF16)
    if gla:
        wg, bg = extra
        ins = [z3, z3, z3, z3, wg, bg, tri]
        specs = [cb(Z_GQ), cb(Z_GK), cb(Z_GV),
                 pl.BlockSpec((bb, chunk, LANE), lambda b, c: (b, c, Z_GLOW // LANE)), full(wg), full(bg), full(tri)]
    else:
        la, l1 = extra
        ins = [z3, z3, z3, la, l1, tri]
        specs = [cb(Z_HQ), cb(Z_HF), cb(Z_HI), full(la), full(l1), full(tri)]
    if s0 is not None:
        ins.append(s0)
        specs.append(state)
    ins.append(stacked)
    specs.append(pl.BlockSpec(memory_space=pl.ANY))
    o, stacked = pl.pallas_call(
        functools.partial(_scan_body, gla=gla, chunk=chunk, sub=sub, valid=valid, has_s0=s0 is not None,
                          k_dim=k_dim, bb=bb),
        grid=(batch // bb, nck),
        in_specs=specs,
        out_specs=[pl.BlockSpec((bb, chunk, REC_WIDTH), lambda b, c: (b, c, 0)),
                   pl.BlockSpec((1, bb, nh, k_dim, HEAD_SLAB), lambda b, c: (layer, b, 0, 0, 0))],
        out_shape=[jax.ShapeDtypeStruct((batch, time, REC_WIDTH), F32),
                   jax.ShapeDtypeStruct(stacked.shape, F32)],
        input_output_aliases={len(ins) - 1: 1},
        scratch_shapes=[pltpu.VMEM((bb * nh, HEAD_SLAB, HEAD_SLAB), F32)],
        compiler_params=_cparams(("parallel", "arbitrary")),
        name="scan_gla" if gla else "scan_hgrn",
    )(*ins)
    return o.reshape(batch * time, REC_WIDTH), stacked


def _merge_body(x_ref, ym_ref, oh_ref, og_ref, hgate_ref, ggate_ref, br0_ref, br1_ref, br2_ref,
                g1_ref, sc2_ref, sh2_ref, hgw_ref, glw_ref, nfw_ref, wbm_ref, wbh_ref, wbg_ref, wout_ref,
                rwh_ref, rwl_ref, rb_ref, tri_ref, cnt0_ref, x1_ref, h2_ref, te_ref, tw_ref, rk_ref, cnt_ref,
                carry_scr):
    silu = lambda t: t * jax.nn.sigmoid(t)
    yh = _rms(oh_ref[...]) * hgw_ref[...] * silu(hgate_ref[...])
    og = og_ref[...]
    glw = glw_ref[...]
    yg = jnp.concatenate([_rms(og[:, h * GLA_DV:(h + 1) * GLA_DV]) * glw for h in range(GLA_HEADS)], axis=1)
    yg = yg * silu(ggate_ref[...])
    m = (jax.nn.sigmoid(br0_ref[...]) * _dot(ym_ref[...].astype(BF16), wbm_ref[...])
         + jax.nn.sigmoid(br1_ref[...]) * _dot(yh.astype(BF16), wbh_ref[...])
         + jax.nn.sigmoid(br2_ref[...]) * _dot(yg.astype(BF16), wbg_ref[...]))
    x1 = x_ref[...] + g1_ref[0, 0] * _dot(m.astype(BF16), wout_ref[...])
    x1_ref[...] = x1
    h2 = _rms(x1) * nfw_ref[...] * (1.0 + sc2_ref[0, 0]) + sh2_ref[0, 0]
    h2_ref[...] = h2
    hh = h2.astype(BF16)
    hl = (h2 - hh.astype(F32)).astype(BF16)
    rwh = rwh_ref[...]
    logits = _dot(hh, rwh) + _dot(hh, rwl_ref[...]) + _dot(hl, rwh) + rb_ref[...]

    lane = lax.broadcasted_iota(jnp.int32, logits.shape, 1).astype(F32)
    work = jnp.where(lane < N_EXPERTS, logits, NEG_INF)
    vals, idxs, hots = [], [], []
    for _ in range(TOP_K):
        mx = jnp.max(work, axis=1, keepdims=True)
        idx = jnp.min(jnp.where(work == mx, lane, float(LANE)), axis=1, keepdims=True)
        hot = lane == idx
        work = jnp.where(hot, NEG_INF, work)
        vals.append(mx)
        idxs.append(idx)
        hots.append(hot)
    ex = [jnp.exp(v - vals[0]) for v in vals]
    den = ex[0] + ex[1] + ex[2] + ex[3]

    @pl.when(pl.program_id(0) == 0)
    def _():
        carry_scr[...] = cnt0_ref[...]

    cnt = sum(h.astype(F32) for h in hots)
    before = _dot(tri_ref[...], cnt.astype(BF16)) + carry_scr[...]
    te = jnp.zeros(logits.shape, F32)
    tw = jnp.zeros(logits.shape, F32)
    rk = jnp.zeros(logits.shape, F32)
    for k in range(TOP_K):
        sel = lane == float(k)
        te = jnp.where(sel, idxs[k], te)
        tw = jnp.where(sel, ex[k] / den, tw)
        rk = jnp.where(sel, jnp.sum(jnp.where(hots[k], before, 0.0), axis=1, keepdims=True), rk)
    te_ref[...] = te.astype(jnp.int32)
    tw_ref[...] = tw
    rk_ref[...] = rk.astype(jnp.int32)
    carry_scr[...] = carry_scr[...] + jnp.sum(cnt, axis=0, keepdims=True)
    cnt_ref[...] = carry_scr[...]


def _merge(grp, layer, x, y_mla, o_hg, o_gla, z, lw, cnt0):
    tm = grp.tile(256)
    tri = jnp.tril(jnp.ones((tm, tm), F32), k=-1).astype(BF16)
    row = lambda w: pl.BlockSpec((tm, w), lambda i: (i, 0))
    zc = lambda col, w: pl.BlockSpec((tm, w), lambda i: (i, col // w))
    full = lambda a: pl.BlockSpec(a.shape, lambda i: (0,) * a.ndim)
    ws = [lw["hg_norm_w"], lw["gla_norm_w"], lw["norm_ffn_w"], lw["w_br_mla"], lw["w_br_hg"], lw["w_br_gla"],
          lw["w_out"], lw["rw_hi"], lw["rw_lo"], lw["rb"], tri, cnt0]
    n = grp.n_tok
    one = pl.BlockSpec((1, LANE), lambda i: (0, 0))
    return pl.pallas_call(
        _merge_body, grid=(grp.n_tok // tm,),
        in_specs=[row(D_MODEL), row(512), row(512), row(512), zc(Z_HGATE, 512), zc(Z_GGATE, 512),
                  zc(Z_BR, 1024), zc(Z_BR + 1024, 1024), zc(Z_BR + 2048, 1024),
                  grp.mod_spec(layer, MOD_GATE1, tm), grp.mod_spec(layer, MOD_SCALE2, tm),
                  grp.mod_spec(layer, MOD_SHIFT2, tm)] + [full(w) for w in ws],
        out_specs=[row(D_MODEL), row(D_MODEL), row(LANE), row(LANE), row(LANE), one],
        out_shape=[jax.ShapeDtypeStruct((n, D_MODEL), F32), jax.ShapeDtypeStruct((n, D_MODEL), F32),
                   jax.ShapeDtypeStruct((n, LANE), jnp.int32), jax.ShapeDtypeStruct((n, LANE), F32),
                   jax.ShapeDtypeStruct((n, LANE), jnp.int32), jax.ShapeDtypeStruct((1, LANE), F32)],
        scratch_shapes=[pltpu.VMEM((1, LANE), F32)],
        compiler_params=_cparams(("arbitrary",)), name="merge",
    )(x, y_mla, o_hg, o_gla, z, z, z, z, z, grp.mod, grp.mod, grp.mod, *ws)


def _experts_body(be_ref, grp_ref, nxt_ref, nu_ref, x_ref, bgu_ref, bd_ref, wgu_hbm, wd_hbm, y_ref,
                  wgu_buf, wd_buf, wgu_bf, wd_bf, sem, *, layer):
    i = pl.program_id(0)
    e = be_ref[i]
    slot = grp_ref[i] % 2
    first = jnp.logical_or(i == 0, e != be_ref[jnp.maximum(i - 1, 0)])

    def fetch(expert, s):
        w = layer * N_EXPERTS + expert
        return (pltpu.make_async_copy(wgu_hbm.at[w], wgu_buf.at[s], sem.at[0, s]),
                pltpu.make_async_copy(wd_hbm.at[w], wd_buf.at[s], sem.at[1, s]))

    @pl.when(i == 0)
    def _():
        for c in fetch(e, 0):
            c.start()

    @pl.when(first)
    def _():
        for c in fetch(e, slot):
            c.wait()

        @pl.when(nxt_ref[i] >= 0)
        def _():
            for c in fetch(nxt_ref[i], 1 - slot):
                c.start()

        wgu_bf[...] = wgu_buf[slot].astype(BF16)
        wd_bf[...] = wd_buf[slot].astype(BF16)

    @pl.when(i < nu_ref[0])
    def _():
        gu = _dot(x_ref[...].astype(BF16), wgu_bf[...]) + bgu_ref[0]
        gate = jnp.minimum(gu[:, :D_EXPERT], SWIGLU_LIMIT)
        up = jnp.clip(gu[:, D_EXPERT:], -SWIGLU_LIMIT, SWIGLU_LIMIT)
        act = (up + 1.0) * gate * jax.nn.sigmoid(SWIGLU_ALPHA * gate)
        y_ref[...] = _dot(act.astype(BF16), wd_bf[...]) + bd_ref[0]

    @pl.when(i >= nu_ref[0])
    def _():
        y_ref[...] = jnp.zeros(y_ref.shape, F32)


def _experts(block_e, block_grp, block_nxt, n_used, xb, w_gate_up, b_gate_up, w_down, b_down, layer):
    n_slots = xb.shape[0]
    tm = MOE_TM
    ne = N_EXPERTS
    bias = lambda w: pl.BlockSpec((1, 1, w), lambda i, be, grp, nxt, nu: (layer * ne + be[i], 0, 0))
    return pl.pallas_call(
        functools.partial(_experts_body, layer=layer),
        grid_spec=pltpu.PrefetchScalarGridSpec(
            num_scalar_prefetch=4, grid=(n_slots // tm,),
            in_specs=[pl.BlockSpec((tm, D_MODEL), lambda i, be, grp, nxt, nu: (i, 0)),
                      bias(2 * D_EXPERT), bias(D_MODEL),
                      pl.BlockSpec(memory_space=pl.ANY), pl.BlockSpec(memory_space=pl.ANY)],
            out_specs=pl.BlockSpec((tm, D_MODEL), lambda i, be, grp, nxt, nu: (i, 0)),
            scratch_shapes=[pltpu.VMEM((2, D_MODEL, 2 * D_EXPERT), F32), pltpu.VMEM((2, D_EXPERT, D_MODEL), F32),
                            pltpu.VMEM((D_MODEL, 2 * D_EXPERT), BF16), pltpu.VMEM((D_EXPERT, D_MODEL), BF16),
                            pltpu.SemaphoreType.DMA((2, 2))]),
        out_shape=jax.ShapeDtypeStruct((n_slots, D_MODEL), F32),
        compiler_params=_cparams(("arbitrary",)),
        name="experts",
    )(block_e, block_grp, block_nxt, n_used, xb,
      b_gate_up.reshape(DEPTH * ne, 1, 2 * D_EXPERT), b_down.reshape(DEPTH * ne, 1, D_MODEL),
      w_gate_up.reshape(DEPTH * ne, D_MODEL, 2 * D_EXPERT), w_down.reshape(DEPTH * ne, D_EXPERT, D_MODEL))


def _route(top_e, rank, counts):
    n = top_e.shape[0]
    nk = n * TOP_K
    tm = MOE_TM
    experts = jnp.arange(N_EXPERTS, dtype=jnp.int32)
    padded = (counts + tm - 1) // tm * tm
    pad_end = jnp.cumsum(padded)
    pad_start = pad_end - padded
    start = jnp.cumsum(counts) - counts
    dest = jnp.sum(jnp.where(top_e[..., None] == experts, pad_start, 0), axis=-1) + rank
    flat_tok = jnp.arange(nk, dtype=jnp.int32) // TOP_K
    _, stok = lax.sort((dest.reshape(-1), flat_tok), num_keys=1)
    n_blocks = (nk + N_EXPERTS * (tm - 1) + tm - 1) // tm
    block_lo = jnp.arange(n_blocks, dtype=jnp.int32) * tm
    block_e = jnp.minimum(jnp.sum((pad_end[None, :] <= block_lo[:, None]).astype(jnp.int32), axis=1),
                          N_EXPERTS - 1)
    off = (block_lo - pad_start[block_e])[:, None] + jnp.arange(tm, dtype=jnp.int32)[None, :]
    live = off < counts[block_e][:, None]
    src = jnp.clip(start[block_e][:, None] + off, 0, nk - 1)
    slot_tok = jnp.where(live, stok[src.reshape(-1)].reshape(n_blocks, tm), 0).reshape(-1)
    n_used = (pad_end[-1] // tm).astype(jnp.int32).reshape(1)
    block_e = block_e.astype(jnp.int32)
    first = jnp.concatenate([jnp.ones((1,), jnp.bool_), block_e[1:] != block_e[:-1]])
    block_grp = jnp.cumsum(first.astype(jnp.int32)) - 1
    runs = jnp.arange(n_blocks, dtype=jnp.int32)
    run_e = jnp.sum(jnp.where(first[:, None] & (block_grp[:, None] == runs[None, :]), block_e[:, None], 0), axis=0)
    following = jnp.minimum(block_grp + 1, n_blocks - 1)
    block_nxt = jnp.where(block_grp + 1 <= block_grp[-1], run_e[following], -1).astype(jnp.int32)
    return slot_tok, (block_e, block_grp, block_nxt), n_used, dest


def _final_norm_body(x_ref, w_ref, o_ref):
    o_ref[...] = _rms(x_ref[...]) * w_ref[...]


def _final_norm(x, w, tm):
    n = x.shape[0]
    return pl.pallas_call(
        _final_norm_body, grid=(n // tm,),
        in_specs=[pl.BlockSpec((tm, D_MODEL), lambda i: (i, 0)), pl.BlockSpec((1, D_MODEL), lambda i: (0, 0))],
        out_specs=pl.BlockSpec((tm, D_MODEL), lambda i: (i, 0)),
        out_shape=jax.ShapeDtypeStruct((n, D_MODEL), F32),
        compiler_params=_cparams(("parallel",)), name="final_norm",
    )(x, w.reshape(1, D_MODEL))


def kernel(x_prompt, x_sample, cache_latent, cache_k_rope, state_hgrn, state_gla, page_table, c_prompt, c_sample, norm_mix_w, norm_ffn_w, final_norm_w, w_ada, b_ada, w_in, mla_q_norm_w, mla_w_uq, mla_kv_norm_w, mla_w_uk, mla_w_uv, hgrn_lower_bounds, hgrn_norm_w, gla_w_gate, gla_b_gate, gla_norm_w, w_branch_mla, w_branch_hgrn, w_branch_gla, w_out, router_w, router_b, w_gate_up, b_gate_up, w_down, b_down):
    bp, seq, _ = x_prompt.shape
    bs, tnew, _ = x_sample.shape
    n_p, n_s = bp * seq, bs * tnew
    past_len = page_table.shape[1] * cache_latent.shape[2]
    s_pad = SCAN_SUB

    rows = bp + bs
    rows_pad = -(-rows // 8) * 8
    c_all = jnp.pad(jnp.concatenate([c_prompt, c_sample], axis=0), ((0, rows_pad - rows), (0, 0)))
    mod = _adaln(c_all, w_ada, b_ada)
    gp = _Group(bp, seq, per_token=False, mod=mod[:, :bp])
    gs = _Group(bs, tnew, per_token=True, mod=mod[:, bp:rows])

    lbs = jax.nn.softmax(hgrn_lower_bounds.astype(F32), axis=0)
    lbs = jnp.cumsum(lbs, axis=0) - lbs[0]
    log_lb = jnp.log(jnp.maximum(lbs, TINY))
    log_1m_lb = jnp.log1p(-lbs)

    ct_p, st_p = _rope_tables(jnp.arange(seq, dtype=jnp.int32))
    pos_s = past_len + jnp.arange(tnew, dtype=jnp.int32)
    ct_s, st_s = _rope_tables(jnp.tile(pos_s, bs))

    xp = x_prompt.reshape(n_p, D_MODEL)
    xs = x_sample.reshape(n_s, D_MODEL)
    lat_p, kpe_p, lat_s, kpe_s = [], [], [], []
    state_buf = lambda b, k: jnp.zeros((DEPTH, b, N_REC_HEADS, k, HEAD_SLAB), F32)
    hg_p, gla_p, hg_s, gla_s = state_buf(bp, HG_DK), state_buf(bp, GLA_DK), state_buf(bs, HG_DK), state_buf(bs, GLA_DK)

    for l in range(DEPTH):
        w_pad = _pad_w_in(w_in[l])
        mw = _mla_weights(mla_w_uq[l], mla_w_uk[l], mla_w_uv[l])
        rw = jnp.pad(router_w[l], ((0, 0), (0, LANE - N_EXPERTS)))
        rw_hi = rw.astype(BF16)
        lw = dict(hg_norm_w=hgrn_norm_w[l].reshape(1, -1), gla_norm_w=gla_norm_w[l].reshape(1, -1),
                  norm_ffn_w=norm_ffn_w[l].reshape(1, -1), w_br_mla=w_branch_mla[l].astype(BF16),
                  w_br_hg=w_branch_hgrn[l].astype(BF16), w_br_gla=w_branch_gla[l].astype(BF16),
                  w_out=w_out[l].astype(BF16), rw_hi=rw_hi, rw_lo=(rw - rw_hi.astype(F32)).astype(BF16),
                  rb=jnp.pad(router_b[l], (0, LANE - N_EXPERTS)).reshape(1, LANE))
        wg = jnp.pad(gla_w_gate[l].reshape(GLA_GATE_RANK, GLA_HEADS, GLA_DK),
                     ((0, LANE - GLA_GATE_RANK), (0, 0), (0, HEAD_SLAB - GLA_DK))).reshape(LANE, REC_WIDTH).astype(BF16)
        bg = jnp.pad(gla_b_gate[l].reshape(GLA_HEADS, GLA_DK), ((0, 0), (0, HEAD_SLAB - GLA_DK))).reshape(1, REC_WIDTH)
        hg_extra = (log_lb[l].reshape(1, -1), log_1m_lb[l].reshape(1, -1))
        g2p = mod[l, :bp, MOD_GATE2 * D_MODEL:]
        g2s = mod[l, bp:rows, MOD_GATE2 * D_MODEL:]

        zp = _in_proj(gp, l, xp, norm_mix_w[l], w_pad)
        q, k, v, latp, kpep = _mla_prep(gp, zp, mla_q_norm_w[l], mla_kv_norm_w[l], mw, ct_p, st_p, sample=False)
        y_mla_p = _flash(q, k, v, bp, seq, tq=512)
        o_hg_p, hg_p = _scan(zp, bp, seq, SCAN_CHUNK, SCAN_SUB, SCAN_CHUNK, False, hg_extra, None, bp, l, hg_p)
        o_gl_p, gla_p = _scan(zp, bp, seq, SCAN_CHUNK, SCAN_SUB, SCAN_CHUNK, True, (wg, bg), None, bp, l, gla_p)
        x1p, h2p, te_p, tw_p, rk_p, cnt_p = _merge(gp, l, xp, y_mla_p, o_hg_p, o_gl_p, zp, lw,
                                                   jnp.zeros((1, LANE), F32))

        zs = _in_proj(gs, l, xs, norm_mix_w[l], w_pad)
        qs, qlat, lats, kpes = _mla_prep(gs, zs, mla_q_norm_w[l], mla_kv_norm_w[l], mw, ct_s, st_s, sample=True)
        o_lat = _paged(page_table, qs, qlat, lats, kpes, cache_latent, cache_k_rope, l, bs, tnew)
        y_mla_s = _head_proj(o_lat, mla_w_uv[l])
        zs_pad = jnp.pad(zs.reshape(bs, tnew, Z_WIDTH), ((0, 0), (0, s_pad - tnew), (0, 0))).reshape(bs * s_pad, Z_WIDTH)
        unpad = lambda o: o.reshape(bs, s_pad, -1)[:, :tnew].reshape(n_s, -1)
        o_hg_s, hg_s = _scan(zs_pad, bs, s_pad, s_pad, SCAN_SUB, tnew, False, hg_extra, state_hgrn[l],
                             SAMPLE_ROWS_PER_STEP, l, hg_s)
        o_gl_s, gla_s = _scan(zs_pad, bs, s_pad, s_pad, SCAN_SUB, tnew, True, (wg, bg), state_gla[l],
                              SAMPLE_ROWS_PER_STEP, l, gla_s)
        x1s, h2s, te_s, tw_s, rk_s, cnt_all = _merge(gs, l, xs, y_mla_s, unpad(o_hg_s), unpad(o_gl_s), zs, lw, cnt_p)

        h2 = jnp.concatenate([h2p, h2s], axis=0)
        both = lambda a, b: jnp.concatenate([a, b], axis=0)[:, :TOP_K]
        top_w = both(tw_p, tw_s)
        slot_tok, block_e, n_used, dest = _route(both(te_p, te_s), both(rk_p, rk_s),
                                                 cnt_all[0, :N_EXPERTS].astype(jnp.int32))
        yb = _experts(*block_e, n_used, h2[slot_tok], w_gate_up, b_gate_up, w_down, b_down, l)
        combine = lambda w, d: sum(w[:, k:k + 1] * yb[d[:, k]] for k in range(TOP_K))
        xp = x1p + jnp.repeat(g2p, seq, axis=0) * combine(top_w[:n_p], dest[:n_p])
        xs = x1s + jnp.repeat(g2s, tnew, axis=0) * combine(top_w[n_p:], dest[n_p:])

        sl = slice(MLA_NOPE, MLA_NOPE + MLA_ROPE)
        lat_p.append(latp.reshape(bp, seq, -1)); kpe_p.append(kpep[:, sl].reshape(bp, seq, -1))
        lat_s.append(lats.reshape(bs, tnew, -1)); kpe_s.append(kpes[:, sl].reshape(bs, tnew, -1))

    y_prompt = _final_norm(xp, final_norm_w, 1024).reshape(bp, seq, D_MODEL)
    y_sample = _final_norm(xs, final_norm_w, n_s).reshape(bs, tnew, D_MODEL)
    st = jnp.stack
    return (y_prompt, y_sample, st(lat_p), st(kpe_p), hg_p, gla_p, st(lat_s), st(kpe_s), hg_s, gla_s)
```
